```python
import jax, jax.numpy as jnp
from jax import lax
import numpy as np

D_MODEL = 1024
BATCH = 8
SEQ = 4096
DEPTH = 2

HEAD_DIM = 64
N_HEADS_A = 12
N_KV_A = 4
IDX_HEADS = 8
IDX_DIM = 64
IDX_ROPE_DIM = 32
TOPK_MAX = 256
DIL_PATTERNS = ((128, 1), (512, 4), (2048, 16))
HEADS_PER_DIL = 4
N_MEM_HEADS = 4
MEM_TOKENS = 256
D_FF = ((8 * D_MODEL // 3 + 255) // 256) * 256
BLK = 128
ROPE_THETA = 10000.0
EPS = 1e-6
NEG = -1e30

A_SIZES = (N_HEADS_A * HEAD_DIM, N_KV_A * HEAD_DIM, N_KV_A * HEAD_DIM,
           IDX_HEADS * IDX_DIM, IDX_DIM, IDX_HEADS, N_MEM_HEADS * HEAD_DIM)
A_IN = sum(A_SIZES)
A_OUT = (N_HEADS_A + N_MEM_HEADS) * HEAD_DIM
B_SIZES = (HEADS_PER_DIL * HEAD_DIM,) * (3 * len(DIL_PATTERNS)) + (N_MEM_HEADS * HEAD_DIM,)
B_IN = sum(B_SIZES)
B_OUT = (HEADS_PER_DIL + N_MEM_HEADS) * HEAD_DIM

kernel_name = 'hybrid_dsa_dilated_decoder'


def _split(t, sizes):
    offs = np.cumsum(sizes)[:-1].tolist()
    return jnp.split(t, offs, axis=-1)


def rms_norm(x, g):
    xf = x.astype(jnp.float32)
    y = xf * lax.rsqrt(jnp.mean(xf * xf, axis=-1, keepdims=True) + EPS)
    return (y * g.astype(jnp.float32)).astype(x.dtype)


def rope(x, pos, rot_dim):
    half = rot_dim // 2
    freqs = ROPE_THETA ** (-jnp.arange(half, dtype=jnp.float32) / half)
    ang = pos.astype(jnp.float32)[:, :, None, None] * freqs
    cos, sin = jnp.cos(ang), jnp.sin(ang)
    xr = x[..., :rot_dim].astype(jnp.float32)
    x1, x2 = xr[..., :half], xr[..., half:]
    rot = jnp.concatenate([x1 * cos - x2 * sin, x2 * cos + x1 * sin], axis=-1).astype(x.dtype)
    return jnp.concatenate([rot, x[..., rot_dim:]], axis=-1)


def dsa_mixer(proj, pos):
    bsz, seq, _ = proj.shape
    q, k, v, qi, ki, wi, qm = _split(proj, A_SIZES)
    q = rope(q.reshape(bsz, seq, N_HEADS_A, HEAD_DIM), pos, HEAD_DIM)
    k = rope(k.reshape(bsz, seq, N_KV_A, HEAD_DIM), pos, HEAD_DIM)
    v = v.reshape(bsz, seq, N_KV_A, HEAD_DIM)
    qi = rope(qi.reshape(bsz, seq, IDX_HEADS, IDX_DIM), pos, IDX_ROPE_DIM)
    ki = rope(ki.reshape(bsz, seq, 1, IDX_DIM), pos, IDX_ROPE_DIM)[:, :, 0]
    wi = wi.astype(jnp.float32) * (IDX_HEADS ** -0.5 * IDX_DIM ** -0.5)
    n_sel = min(TOPK_MAX, seq // 4)
    grp = N_HEADS_A // N_KV_A
    scale = HEAD_DIM ** -0.5
    key_pos = jnp.arange(seq)
    b_idx = jnp.arange(bsz)[:, None, None]

    def block(i):
        start = i * BLK
        qb = lax.dynamic_slice_in_dim(q, start, BLK, axis=1)
        qib = lax.dynamic_slice_in_dim(qi, start, BLK, axis=1)
        wib = lax.dynamic_slice_in_dim(wi, start, BLK, axis=1)
        t_pos = start + jnp.arange(BLK)
        logits = jnp.einsum('bqhd,bsd->bqhs', qib, ki, preferred_element_type=jnp.float32)
        score = jnp.einsum('bqhs,bqh->bqs', jax.nn.relu(logits), wib)
        score = jnp.where(key_pos[None, None, :] <= t_pos[None, :, None], score, NEG)
        _, idx = lax.top_k(score, n_sel)
        k_sel = k[b_idx, idx]
        v_sel = v[b_idx, idx]
        qg = qb.reshape(bsz, BLK, N_KV_A, grp, HEAD_DIM)
        s = jnp.einsum('bqkgd,bqskd->bqkgs', qg, k_sel, preferred_element_type=jnp.float32) * scale
        valid = (idx <= t_pos[None, :, None])[:, :, None, None, :]
        p = jax.nn.softmax(jnp.where(valid, s, NEG), axis=-1)
        o = jnp.einsum('bqkgs,bqskd->bqkgd', p, v_sel.astype(jnp.float32))
        return o.reshape(bsz, BLK, N_HEADS_A * HEAD_DIM).astype(proj.dtype)

    out = lax.map(block, jnp.arange(seq // BLK))
    out = out.transpose(1, 0, 2, 3).reshape(bsz, seq, N_HEADS_A * HEAD_DIM)
    return out, qm


def dilated_group(q, k, v, window, dil):
    bsz, seq, nh, hd = q.shape
    steps = window // dil
    n = seq // dil
    nb = -(-n // BLK)
    n_pad = nb * BLK

    def to_sub(t, front):
        t = t.reshape(bsz, n, dil, nh, hd).transpose(0, 2, 3, 1, 4)
        return jnp.pad(t, ((0, 0), (0, 0), (0, 0), (front, n_pad - n), (0, 0)))

    qs = to_sub(q, 0).reshape(bsz, dil, nh, nb, BLK, hd)
    kb = to_sub(k, BLK).reshape(bsz, dil, nh, nb + 1, BLK, hd)
    vb = to_sub(v, BLK).reshape(bsz, dil, nh, nb + 1, BLK, hd)
    kw = jnp.concatenate([kb[:, :, :, :-1], kb[:, :, :, 1:]], axis=4)
    vw = jnp.concatenate([vb[:, :, :, :-1], vb[:, :, :, 1:]], axis=4)
    s = jnp.einsum('brhnqd,brhnkd->brhnqk', qs, kw, preferred_element_type=jnp.float32) * (hd ** -0.5)
    qi = jnp.arange(BLK)[None, :, None]
    kj = jnp.arange(2 * BLK)[None, None, :]
    blk = jnp.arange(nb)[:, None, None]
    dist = qi + BLK - kj
    valid = (dist >= 0) & (dist <= steps) & (blk * BLK + kj - BLK >= 0)
    s = jnp.where(valid, s, NEG)
    m = jnp.max(s, axis=-1, keepdims=True)
    p = jnp.exp(s - m)
    den = jnp.sum(p, axis=-1, keepdims=True)
    o = jnp.einsum('brhnqk,brhnkd->brhnqd', p, vw.astype(jnp.float32)) / den
    lse = m[..., 0] + jnp.log(den[..., 0])
    o = o.reshape(bsz, dil, nh, n_pad, hd)[:, :, :, :n].transpose(0, 3, 1, 2, 4).reshape(bsz, seq, nh, hd)
    lse = lse.reshape(bsz, dil, nh, n_pad)[..., :n].transpose(0, 3, 1, 2).reshape(bsz, seq, nh)
    return o, lse


def dilated_mixer(proj, pos):
    bsz, seq, _ = proj.shape
    parts = _split(proj, B_SIZES)
    outs, lses = [], []
    for g, (window, dil) in enumerate(DIL_PATTERNS):
        q, k, v = [parts[3 * g + j].reshape(bsz, seq, HEADS_PER_DIL, HEAD_DIM) for j in range(3)]
        o, lse = dilated_group(rope(q, pos, HEAD_DIM), rope(k, pos, HEAD_DIM), v, window, dil)
        outs.append(o)
        lses.append(lse)
    wts = jax.nn.softmax(jnp.stack(lses, axis=-1), axis=-1)
    o = jnp.sum(jnp.stack(outs, axis=-2) * wts[..., None], axis=-2)
    return o.reshape(bsz, seq, HEADS_PER_DIL * HEAD_DIM).astype(proj.dtype), parts[-1]


def memory_attention(qm, mem, g_mem, w_mem_kv):
    bsz, seq, _ = qm.shape
    kv = (rms_norm(mem, g_mem) @ w_mem_kv).reshape(bsz, mem.shape[1], 2, N_MEM_HEADS, HEAD_DIM)
    q = qm.reshape(bsz, seq, N_MEM_HEADS, HEAD_DIM)
    s = jnp.einsum('bshd,bmhd->bhsm', q, kv[:, :, 0], preferred_element_type=jnp.float32) * (HEAD_DIM ** -0.5)
    p = jax.nn.softmax(s, axis=-1)
    o = jnp.einsum('bhsm,bmhd->bshd', p, kv[:, :, 1].astype(jnp.float32))
    return o.reshape(bsz, seq, N_MEM_HEADS * HEAD_DIM).astype(qm.dtype)


def swiglu_ffn(h, w_gate_up, w_down):
    g, u = jnp.split(h @ w_gate_up, 2, axis=-1)
    return (jax.nn.silu(g) * u) @ w_down


def hybrid_layer(x, mem, pos, layer_idx, norm_mix, norm_mem, w_in, w_mem_kv, w_out,
                 norm_ffn, w_gate_up, w_down):
    proj = rms_norm(x, norm_mix) @ w_in
    if layer_idx % 2 == 0:
        mix, qm = dsa_mixer(proj, pos)
    else:
        mix, qm = dilated_mixer(proj, pos)
    mo = memory_attention(qm, mem, norm_mem, w_mem_kv)
    x = x + jnp.concatenate([mix, mo], axis=-1) @ w_out
    x = x + swiglu_ffn(rms_norm(x, norm_ffn), w_gate_up, w_down)
    return x


def setup_inputs(seed: int = 0) -> dict:
    key = jax.random.key(seed)
    ks = jax.random.split(key, 24)
    f32 = jnp.float32

    def dense(k, fi, fo):
        return jax.random.normal(k, (fi, fo), f32) * fi ** -0.5

    def gain(k):
        return 1.0 + 0.02 * jax.random.normal(k, (D_MODEL,), f32)

    x = jax.random.normal(ks[0], (BATCH, SEQ, D_MODEL), f32)
    mem = jax.random.normal(ks[1], (BATCH, MEM_TOKENS, D_MODEL), f32)
    offs = jax.random.randint(ks[2], (BATCH, 1), 0, 1024, dtype=jnp.int32)
    positions = (jnp.arange(SEQ, dtype=jnp.int32)[None, :] + offs).astype(jnp.int32)
    return {
        'x': x, 'mem': mem, 'positions': positions,
        'l0_norm_mix': gain(ks[3]), 'l0_norm_mem': gain(ks[4]),
        'l0_w_in': dense(ks[5], D_MODEL, A_IN),
        'l0_w_mem_kv': dense(ks[6], D_MODEL, 2 * N_MEM_HEADS * HEAD_DIM),
        'l0_w_out': dense(ks[7], A_OUT, D_MODEL),
        'l0_norm_ffn': gain(ks[8]),
        'l0_w_gate_up': dense(ks[9], D_MODEL, 2 * D_FF),
        'l0_w_down': dense(ks[10], D_FF, D_MODEL),
        'l1_norm_mix': gain(ks[11]), 'l1_norm_mem': gain(ks[12]),
        'l1_w_in': dense(ks[13], D_MODEL, B_IN),
        'l1_w_mem_kv': dense(ks[14], D_MODEL, 2 * N_MEM_HEADS * HEAD_DIM),
        'l1_w_out': dense(ks[15], B_OUT, D_MODEL),
        'l1_norm_ffn': gain(ks[16]),
        'l1_w_gate_up': dense(ks[17], D_MODEL, 2 * D_FF),
        'l1_w_down': dense(ks[18], D_FF, D_MODEL),
        'final_norm': gain(ks[19]),
    }


def reference(x, mem, positions,
              l0_norm_mix, l0_norm_mem, l0_w_in, l0_w_mem_kv, l0_w_out, l0_norm_ffn, l0_w_gate_up, l0_w_down,
              l1_norm_mix, l1_norm_mem, l1_w_in, l1_w_mem_kv, l1_w_out, l1_norm_ffn, l1_w_gate_up, l1_w_down,
              final_norm):
    layers = (
        (l0_norm_mix, l0_norm_mem, l0_w_in, l0_w_mem_kv, l0_w_out, l0_norm_ffn, l0_w_gate_up, l0_w_down),
        (l1_norm_mix, l1_norm_mem, l1_w_in, l1_w_mem_kv, l1_w_out, l1_norm_ffn, l1_w_gate_up, l1_w_down),
    )
    for i in range(DEPTH):
        x = hybrid_layer(x, mem, positions, i, *layers[i])
    return rms_norm(x, final_norm)
```

```python
import functools

import jax
import jax.numpy as jnp
from jax import lax
from jax.experimental import pallas as pl
from jax.experimental.pallas import tpu as pltpu

F32 = jnp.float32
BF16 = jnp.bfloat16
I32 = jnp.int32

HEAD_DIM = 64
N_HEADS_A = 12
N_KV_A = 4
IDX_HEADS = 8
IDX_DIM = 64
IDX_ROPE_DIM = 32
TOPK_MAX = 256
DIL_PATTERNS = ((128, 1), (512, 4), (2048, 16))
HEADS_PER_DIL = 4
N_MEM_HEADS = 4
BLK = 128
ROPE_THETA = 10000.0
EPS = 1e-6
NEG = -1e30
INT_MIN = -2147483648

LANES = 128
VMEM_LIMIT = 56 * 1024 * 1024

Q_SCALE = HEAD_DIM ** -0.5
WI_SCALE = IDX_HEADS ** -0.5 * IDX_DIM ** -0.5


def _dot(a, b):
    return jnp.dot(a, b, preferred_element_type=F32)


def _dot_nt(a, b):
    return lax.dot_general(a, b, (((1,), (1,)), ((), ())), preferred_element_type=F32)


def _params(*sem):
    return pltpu.CompilerParams(dimension_semantics=sem, vmem_limit_bytes=VMEM_LIMIT)


def _rms(x, g):
    ms = jnp.mean(x * x, axis=-1, keepdims=True)
    return x * lax.rsqrt(ms + EPS) * g


def _lane(shape):
    return lax.broadcasted_iota(I32, shape, 1)


def _rope_block(blk, cos_p, sin_p, half):
    first = (_lane(blk.shape) % (2 * half)) < half
    sw = jnp.where(first, pltpu.roll(blk, LANES - half, 1), pltpu.roll(blk, half, 1))
    return blk * cos_p + sw * sin_p


def _trig_kernel(pos_ref, rows_ref, chd_ref, shd_ref, cix_ref, six_ref):
    pos = pos_ref[...].astype(F32)
    a_hd = pos * rows_ref[0:1, :]
    a_ix = pos * rows_ref[2:3, :]
    chd_ref[...] = jnp.cos(a_hd)
    shd_ref[...] = jnp.sin(a_hd) * rows_ref[1:2, :]
    cix_ref[...] = jnp.cos(a_ix)
    six_ref[...] = jnp.sin(a_ix) * rows_ref[3:4, :]


def _trig_rows():
    lane = jnp.arange(LANES)
    h_hd = HEAD_DIM // 2
    f_hd = ROPE_THETA ** (-jnp.arange(h_hd, dtype=F32) / h_hd)
    h_ix = IDX_ROPE_DIM // 2
    f_ix = ROPE_THETA ** (-jnp.arange(h_ix, dtype=F32) / h_ix)
    in_head = lane % HEAD_DIM
    row_f_hd = f_hd[lane % h_hd]
    row_s_hd = jnp.where(in_head < h_hd, -1.0, 1.0).astype(F32)
    rot = in_head < IDX_ROPE_DIM
    row_f_ix = jnp.where(rot, f_ix[lane % h_ix], 0.0).astype(F32)
    row_s_ix = jnp.where(rot, jnp.where(in_head < h_ix, -1.0, 1.0), 0.0).astype(F32)
    z = jnp.zeros((LANES,), F32)
    return jnp.stack([row_f_hd, row_s_hd, row_f_ix, row_s_ix, z, z, z, z])


def _trig_tables(pos_col):
    t = pos_col.shape[0]
    tm = min(1024, t)
    tab = jax.ShapeDtypeStruct((t, LANES), F32)
    spec = pl.BlockSpec((tm, LANES), lambda i: (i, 0))
    return pl.pallas_call(
        _trig_kernel,
        grid=(t // tm,),
        in_specs=[pl.BlockSpec((tm, 1), lambda i: (i, 0)),
                  pl.BlockSpec((8, LANES), lambda i: (0, 0))],
        out_specs=[spec] * 4,
        out_shape=[tab] * 4,
        compiler_params=_params("parallel"),
        name="rope_tables",
    )(pos_col, _trig_rows())


def _norm_matmul_kernel(x_ref, g_ref, w_ref, o_ref):
    h = _rms(x_ref[...], g_ref[...]).astype(BF16)
    o_ref[...] = _dot(h, w_ref[...]).astype(o_ref.dtype)


def _norm_matmul(x, g, w, out_dtype):
    t, d = x.shape
    n = w.shape[1]
    tm = min(512, t)
    return pl.pallas_call(
        _norm_matmul_kernel,
        grid=(t // tm,),
        in_specs=[pl.BlockSpec((tm, d), lambda i: (i, 0)),
                  pl.BlockSpec((1, d), lambda i: (0, 0)),
                  pl.BlockSpec((d, n), lambda i: (0, 0))],
        out_specs=pl.BlockSpec((tm, n), lambda i: (i, 0)),
        out_shape=jax.ShapeDtypeStruct((t, n), out_dtype),
        compiler_params=_params("parallel"),
        name="norm_matmul",
    )(x, g.reshape(1, d), w)


def _mem_kv(mem, g, w_kv):
    b, m, d = mem.shape
    nh, hd = N_MEM_HEADS, HEAD_DIM
    wk = w_kv[:, :nh * hd]
    wv = w_kv[:, nh * hd:].reshape(d, nh, hd)
    z = jnp.zeros((d, hd), w_kv.dtype)
    cols = [wk]
    for h in range(nh):
        cols += ([wv[:, h], z] if h % 2 == 0 else [z, wv[:, h]])
    w = jnp.concatenate(cols, axis=1).astype(BF16)
    kv = _norm_matmul(mem.reshape(b * m, d), g, w, BF16)
    kv = kv.reshape(b, m, -1)
    return kv[:, :, :nh * hd], kv[:, :, nh * hd:]


def _v_aug_blocks(src):
    lane = _lane(src.shape)
    one = jnp.where(lane == HEAD_DIM, 1.0, 0.0).astype(F32)
    lo = lane < HEAD_DIM
    return jnp.where(lo, src, one), jnp.where(lo, pltpu.roll(src, HEAD_DIM, 1), one)


def _inproj_a_kernel(x_ref, g_ref, w_ref, chd_ref, shd_ref, cix_ref, six_ref,
                     q_ref, k_ref, va_ref, qi_ref, kib_ref, wif_ref, qm_ref):
    h = _rms(x_ref[...], g_ref[...]).astype(BF16)
    chd, shd = chd_ref[...], shd_ref[...]
    cix, six = cix_ref[...], six_ref[...]

    def proj(a, b):
        return _dot(h, w_ref[:, a:b])

    def blk(p, j):
        return p[:, j * LANES:(j + 1) * LANES]

    pq = proj(0, 768)
    for j in range(6):
        q_ref[:, j * LANES:(j + 1) * LANES] = (
            _rope_block(blk(pq, j), chd, shd, HEAD_DIM // 2) * Q_SCALE).astype(BF16)
    pk = proj(768, 1024)
    for j in range(2):
        k_ref[:, j * LANES:(j + 1) * LANES] = _rope_block(blk(pk, j), chd, shd, HEAD_DIM // 2).astype(BF16)
    pv = proj(1024, 1280)
    for j in range(2):
        ev, od = _v_aug_blocks(blk(pv, j))
        va_ref[:, (2 * j) * LANES:(2 * j + 1) * LANES] = ev.astype(BF16)
        va_ref[:, (2 * j + 1) * LANES:(2 * j + 2) * LANES] = od.astype(BF16)
    pqi = proj(1280, 1792)
    for j in range(4):
        qi_ref[:, j * LANES:(j + 1) * LANES] = _rope_block(blk(pqi, j), cix, six, IDX_ROPE_DIM // 2).astype(BF16)
    pkw = proj(1792, 1920)
    rot = _rope_block(pkw, cix, six, IDX_ROPE_DIM // 2)
    kw = jnp.where(_lane(pkw.shape) < IDX_DIM, rot, pkw * WI_SCALE)
    kib_ref[...] = kw.astype(BF16)
    wif_ref[...] = kw
    qm_ref[...] = (proj(1920, 2176) * Q_SCALE).astype(BF16)


def _inproj_a(x, g, w_in, tabs):
    t, d = x.shape
    pad = jnp.zeros((d, 1920 - 1864), w_in.dtype)
    w = jnp.concatenate([w_in[:, :1864], pad, w_in[:, 1864:]], axis=1).astype(BF16)
    n = w.shape[1]
    tm = min(512, t)
    row = lambda c: pl.BlockSpec((tm, c), lambda i: (i, 0))
    outs = [(768, BF16), (256, BF16), (512, BF16), (512, BF16), (128, BF16), (128, F32), (256, BF16)]
    return pl.pallas_call(
        _inproj_a_kernel,
        grid=(t // tm,),
        in_specs=[row(d), pl.BlockSpec((1, d), lambda i: (0, 0)), pl.BlockSpec((d, n), lambda i: (0, 0))]
                 + [row(LANES)] * 4,
        out_specs=[row(c) for c, _ in outs],
        out_shape=[jax.ShapeDtypeStruct((t, c), dt) for c, dt in outs],
        compiler_params=_params("parallel"),
        name="inproj_a",
    )(x, g.reshape(1, d), w, *tabs)


def _inproj_b_kernel(x_ref, g_ref, w_ref, chd_ref, shd_ref, *out_refs):
    h = _rms(x_ref[...], g_ref[...]).astype(BF16)
    chd, shd = chd_ref[...], shd_ref[...]
    ng = len(DIL_PATTERNS)
    for g in range(ng):
        q_ref, k_ref, va_ref = out_refs[3 * g:3 * g + 3]
        base = g * 768
        pq = _dot(h, w_ref[:, base:base + 256])
        pk = _dot(h, w_ref[:, base + 256:base + 512])
        pv = _dot(h, w_ref[:, base + 512:base + 768])
        for j in range(2):
            sl = slice(j * LANES, (j + 1) * LANES)
            q_ref[:, sl] = (_rope_block(pq[:, sl], chd, shd, HEAD_DIM // 2) * Q_SCALE).astype(BF16)
            k_ref[:, sl] = _rope_block(pk[:, sl], chd, shd, HEAD_DIM // 2).astype(BF16)
            ev, od = _v_aug_blocks(pv[:, sl])
            va_ref[:, (2 * j) * LANES:(2 * j + 1) * LANES] = ev.astype(BF16)
            va_ref[:, (2 * j + 1) * LANES:(2 * j + 2) * LANES] = od.astype(BF16)
    out_refs[3 * ng][...] = (_dot(h, w_ref[:, ng * 768:ng * 768 + 256]) * Q_SCALE).astype(BF16)


def _inproj_b(x, g, w_in, tabs):
    t, d = x.shape
    w = w_in.astype(BF16)
    n = w.shape[1]
    tm = min(512, t)
    row = lambda c: pl.BlockSpec((tm, c), lambda i: (i, 0))
    widths = [256, 256, 512] * len(DIL_PATTERNS) + [256]
    return pl.pallas_call(
        _inproj_b_kernel,
        grid=(t // tm,),
        in_specs=[row(d), pl.BlockSpec((1, d), lambda i: (0, 0)), pl.BlockSpec((d, n), lambda i: (0, 0)),
                  row(LANES), row(LANES)],
        out_specs=[row(c) for c in widths],
        out_shape=[jax.ShapeDtypeStruct((t, c), BF16) for c in widths],
        compiler_params=_params("parallel"),
        name="inproj_b",
    )(x, g.reshape(1, d), w, tabs[0], tabs[1])


TQ = 256
RH = 128


def _dsa_kernel(q_ref, qi_ref, wi_ref, ki_ref, k_ref, va_ref, o_ref,
                key_scr, tau_scr, cge_scr, xcut_scr, wb_scr, m_scr, acc_scr, *, seq, n_sel):
    i = pl.program_id(1)
    nch = i + 1
    nrep = TQ // LANES

    def tile(v):
        return jnp.concatenate([v] * nrep, axis=1)

    def chunk_off(c):
        return pl.multiple_of(c * TQ, TQ)

    for h in range(IDX_HEADS):
        wb_scr[h] = jnp.broadcast_to(wi_ref[:, IDX_DIM + h:IDX_DIM + h + 1], (TQ, LANES))

    def score_chunk(c, diag):
        off = chunk_off(c)
        kic = ki_ref[pl.ds(off, TQ), 0:IDX_DIM]
        sc = jnp.zeros((TQ, TQ), F32)
        for h in range(IDX_HEADS):
            lg = _dot_nt(qi_ref[:, h * IDX_DIM:(h + 1) * IDX_DIM], kic)
            sc = sc + jnp.maximum(lg, 0.0) * tile(wb_scr[h])
        bits = pltpu.bitcast(sc, I32)
        key = jnp.where(bits < 0, bits ^ jnp.int32(0x7FFFFFFF), bits)
        if diag:
            row = lax.broadcasted_iota(I32, (TQ, TQ), 0)
            key = jnp.where(_lane((TQ, TQ)) > row, jnp.int32(INT_MIN), key)
        key_scr[:, pl.ds(off, TQ)] = key

    def score_body(c, carry):
        score_chunk(c, False)
        return carry

    lax.fori_loop(0, i, score_body, 0)
    score_chunk(i, True)

    def count(r0, pred):
        def body(c, acc):
            off = chunk_off(c)
            kk = key_scr[r0:r0 + RH, pl.ds(off, TQ)]
            for cb in range(nrep):
                acc = acc + pred(kk[:, cb * LANES:(cb + 1) * LANES], off + cb * LANES)
            return acc
        acc = lax.fori_loop(0, nch, body, jnp.zeros((RH, LANES), F32))
        return jnp.sum(acc, axis=1, keepdims=True)

    @pl.when(i * TQ < n_sel)
    def _():
        tau_scr[...] = jnp.full((TQ, LANES), INT_MIN, I32)
        xcut_scr[...] = jnp.full((TQ, LANES), -1, I32)

    @pl.when(i * TQ >= n_sel)
    def _():
        tau_scr[...] = jnp.full((TQ, LANES), INT_MIN, I32)
        cge_scr[...] = jnp.full((TQ, LANES), 2.0 * n_sel, F32)
        xcut_scr[...] = jnp.full((TQ, LANES), seq, I32)

        def bit_body(step, carry):
            bitval = jnp.left_shift(jnp.int32(1), 31 - step)
            for r0 in range(0, TQ, RH):
                tau = tau_scr[r0:r0 + RH, :]
                cand = tau + bitval
                cnt = count(r0, lambda kk, _: jnp.where(kk >= cand, 1.0, 0.0))
                ok = cnt >= float(n_sel)
                tau_scr[r0:r0 + RH, :] = jnp.where(ok, cand, tau)
                cge_scr[r0:r0 + RH, :] = jnp.where(ok, cnt, cge_scr[r0:r0 + RH, :])
            return carry

        lax.fori_loop(0, 32, bit_body, 0)

        @pl.when(jnp.max(cge_scr[...]) > float(n_sel))
        def _():
            lane = _lane((RH, LANES))
            nbits = (seq - 1).bit_length()
            for r0 in range(0, TQ, RH):
                tau = tau_scr[r0:r0 + RH, :]
                cgt = count(r0, lambda kk, _: jnp.where(kk > tau, 1.0, 0.0))
                need = float(n_sel) - cgt

                def xbit_body(step, x):
                    cand = x + jnp.left_shift(jnp.int32(1), nbits - 1 - step)
                    below = count(r0, lambda kk, base: jnp.where(
                        kk == tau, jnp.where(lane + base < cand, 1.0, 0.0), 0.0))
                    return jnp.where(below < need, cand, x)

                x = lax.fori_loop(0, nbits, xbit_body, jnp.zeros((RH, LANES), I32))
                xcut_scr[r0:r0 + RH, :] = x

    m_scr[...] = jnp.full(m_scr.shape, NEG, F32)
    acc_scr[...] = jnp.zeros(acc_scr.shape, F32)
    grp = N_HEADS_A // N_KV_A

    def attn_body(c, carry):
        off = chunk_off(c)
        kk = key_scr[:, pl.ds(off, TQ)]
        tau = tile(tau_scr[...])
        idx = _lane((TQ, TQ)) + off
        bias = jnp.where(kk > tau, 0.0,
                         jnp.where(kk == tau, jnp.where(idx <= tile(xcut_scr[...]), 0.0, NEG), NEG))
        for g in range(N_KV_A):
            kc = k_ref[pl.ds(off, TQ), g * HEAD_DIM:(g + 1) * HEAD_DIM]
            vc = va_ref[pl.ds(off, TQ), g * LANES:(g + 1) * LANES]
            for j in range(grp):
                h = g * grp + j
                s = _dot_nt(q_ref[:, h * HEAD_DIM:(h + 1) * HEAD_DIM], kc) + bias
                m_old = m_scr[h]
                m_new = jnp.maximum(m_old, jnp.max(s, axis=1, keepdims=True))
                p = jnp.exp(s - tile(m_new))
                alpha = jnp.exp(m_old - m_new)
                acc_scr[h] = alpha * acc_scr[h] + _dot(p.astype(BF16), vc)
                m_scr[h] = m_new
        return carry

    lax.fori_loop(0, nch, attn_body, 0)

    lo = _lane((TQ, LANES)) < HEAD_DIM
    for hp in range(N_HEADS_A // 2):
        a0, a1 = acc_scr[2 * hp], acc_scr[2 * hp + 1]
        o0 = a0 / a0[:, HEAD_DIM:HEAD_DIM + 1]
        o1 = a1 / a1[:, HEAD_DIM:HEAD_DIM + 1]
        o_ref[:, hp * LANES:(hp + 1) * LANES] = jnp.where(lo, o0, pltpu.roll(o1, HEAD_DIM, 1)).astype(BF16)


def _dsa_attention(q, qi, wi, kib, k, va):
    b, s, _ = q.shape
    n_sel = min(TOPK_MAX, s // 4)
    assert s % TQ == 0 and n_sel % TQ == 0
    qblk = lambda c: pl.BlockSpec((None, TQ, c), lambda bi, i: (bi, i, 0))
    full = lambda c: pl.BlockSpec((None, s, c), lambda bi, i: (bi, 0, 0))
    return pl.pallas_call(
        functools.partial(_dsa_kernel, seq=s, n_sel=n_sel),
        grid=(b, s // TQ),
        in_specs=[qblk(768), qblk(512), qblk(128), full(128), full(256), full(512)],
        out_specs=qblk(768),
        out_shape=jax.ShapeDtypeStruct((b, s, 768), BF16),
        scratch_shapes=[
            pltpu.VMEM((TQ, s), I32),
            pltpu.VMEM((TQ, LANES), I32),
            pltpu.VMEM((TQ, LANES), F32),
            pltpu.VMEM((TQ, LANES), I32),
            pltpu.VMEM((IDX_HEADS, TQ, LANES), F32),
            pltpu.VMEM((N_HEADS_A, TQ, LANES), F32),
            pltpu.VMEM((N_HEADS_A, TQ, LANES), F32),
        ],
        compiler_params=_params("parallel", "arbitrary"),
        name="dsa_attention",
    )(q, qi, wi, kib, k, va)


def _mem_attn_kernel(q_ref, k_ref, v_ref, o_ref):
    nh = N_MEM_HEADS
    for hp in range(nh // 2):
        out = None
        for h in (2 * hp, 2 * hp + 1):
            sl = slice(h * HEAD_DIM, (h + 1) * HEAD_DIM)
            s = _dot_nt(q_ref[:, sl], k_ref[:, sl])
            p = jnp.exp(s - jnp.max(s, axis=1, keepdims=True))
            l = jnp.sum(p, axis=1, keepdims=True)
            o = _dot(p.astype(BF16), v_ref[:, h * LANES:(h + 1) * LANES]) / l
            out = o if out is None else out + o
        o_ref[:, hp * LANES:(hp + 1) * LANES] = out.astype(BF16)


def _mem_attention(qm, km, vm):
    b, s, c = qm.shape
    m = km.shape[1]
    tm = min(512, s)
    return pl.pallas_call(
        _mem_attn_kernel,
        grid=(b, s // tm),
        in_specs=[pl.BlockSpec((None, tm, c), lambda bi, i: (bi, i, 0)),
                  pl.BlockSpec((None, m, km.shape[2]), lambda bi, i: (bi, 0, 0)),
                  pl.BlockSpec((None, m, vm.shape[2]), lambda bi, i: (bi, 0, 0))],
        out_specs=pl.BlockSpec((None, tm, c), lambda bi, i: (bi, i, 0)),
        out_shape=jax.ShapeDtypeStruct((b, s, c), BF16),
        compiler_params=_params("parallel", "parallel"),
        name="mem_attention",
    )(qm, km, vm)


QB = 256


def _band_kernel(q_ref, kp_ref, kc_ref, vp_ref, vc_ref, o_ref, lse_ref, *, qb):
    j = pl.program_id(1)
    nsub = qb // BLK
    row = lax.broadcasted_iota(I32, (BLK, BLK), 0)
    col = _lane((BLK, BLK))
    bias_prev = jnp.where(col >= row, 0.0, NEG)
    bias_cur = jnp.where(col <= row, 0.0, NEG)
    lo = _lane((BLK, LANES)) < HEAD_DIM
    for sb in range(nsub):
        rows = slice(sb * BLK, (sb + 1) * BLK)
        if sb == 0:
            kprev, vprev = kp_ref, vp_ref
            prows = slice(0, BLK)
            bp = bias_prev + jnp.where(j > 0, 0.0, NEG)
        else:
            kprev, vprev = kc_ref, vc_ref
            prows = slice((sb - 1) * BLK, sb * BLK)
            bp = bias_prev
        for hp in range(HEADS_PER_DIL // 2):
            outs, lses = [], []
            for h in (2 * hp, 2 * hp + 1):
                hs = slice(h * HEAD_DIM, (h + 1) * HEAD_DIM)
                vs = slice(h * LANES, (h + 1) * LANES)
                qh = q_ref[rows, hs]
                s_p = _dot_nt(qh, kprev[prows, hs]) + bp
                s_c = _dot_nt(qh, kc_ref[rows, hs]) + bias_cur
                m = jnp.maximum(jnp.max(s_p, axis=1, keepdims=True), jnp.max(s_c, axis=1, keepdims=True))
                p_p = jnp.exp(s_p - m).astype(BF16)
                p_c = jnp.exp(s_c - m).astype(BF16)
                acc = _dot(p_p, vprev[prows, vs]) + _dot(p_c, vc_ref[rows, vs])
                l = acc[:, HEAD_DIM:HEAD_DIM + 1]
                outs.append(acc / l)
                lses.append(jnp.broadcast_to(m + jnp.log(l), (BLK, LANES)))
            o_ref[rows, hp * LANES:(hp + 1) * LANES] = jnp.where(
                lo, outs[0], pltpu.roll(outs[1], HEAD_DIM, 1)).astype(o_ref.dtype)
            lse_ref[rows, hp * LANES:(hp + 1) * LANES] = jnp.where(lo, lses[0], lses[1])


def _band_attention(q, k, va):
    ns, n, c = q.shape
    qb = min(QB, n)
    assert n % qb == 0 and qb % BLK == 0
    r = qb // BLK
    cur = lambda w: pl.BlockSpec((None, qb, w), lambda si, j: (si, j, 0))
    prev = lambda w: pl.BlockSpec((None, BLK, w), lambda si, j: (si, jnp.maximum(j * r - 1, 0), 0))
    return pl.pallas_call(
        functools.partial(_band_kernel, qb=qb),
        grid=(ns, n // qb),
        in_specs=[cur(c), prev(c), cur(c), prev(va.shape[2]), cur(va.shape[2])],
        out_specs=[cur(c), cur(c)],
        out_shape=[jax.ShapeDtypeStruct((ns, n, c), BF16), jax.ShapeDtypeStruct((ns, n, c), F32)],
        compiler_params=_params("parallel", "parallel"),
        name="band_attention",
    )(q, k, k, va, va)


def _merge_kernel(*refs):
    ng = len(DIL_PATTERNS)
    o_refs, l_refs, out_ref = refs[:ng], refs[ng:2 * ng], refs[2 * ng]
    lses = [r[...] for r in l_refs]
    m = functools.reduce(jnp.maximum, lses)
    es = [jnp.exp(l - m) for l in lses]
    num = sum(e * o[...].astype(F32) for e, o in zip(es, o_refs))
    out_ref[...] = (num / sum(es)).astype(out_ref.dtype)


def _merge_groups(os_, lses):
    t, c = os_[0].shape
    tm = min(1024, t)
    spec = pl.BlockSpec((tm, c), lambda i: (i, 0))
    return pl.pallas_call(
        _merge_kernel,
        grid=(t // tm,),
        in_specs=[spec] * (2 * len(os_)),
        out_specs=spec,
        out_shape=jax.ShapeDtypeStruct((t, c), BF16),
        compiler_params=_params("parallel"),
        name="merge_groups",
    )(*os_, *lses)


def _to_sub(a, dil):
    b, s, c = a.shape
    if dil == 1:
        return a
    return a.reshape(b, s // dil, dil, c).transpose(0, 2, 1, 3).reshape(b * dil, s // dil, c)


def _from_sub(a, dil, b):
    if dil == 1:
        return a
    ns, n, c = a.shape
    return a.reshape(b, dil, n, c).transpose(0, 2, 1, 3).reshape(b, n * dil, c)


def _ffn_kernel(x_ref, mix_ref, mo_ref, wo1_ref, wo2_ref, g_ref, wg_ref, wu_ref, wd_ref, gf_ref,
                o_ref, x2_scr, h_scr, acc_scr, *, final_norm):
    f = pl.program_id(1)

    @pl.when(f == 0)
    def _():
        x2 = x_ref[...] + _dot(mix_ref[...], wo1_ref[...]) + _dot(mo_ref[...], wo2_ref[...])
        x2_scr[...] = x2
        h_scr[...] = _rms(x2, g_ref[...]).astype(BF16)
        acc_scr[...] = jnp.zeros(acc_scr.shape, F32)

    h = h_scr[...]
    gate = _dot(h, wg_ref[...])
    up = _dot(h, wu_ref[...])
    act = (gate * jax.nn.sigmoid(gate) * up).astype(BF16)
    acc_scr[...] += _dot(act, wd_ref[...])

    @pl.when(f == pl.num_programs(1) - 1)
    def _():
        y = x2_scr[...] + acc_scr[...]
        if final_norm:
            y = _rms(y, gf_ref[...])
        o_ref[...] = y


def _out_ffn(x, mix, mo, w_out, g_ffn, w_gate_up, w_down, g_final, final_norm):
    t, d = x.shape
    cm, cmo = mix.shape[1], mo.shape[1]
    dff = w_down.shape[0]
    wo1 = w_out[:cm].astype(BF16)
    wo2 = w_out[cm:].astype(BF16)
    wgu = w_gate_up.astype(BF16)
    wd = w_down.astype(BF16)
    tm = min(512, t)
    tf = 256 if dff % 256 == 0 else dff
    nf = dff // tf
    row = lambda c: pl.BlockSpec((tm, c), lambda i, f: (i, 0))
    const = lambda r, c: pl.BlockSpec((r, c), lambda i, f: (0, 0))
    return pl.pallas_call(
        functools.partial(_ffn_kernel, final_norm=final_norm),
        grid=(t // tm, nf),
        in_specs=[row(d), row(cm), row(cmo), const(cm, d), const(cmo, d), const(1, d),
                  pl.BlockSpec((d, tf), lambda i, f: (0, f)),
                  pl.BlockSpec((d, tf), lambda i, f: (0, f + nf)),
                  pl.BlockSpec((tf, d), lambda i, f: (f, 0)),
                  const(1, d)],
        out_specs=row(d),
        out_shape=jax.ShapeDtypeStruct((t, d), F32),
        scratch_shapes=[pltpu.VMEM((tm, d), F32), pltpu.VMEM((tm, d), BF16), pltpu.VMEM((tm, d), F32)],
        compiler_params=_params("parallel", "arbitrary"),
        name="out_ffn",
    )(x, mix, mo, wo1, wo2, g_ffn.reshape(1, d), wgu, wgu, wd, g_final.reshape(1, d))


def kernel(x, mem, positions,
           l0_norm_mix, l0_norm_mem, l0_w_in, l0_w_mem_kv, l0_w_out, l0_norm_ffn, l0_w_gate_up, l0_w_down,
           l1_norm_mix, l1_norm_mem, l1_w_in, l1_w_mem_kv, l1_w_out, l1_norm_ffn, l1_w_gate_up, l1_w_down,
           final_norm):
    b, s, d = x.shape
    t = b * s
    xt = x.reshape(t, d)
    tabs = _trig_tables(positions.reshape(t, 1))

    q, k, va, qi, kib, wif, qm = _inproj_a(xt, l0_norm_mix, l0_w_in, tabs)
    r3 = lambda a: a.reshape(b, s, a.shape[-1])
    mix = _dsa_attention(r3(q), r3(qi), r3(wif), r3(kib), r3(k), r3(va))
    km, vm = _mem_kv(mem, l0_norm_mem, l0_w_mem_kv)
    mo = _mem_attention(r3(qm), km, vm)
    xt = _out_ffn(xt, mix.reshape(t, -1), mo.reshape(t, -1), l0_w_out, l0_norm_ffn,
                  l0_w_gate_up, l0_w_down, final_norm, False)

    outs = _inproj_b(xt, l1_norm_mix, l1_w_in, tabs)
    os_, lses = [], []
    for g, (window, dil) in enumerate(DIL_PATTERNS):
        assert window // dil == BLK
        qg, kg, vag = (_to_sub(r3(a), dil) for a in outs[3 * g:3 * g + 3])
        o, lse = _band_attention(qg, kg, vag)
        os_.append(_from_sub(o, dil, b).reshape(t, -1))
        lses.append(_from_sub(lse, dil, b).reshape(t, -1))
    mix = _merge_groups(os_, lses)
    km, vm = _mem_kv(mem, l1_norm_mem, l1_w_mem_kv)
    mo = _mem_attention(r3(outs[-1]), km, vm)
    xt = _out_ffn(xt, mix, mo.reshape(t, -1), l1_w_out, l1_norm_ffn,
                  l1_w_gate_up, l1_w_down, final_norm, True)
    return xt.reshape(b, s, d)
```

```python
import functools

import jax
import jax.numpy as jnp
from jax import lax
from jax.experimental import pallas as pl
from jax.experimental.pallas import tpu as pltpu

F32 = jnp.float32
BF16 = jnp.bfloat16
I32 = jnp.int32

HEAD_DIM = 64
N_HEADS_A = 12
N_KV_A = 4
IDX_HEADS = 8
IDX_DIM = 64
IDX_ROPE_DIM = 32
TOPK_MAX = 256
DIL_PATTERNS = ((128, 1), (512, 4), (2048, 16))
HEADS_PER_DIL = 4
N_MEM_HEADS = 4
BLK = 128
ROPE_THETA = 10000.0
EPS = 1e-6
NEG = -1e30
INT_MIN = -2147483648

LANES = 128
VMEM_LIMIT = 56 * 1024 * 1024

Q_SCALE = HEAD_DIM ** -0.5
WI_SCALE = IDX_HEADS ** -0.5 * IDX_DIM ** -0.5
LOG2E = 1.4426950408889634


def _dot(a, b):
    return jnp.dot(a, b, preferred_element_type=F32)


def _dot_nt(a, b):
    return lax.dot_general(a, b, (((1,), (1,)), ((), ())), preferred_element_type=F32)


def _params(*sem):
    return pltpu.CompilerParams(dimension_semantics=sem, vmem_limit_bytes=VMEM_LIMIT)


def _rms(x, g):
    ms = jnp.mean(x * x, axis=-1, keepdims=True)
    return x * lax.rsqrt(ms + EPS) * g


def _lane(shape):
    return lax.broadcasted_iota(I32, shape, 1)


def _rope_block(blk, cos_p, sin_p, half):
    first = (_lane(blk.shape) % (2 * half)) < half
    sw = jnp.where(first, pltpu.roll(blk, LANES - half, 1), pltpu.roll(blk, half, 1))
    return blk * cos_p + sw * sin_p


H_HD = HEAD_DIM // 2
H_IX = IDX_ROPE_DIM // 2


def _trig_kernel(pos_ref, f_ref, chd_ref, shd_ref, cix_ref, six_ref):
    tm = pos_ref.shape[1]
    pos = pos_ref[...].astype(F32)
    f = jnp.concatenate([f_ref[...]] * (tm // LANES), axis=1)
    ang = f * pos
    c, s = jnp.cos(ang), jnp.sin(ang)
    chd_ref[...] = c[:H_HD]
    shd_ref[...] = s[:H_HD]
    cix_ref[...] = c[H_HD:]
    six_ref[...] = s[H_HD:]


def _trig_tables(pos_row):
    t = pos_row.shape[1]
    tm = min(2048, t)
    f_hd = ROPE_THETA ** (-jnp.arange(H_HD, dtype=F32) / H_HD)
    f_ix = ROPE_THETA ** (-jnp.arange(H_IX, dtype=F32) / H_IX)
    f = jnp.broadcast_to(jnp.concatenate([f_hd, f_ix])[:, None], (H_HD + H_IX, LANES))
    spec = lambda r: pl.BlockSpec((r, tm), lambda i: (0, i))
    rows = [H_HD, H_HD, H_IX, H_IX]
    return pl.pallas_call(
        _trig_kernel,
        grid=(t // tm,),
        in_specs=[spec(1), pl.BlockSpec((H_HD + H_IX, LANES), lambda i: (0, 0))],
        out_specs=[spec(r) for r in rows],
        out_shape=[jax.ShapeDtypeStruct((r, t), F32) for r in rows],
        compiler_params=_params("parallel"),
        name="rope_tables",
    )(pos_row, f)


def _token_major_patterns(chd, shd):
    c = chd.T
    s = shd.T
    return jnp.tile(c, (1, LANES // H_HD)), jnp.tile(jnp.concatenate([-s, s], axis=1), (1, LANES // HEAD_DIM))


def _norm_matmul_kernel(x_ref, g_ref, w_ref, o_ref):
    h = _rms(x_ref[...], g_ref[...]).astype(BF16)
    o_ref[...] = _dot(h, w_ref[...]).astype(o_ref.dtype)


def _norm_matmul(x, g, w, out_dtype):
    t, d = x.shape
    n = w.shape[1]
    tm = min(512, t)
    return pl.pallas_call(
        _norm_matmul_kernel,
        grid=(t // tm,),
        in_specs=[pl.BlockSpec((tm, d), lambda i: (i, 0)),
                  pl.BlockSpec((1, d), lambda i: (0, 0)),
                  pl.BlockSpec((d, n), lambda i: (0, 0))],
        out_specs=pl.BlockSpec((tm, n), lambda i: (i, 0)),
        out_shape=jax.ShapeDtypeStruct((t, n), out_dtype),
        compiler_params=_params("parallel"),
        name="norm_matmul",
    )(x, g.reshape(1, d), w)


def _mem_kv(mem, g, w_kv):
    b, m, d = mem.shape
    nh, hd = N_MEM_HEADS, HEAD_DIM
    wk = w_kv[:, :nh * hd]
    wv = w_kv[:, nh * hd:].reshape(d, nh, hd)
    z = jnp.zeros((d, hd), w_kv.dtype)
    cols = [wk]
    for h in range(nh):
        cols += ([wv[:, h], z] if h % 2 == 0 else [z, wv[:, h]])
    w = jnp.concatenate(cols, axis=1).astype(BF16)
    kv = _norm_matmul(mem.reshape(b * m, d), g, w, BF16)
    kv = kv.reshape(b, m, -1)
    return kv[:, :, :nh * hd], kv[:, :, nh * hd:]


def _v_aug_blocks(src):
    lane = _lane(src.shape)
    one = jnp.where(lane == HEAD_DIM, 1.0, 0.0).astype(F32)
    lo = lane < HEAD_DIM
    return jnp.where(lo, src, one), jnp.where(lo, pltpu.roll(src, HEAD_DIM, 1), one)


VR = 80

A_Q, A_K, A_V, A_QI, A_KI, A_WI, A_QM, A_END = 0, 768, 1024, 1280, 1792, 1856, 1872, 2128


def _rope_rows(p, r0, half, c, s):
    x1, x2 = p[r0:r0 + half], p[r0 + half:r0 + 2 * half]
    return x1 * c - x2 * s, x2 * c + x1 * s


def _inproj_a_kernel(x_ref, g_ref, wt_ref, chd_ref, shd_ref, cix_ref, six_ref,
                     q_ref, k_ref, va_ref, qi_ref, ki_ref, wi_ref, qm_ref):
    h = _rms(x_ref[...], g_ref[...]).astype(BF16)
    chd, shd = chd_ref[...], shd_ref[...]
    cix, six = cix_ref[...], six_ref[...]
    tm = h.shape[0]

    def proj(a, b):
        return _dot_nt(wt_ref[a:b, :], h)

    def rope_heads(p, nheads, out_ref, half, c, s, scale):
        for hh in range(nheads):
            r0 = hh * HEAD_DIM
            o1, o2 = _rope_rows(p, r0, half, c, s)
            out_ref[r0:r0 + half, :] = (o1 * scale).astype(BF16)
            out_ref[r0 + half:r0 + 2 * half, :] = (o2 * scale).astype(BF16)
            if 2 * half < HEAD_DIM:
                out_ref[r0 + 2 * half:r0 + HEAD_DIM, :] = (p[r0 + 2 * half:r0 + HEAD_DIM] * scale).astype(BF16)

    rope_heads(proj(A_Q, A_K), N_HEADS_A, q_ref, H_HD, chd, shd, Q_SCALE * LOG2E)
    rope_heads(proj(A_K, A_V), N_KV_A, k_ref, H_HD, chd, shd, 1.0)
    pv = proj(A_V, A_QI)
    ones_rows = jnp.where(lax.broadcasted_iota(I32, (VR - HEAD_DIM, tm), 0) == 0, 1.0, 0.0).astype(BF16)
    for g in range(N_KV_A):
        va_ref[g * VR:g * VR + HEAD_DIM, :] = pv[g * HEAD_DIM:(g + 1) * HEAD_DIM].astype(BF16)
        va_ref[g * VR + HEAD_DIM:(g + 1) * VR, :] = ones_rows
    rope_heads(proj(A_QI, A_KI), IDX_HEADS, qi_ref, H_IX, cix, six, 1.0)
    pkw = proj(A_KI, A_QM)
    rope_heads(pkw, 1, ki_ref, H_IX, cix, six, 1.0)
    wi_ref[...] = pkw[IDX_DIM:IDX_DIM + IDX_HEADS] * WI_SCALE
    qm_ref[...] = (proj(A_QM, A_END) * Q_SCALE).astype(BF16)


def _inproj_a(x, g, w_in, tabs):
    t, d = x.shape
    wt = w_in.T
    pad = jnp.zeros((A_QM - A_WI - IDX_HEADS, d), w_in.dtype)
    split = A_WI + IDX_HEADS
    wt = jnp.concatenate([wt[:split], pad, wt[split:]], axis=0).astype(BF16)
    tm = min(512, t)
    col = lambda r: pl.BlockSpec((r, tm), lambda i: (0, i))
    outs = [(N_HEADS_A * HEAD_DIM, BF16), (N_KV_A * HEAD_DIM, BF16), (N_KV_A * VR, BF16),
            (IDX_HEADS * IDX_DIM, BF16), (IDX_DIM, BF16), (IDX_HEADS, F32), (N_MEM_HEADS * HEAD_DIM, BF16)]
    return pl.pallas_call(
        _inproj_a_kernel,
        grid=(t // tm,),
        in_specs=[pl.BlockSpec((tm, d), lambda i: (i, 0)), pl.BlockSpec((1, d), lambda i: (0, 0)),
                  pl.BlockSpec((A_END, d), lambda i: (0, 0)),
                  col(H_HD), col(H_HD), col(H_IX), col(H_IX)],
        out_specs=[col(r) for r, _ in outs],
        out_shape=[jax.ShapeDtypeStruct((r, t), dt) for r, dt in outs],
        compiler_params=_params("parallel"),
        name="inproj_a",
    )(x, g.reshape(1, d), wt, *tabs)


def _inproj_b_kernel(x_ref, g_ref, w_ref, chd_ref, shd_ref, *out_refs):
    h = _rms(x_ref[...], g_ref[...]).astype(BF16)
    chd, shd = chd_ref[...], shd_ref[...]
    ng = len(DIL_PATTERNS)
    for g in range(ng):
        q_ref, k_ref, va_ref = out_refs[3 * g:3 * g + 3]
        base = g * 768
        pq = _dot(h, w_ref[:, base:base + 256])
        pk = _dot(h, w_ref[:, base + 256:base + 512])
        pv = _dot(h, w_ref[:, base + 512:base + 768])
        for j in range(2):
            sl = slice(j * LANES, (j + 1) * LANES)
            q_ref[:, sl] = (_rope_block(pq[:, sl], chd, shd, HEAD_DIM // 2) * Q_SCALE).astype(BF16)
            k_ref[:, sl] = _rope_block(pk[:, sl], chd, shd, HEAD_DIM // 2).astype(BF16)
            ev, od = _v_aug_blocks(pv[:, sl])
            va_ref[:, (2 * j) * LANES:(2 * j + 1) * LANES] = ev.astype(BF16)
            va_ref[:, (2 * j + 1) * LANES:(2 * j + 2) * LANES] = od.astype(BF16)
    out_refs[3 * ng][...] = (_dot(h, w_ref[:, ng * 768:ng * 768 + 256]) * Q_SCALE).astype(BF16)


def _inproj_b(x, g, w_in, tabs):
    t, d = x.shape
    w = w_in.astype(BF16)
    n = w.shape[1]
    tm = min(512, t)
    row = lambda c: pl.BlockSpec((tm, c), lambda i: (i, 0))
    widths = [256, 256, 512] * len(DIL_PATTERNS) + [256]
    return pl.pallas_call(
        _inproj_b_kernel,
        grid=(t // tm,),
        in_specs=[row(d), pl.BlockSpec((1, d), lambda i: (0, 0)), pl.BlockSpec((d, n), lambda i: (0, 0)),
                  row(LANES), row(LANES)],
        out_specs=[row(c) for c in widths],
        out_shape=[jax.ShapeDtypeStruct((t, c), BF16) for c in widths],
        compiler_params=_params("parallel"),
        name="inproj_b",
    )(x, g.reshape(1, d), w, tabs[0], tabs[1])


TQ = 256
CR = 32
SCORE_BITS = 32


def _key_to_f32(key):
    bits = jnp.where(key < 0, key ^ jnp.int32(0x7FFFFFFF), key)
    return pltpu.bitcast(bits, F32)


def _dsa_kernel(q_ref, qi_ref, wi_ref, ki_ref, k_ref, va_ref, o_ref,
                sc_scr, tau_scr, need_scr, tie_scr, tri_scr, m_scr, alpha_scr, acc_scr, s0_scr, s1_scr,
                *, seq, n_sel):
    i = pl.program_id(1)
    nch = i + 1
    krow = lax.broadcasted_iota(I32, (TQ, TQ), 0)
    qcol = lax.broadcasted_iota(I32, (TQ, TQ), 1)

    def chunk_off(c):
        return pl.multiple_of(c * TQ, TQ)

    def score_chunk(c, diag):
        off = chunk_off(c)
        kic = ki_ref[pl.ds(off, TQ), :]
        sc = jnp.zeros((TQ, TQ), F32)
        for h in range(IDX_HEADS):
            lg = _dot(kic, qi_ref[h * IDX_DIM:(h + 1) * IDX_DIM, :])
            sc = sc + jnp.maximum(lg, 0.0) * wi_ref[h:h + 1, :]
        if diag:
            sc = jnp.where(krow > qcol, NEG, sc)
        sc_scr[pl.ds(off, TQ), :] = sc

    def score_body(c, carry):
        score_chunk(c, False)
        return carry

    lax.fori_loop(0, i, score_body, 0)
    score_chunk(i, True)

    def count(pred):
        def body(c, acc):
            off = chunk_off(c)
            ind = pred(sc_scr[pl.ds(off, TQ), :], off)
            return acc + jnp.sum(ind.reshape(TQ // CR, CR, TQ), axis=0)
        acc = lax.fori_loop(0, nch, body, jnp.zeros((CR, TQ), F32))
        return jnp.sum(acc, axis=0, keepdims=True)

    @pl.when(i * TQ < n_sel)
    def _():
        tau_scr[...] = jnp.full((1, TQ), NEG, F32)
        need_scr[...] = jnp.zeros((1, TQ), F32)

    @pl.when(i * TQ >= n_sel)
    def _():
        def body(step, carry):
            tau, cge, crej = carry
            cand = tau + jnp.left_shift(jnp.int32(1), SCORE_BITS - 1 - step)
            cand_f = _key_to_f32(cand)
            cnt = count(lambda blk, _: jnp.where(blk >= cand_f, 1.0, 0.0))
            ok = cnt >= float(n_sel)
            return jnp.where(ok, cand, tau), jnp.where(ok, cnt, cge), jnp.where(ok, crej, cnt)

        init = (jnp.full((1, TQ), INT_MIN, I32), jnp.full((1, TQ), 2.0 * n_sel, F32), jnp.zeros((1, TQ), F32))
        tau, cge, crej = lax.fori_loop(0, SCORE_BITS, body, init)
        tau_scr[...] = _key_to_f32(tau)
        need_scr[...] = jnp.where(cge > float(n_sel), float(n_sel) - crej, 2.0 * seq)

    m_scr[...] = jnp.full(m_scr.shape, NEG, F32)
    acc_scr[...] = jnp.zeros(acc_scr.shape, F32)
    tie_scr[...] = jnp.zeros((1, TQ), F32)
    tri_scr[...] = jnp.where(krow >= qcol, 1.0, 0.0).astype(BF16)
    grp = N_HEADS_A // N_KV_A
    tau_f = tau_scr[...]
    need = need_scr[...]

    def stage_a(c, s_dst):
        off = chunk_off(c)
        blk = sc_scr[pl.ds(off, TQ), :]
        eq = blk == tau_f
        rank = _dot(tri_scr[...], jnp.where(eq, 1.0, 0.0).astype(BF16)) + tie_scr[...]
        tie_scr[...] = rank[TQ - 1:TQ, :]
        bias = jnp.where(eq, jnp.where(rank <= need, 0.0, NEG), jnp.where(blk > tau_f, 0.0, NEG))
        for g in range(N_KV_A):
            kc = k_ref[pl.ds(off, TQ), g * HEAD_DIM:(g + 1) * HEAD_DIM]
            for j in range(grp):
                h = g * grp + j
                s = _dot(kc, q_ref[h * HEAD_DIM:(h + 1) * HEAD_DIM, :]) + bias
                s_dst[h] = s
                m_old = m_scr[h]
                m_new = jnp.maximum(m_old, jnp.max(s, axis=0, keepdims=True))
                alpha_scr[h] = jnp.exp2(m_old - m_new)
                m_scr[h] = m_new

    def stage_b(c, s_src):
        off = chunk_off(c)
        for g in range(N_KV_A):
            vt = va_ref[g * VR:(g + 1) * VR, pl.ds(off, TQ)]
            for j in range(grp):
                h = g * grp + j
                p = jnp.exp2(s_src[h] - m_scr[h]).astype(BF16)
                acc_scr[h] = alpha_scr[h] * acc_scr[h] + _dot(vt, p)

    def step(c, s_src, s_dst):
        stage_b(c - 1, s_src)
        stage_a(c, s_dst)

    stage_a(0, s0_scr)
    npairs = (nch - 1) // 2

    def pair_body(t, carry):
        step(2 * t + 1, s0_scr, s1_scr)
        step(2 * t + 2, s1_scr, s0_scr)
        return carry

    lax.fori_loop(0, npairs, pair_body, 0)

    @pl.when((nch - 1) % 2 == 1)
    def _():
        step(nch - 1, s0_scr, s1_scr)
        stage_b(nch - 1, s1_scr)

    @pl.when((nch - 1) % 2 == 0)
    def _():
        stage_b(nch - 1, s0_scr)

    for h in range(N_HEADS_A):
        a = acc_scr[h]
        o_ref[h * HEAD_DIM:(h + 1) * HEAD_DIM, :] = (a[:HEAD_DIM] / a[HEAD_DIM:HEAD_DIM + 1]).astype(BF16)


def _dsa_attention(qt, qit, wit, ki, k, vat, b, s):
    n_sel = min(TOPK_MAX, s // 4)
    assert s % TQ == 0 and n_sel % TQ == 0
    nq = s // TQ
    qblk = lambda r: pl.BlockSpec((r, TQ), lambda bi, i: (0, bi * nq + i))
    tok = lambda c: pl.BlockSpec((s, c), lambda bi, i: (bi, 0))
    return pl.pallas_call(
        functools.partial(_dsa_kernel, seq=s, n_sel=n_sel),
        grid=(b, nq),
        in_specs=[qblk(qt.shape[0]), qblk(qit.shape[0]), qblk(wit.shape[0]),
                  tok(ki.shape[1]), tok(k.shape[1]),
                  pl.BlockSpec((vat.shape[0], s), lambda bi, i: (0, bi))],
        out_specs=qblk(qt.shape[0]),
        out_shape=jax.ShapeDtypeStruct(qt.shape, BF16),
        scratch_shapes=[
            pltpu.VMEM((s, TQ), F32),
            pltpu.VMEM((1, TQ), F32),
            pltpu.VMEM((1, TQ), F32),
            pltpu.VMEM((1, TQ), F32),
            pltpu.VMEM((TQ, TQ), BF16),
            pltpu.VMEM((N_HEADS_A, 1, TQ), F32),
            pltpu.VMEM((N_HEADS_A, 1, TQ), F32),
            pltpu.VMEM((N_HEADS_A, VR, TQ), F32),
            pltpu.VMEM((N_HEADS_A, TQ, TQ), F32),
            pltpu.VMEM((N_HEADS_A, TQ, TQ), F32),
        ],
        compiler_params=_params("parallel", "arbitrary"),
        name="dsa_attention",
    )(qt, qit, wit, ki, k, vat)


def _mem_attn_kernel(q_ref, k_ref, v_ref, o_ref):
    nh = N_MEM_HEADS
    for hp in range(nh // 2):
        out = None
        for h in (2 * hp, 2 * hp + 1):
            sl = slice(h * HEAD_DIM, (h + 1) * HEAD_DIM)
            s = _dot_nt(q_ref[:, sl], k_ref[:, sl])
            p = jnp.exp(s - jnp.max(s, axis=1, keepdims=True))
            l = jnp.sum(p, axis=1, keepdims=True)
            o = _dot(p.astype(BF16), v_ref[:, h * LANES:(h + 1) * LANES]) / l
            out = o if out is None else out + o
        o_ref[:, hp * LANES:(hp + 1) * LANES] = out.astype(BF16)


def _mem_attention(qm, km, vm):
    b, s, c = qm.shape
    m = km.shape[1]
    tm = min(512, s)
    return pl.pallas_call(
        _mem_attn_kernel,
        grid=(b, s // tm),
        in_specs=[pl.BlockSpec((None, tm, c), lambda bi, i: (bi, i, 0)),
                  pl.BlockSpec((None, m, km.shape[2]), lambda bi, i: (bi, 0, 0)),
                  pl.BlockSpec((None, m, vm.shape[2]), lambda bi, i: (bi, 0, 0))],
        out_specs=pl.BlockSpec((None, tm, c), lambda bi, i: (bi, i, 0)),
        out_shape=jax.ShapeDtypeStruct((b, s, c), BF16),
        compiler_params=_params("parallel", "parallel"),
        name="mem_attention",
    )(qm, km, vm)


QB = 256


def _band_kernel(q_ref, kp_ref, kc_ref, vp_ref, vc_ref, o_ref, lse_ref, *, qb):
    j = pl.program_id(1)
    nsub = qb // BLK
    row = lax.broadcasted_iota(I32, (BLK, BLK), 0)
    col = _lane((BLK, BLK))
    bias_prev = jnp.where(col >= row, 0.0, NEG)
    bias_cur = jnp.where(col <= row, 0.0, NEG)
    lo = _lane((BLK, LANES)) < HEAD_DIM
    for sb in range(nsub):
        rows = slice(sb * BLK, (sb + 1) * BLK)
        if sb == 0:
            kprev, vprev = kp_ref, vp_ref
            prows = slice(0, BLK)
            bp = bias_prev + jnp.where(j > 0, 0.0, NEG)
        else:
            kprev, vprev = kc_ref, vc_ref
            prows = slice((sb - 1) * BLK, sb * BLK)
            bp = bias_prev
        for hp in range(HEADS_PER_DIL // 2):
            outs, lses = [], []
            for h in (2 * hp, 2 * hp + 1):
                hs = slice(h * HEAD_DIM, (h + 1) * HEAD_DIM)
                vs = slice(h * LANES, (h + 1) * LANES)
                qh = q_ref[rows, hs]
                s_p = _dot_nt(qh, kprev[prows, hs]) + bp
                s_c = _dot_nt(qh, kc_ref[rows, hs]) + bias_cur
                m = jnp.maximum(jnp.max(s_p, axis=1, keepdims=True), jnp.max(s_c, axis=1, keepdims=True))
                p_p = jnp.exp(s_p - m).astype(BF16)
                p_c = jnp.exp(s_c - m).astype(BF16)
                acc = _dot(p_p, vprev[prows, vs]) + _dot(p_c, vc_ref[rows, vs])
                l = acc[:, HEAD_DIM:HEAD_DIM + 1]
                outs.append(acc / l)
                lses.append(jnp.broadcast_to(m + jnp.log(l), (BLK, LANES)))
            o_ref[rows, hp * LANES:(hp + 1) * LANES] = jnp.where(
                lo, outs[0], pltpu.roll(outs[1], HEAD_DIM, 1)).astype(o_ref.dtype)
            lse_ref[rows, hp * LANES:(hp + 1) * LANES] = jnp.where(lo, lses[0], lses[1])


def _band_attention(q, k, va):
    ns, n, c = q.shape
    qb = min(QB, n)
    assert n % qb == 0 and qb % BLK == 0
    r = qb // BLK
    cur = lambda w: pl.BlockSpec((None, qb, w), lambda si, j: (si, j, 0))
    prev = lambda w: pl.BlockSpec((None, BLK, w), lambda si, j: (si, jnp.maximum(j * r - 1, 0), 0))
    return pl.pallas_call(
        functools.partial(_band_kernel, qb=qb),
        grid=(ns, n // qb),
        in_specs=[cur(c), prev(c), cur(c), prev(va.shape[2]), cur(va.shape[2])],
        out_specs=[cur(c), cur(c)],
        out_shape=[jax.ShapeDtypeStruct((ns, n, c), BF16), jax.ShapeDtypeStruct((ns, n, c), F32)],
        compiler_params=_params("parallel", "parallel"),
        name="band_attention",
    )(q, k, k, va, va)


def _merge_kernel(*refs):
    ng = len(DIL_PATTERNS)
    o_refs, l_refs, out_ref = refs[:ng], refs[ng:2 * ng], refs[2 * ng]
    lses = [r[...] for r in l_refs]
    m = functools.reduce(jnp.maximum, lses)
    es = [jnp.exp(l - m) for l in lses]
    num = sum(e * o[...].astype(F32) for e, o in zip(es, o_refs))
    out_ref[...] = (num / sum(es)).astype(out_ref.dtype)


def _merge_groups(os_, lses):
    t, c = os_[0].shape
    tm = min(1024, t)
    spec = pl.BlockSpec((tm, c), lambda i: (i, 0))
    return pl.pallas_call(
        _merge_kernel,
        grid=(t // tm,),
        in_specs=[spec] * (2 * len(os_)),
        out_specs=spec,
        out_shape=jax.ShapeDtypeStruct((t, c), BF16),
        compiler_params=_params("parallel"),
        name="merge_groups",
    )(*os_, *lses)


def _to_sub(a, dil):
    b, s, c = a.shape
    if dil == 1:
        return a
    return a.reshape(b, s // dil, dil, c).transpose(0, 2, 1, 3).reshape(b * dil, s // dil, c)


def _from_sub(a, dil, b):
    if dil == 1:
        return a
    ns, n, c = a.shape
    return a.reshape(b, dil, n, c).transpose(0, 2, 1, 3).reshape(b, n * dil, c)


def _ffn_kernel(x_ref, mix_ref, mo_ref, wo1_ref, wo2_ref, g_ref, wg_ref, wu_ref, wd_ref, gf_ref,
                o_ref, x2_scr, h_scr, acc_scr, *, final_norm):
    f = pl.program_id(1)

    @pl.when(f == 0)
    def _():
        x2 = x_ref[...] + _dot(mix_ref[...], wo1_ref[...]) + _dot(mo_ref[...], wo2_ref[...])
        x2_scr[...] = x2
        h_scr[...] = _rms(x2, g_ref[...]).astype(BF16)
        acc_scr[...] = jnp.zeros(acc_scr.shape, F32)

    h = h_scr[...]
    gate = _dot(h, wg_ref[...])
    up = _dot(h, wu_ref[...])
    act = (gate * jax.nn.sigmoid(gate) * up).astype(BF16)
    acc_scr[...] += _dot(act, wd_ref[...])

    @pl.when(f == pl.num_programs(1) - 1)
    def _():
        y = x2_scr[...] + acc_scr[...]
        if final_norm:
            y = _rms(y, gf_ref[...])
        o_ref[...] = y


def _out_ffn(x, mix, mo, w_out, g_ffn, w_gate_up, w_down, g_final, final_norm):
    t, d = x.shape
    cm, cmo = mix.shape[1], mo.shape[1]
    dff = w_down.shape[0]
    wo1 = w_out[:cm].astype(BF16)
    wo2 = w_out[cm:].astype(BF16)
    wgu = w_gate_up.astype(BF16)
    wd = w_down.astype(BF16)
    tm = min(512, t)
    tf = 256 if dff % 256 == 0 else dff
    nf = dff // tf
    row = lambda c: pl.BlockSpec((tm, c), lambda i, f: (i, 0))
    const = lambda r, c: pl.BlockSpec((r, c), lambda i, f: (0, 0))
    return pl.pallas_call(
        functools.partial(_ffn_kernel, final_norm=final_norm),
        grid=(t // tm, nf),
        in_specs=[row(d), row(cm), row(cmo), const(cm, d), const(cmo, d), const(1, d),
                  pl.BlockSpec((d, tf), lambda i, f: (0, f)),
                  pl.BlockSpec((d, tf), lambda i, f: (0, f + nf)),
                  pl.BlockSpec((tf, d), lambda i, f: (f, 0)),
                  const(1, d)],
        out_specs=row(d),
        out_shape=jax.ShapeDtypeStruct((t, d), F32),
        scratch_shapes=[pltpu.VMEM((tm, d), F32), pltpu.VMEM((tm, d), BF16), pltpu.VMEM((tm, d), F32)],
        compiler_params=_params("parallel", "arbitrary"),
        name="out_ffn",
    )(x, mix, mo, wo1, wo2, g_ffn.reshape(1, d), wgu, wgu, wd, g_final.reshape(1, d))


def kernel(x, mem, positions,
           l0_norm_mix, l0_norm_mem, l0_w_in, l0_w_mem_kv, l0_w_out, l0_norm_ffn, l0_w_gate_up, l0_w_down,
           l1_norm_mix, l1_norm_mem, l1_w_in, l1_w_mem_kv, l1_w_out, l1_norm_ffn, l1_w_gate_up, l1_w_down,
           final_norm):
    b, s, d = x.shape
    t = b * s
    xt = x.reshape(t, d)
    tabs = _trig_tables(positions.reshape(1, t))
    r3 = lambda a: a.reshape(b, s, a.shape[-1])

    qt, kt, vat, qit, kit, wit, qmt = _inproj_a(xt, l0_norm_mix, l0_w_in, tabs)
    mix = _dsa_attention(qt, qit, wit, kit.T, kt.T, vat, b, s).T
    km, vm = _mem_kv(mem, l0_norm_mem, l0_w_mem_kv)
    mo = _mem_attention(r3(qmt.T), km, vm)
    xt = _out_ffn(xt, mix, mo.reshape(t, -1), l0_w_out, l0_norm_ffn,
                  l0_w_gate_up, l0_w_down, final_norm, False)

    outs = _inproj_b(xt, l1_norm_mix, l1_w_in, _token_major_patterns(tabs[0], tabs[1]))
    os_, lses = [], []
    for g, (window, dil) in enumerate(DIL_PATTERNS):
        assert window // dil == BLK
        qg, kg, vag = (_to_sub(r3(a), dil) for a in outs[3 * g:3 * g + 3])
        o, lse = _band_attention(qg, kg, vag)
        os_.append(_from_sub(o, dil, b).reshape(t, -1))
        lses.append(_from_sub(lse, dil, b).reshape(t, -1))
    mix = _merge_groups(os_, lses)
    km, vm = _mem_kv(mem, l1_norm_mem, l1_w_mem_kv)
    mo = _mem_attention(r3(outs[-1]), km, vm)
    xt = _out_ffn(xt, mix, mo.reshape(t, -1), l1_w_out, l1_norm_ffn,
                  l1_w_gate_up, l1_w_down, final_norm, True)
    return xt.reshape(b, s, d)
```

```python
import functools

import jax
import jax.numpy as jnp
from jax import lax
from jax.experimental import pallas as pl
from jax.experimental.pallas import tpu as pltpu

F32 = jnp.float32
BF16 = jnp.bfloat16
I32 = jnp.int32

HEAD_DIM = 64
N_HEADS_A = 12
N_KV_A = 4
IDX_HEADS = 8
IDX_DIM = 64
IDX_ROPE_DIM = 32
TOPK_MAX = 256
DIL_PATTERNS = ((128, 1), (512, 4), (2048, 16))
HEADS_PER_DIL = 4
N_MEM_HEADS = 4
BLK = 128
ROPE_THETA = 10000.0
EPS = 1e-6
NEG = -1e30
INT_MIN = -2147483648

LANES = 128
VMEM_LIMIT = 56 * 1024 * 1024

Q_SCALE = HEAD_DIM ** -0.5
WI_SCALE = IDX_HEADS ** -0.5 * IDX_DIM ** -0.5
LOG2E = 1.4426950408889634


def _dot(a, b):
    return jnp.dot(a, b, preferred_element_type=F32)


def _dot_nt(a, b):
    return lax.dot_general(a, b, (((1,), (1,)), ((), ())), preferred_element_type=F32)


def _params(*sem):
    return pltpu.CompilerParams(dimension_semantics=sem, vmem_limit_bytes=VMEM_LIMIT)


def _rms(x, g):
    ms = jnp.mean(x * x, axis=-1, keepdims=True)
    return x * lax.rsqrt(ms + EPS) * g


H_HD = HEAD_DIM // 2
H_IX = IDX_ROPE_DIM // 2


def _trig_kernel(pos_ref, f_ref, chd_ref, shd_ref, cix_ref, six_ref):
    tm = pos_ref.shape[1]
    pos = pos_ref[...].astype(F32)
    f = jnp.concatenate([f_ref[...]] * (tm // LANES), axis=1)
    ang = f * pos
    c, s = jnp.cos(ang), jnp.sin(ang)
    chd_ref[...] = c[:H_HD]
    shd_ref[...] = s[:H_HD]
    cix_ref[...] = c[H_HD:]
    six_ref[...] = s[H_HD:]


def _trig_tables(pos_row):
    t = pos_row.shape[1]
    tm = min(2048, t)
    f_hd = ROPE_THETA ** (-jnp.arange(H_HD, dtype=F32) / H_HD)
    f_ix = ROPE_THETA ** (-jnp.arange(H_IX, dtype=F32) / H_IX)
    f = jnp.broadcast_to(jnp.concatenate([f_hd, f_ix])[:, None], (H_HD + H_IX, LANES))
    spec = lambda r: pl.BlockSpec((r, tm), lambda i: (0, i))
    rows = [H_HD, H_HD, H_IX, H_IX]
    return pl.pallas_call(
        _trig_kernel,
        grid=(t // tm,),
        in_specs=[spec(1), pl.BlockSpec((H_HD + H_IX, LANES), lambda i: (0, 0))],
        out_specs=[spec(r) for r in rows],
        out_shape=[jax.ShapeDtypeStruct((r, t), F32) for r in rows],
        compiler_params=_params("parallel"),
        name="rope_tables",
    )(pos_row, f)


def _norm_matmul_kernel(x_ref, g_ref, w_ref, o_ref):
    h = _rms(x_ref[...], g_ref[...]).astype(BF16)
    o_ref[...] = _dot(h, w_ref[...]).astype(o_ref.dtype)


def _norm_matmul(x, g, w, out_dtype):
    t, d = x.shape
    n = w.shape[1]
    tm = min(512, t)
    return pl.pallas_call(
        _norm_matmul_kernel,
        grid=(t // tm,),
        in_specs=[pl.BlockSpec((tm, d), lambda i: (i, 0)),
                  pl.BlockSpec((1, d), lambda i: (0, 0)),
                  pl.BlockSpec((d, n), lambda i: (0, 0))],
        out_specs=pl.BlockSpec((tm, n), lambda i: (i, 0)),
        out_shape=jax.ShapeDtypeStruct((t, n), out_dtype),
        compiler_params=_params("parallel"),
        name="norm_matmul",
    )(x, g.reshape(1, d), w)


def _mem_kv(mem, g, w_kv):
    b, m, d = mem.shape
    nh, hd = N_MEM_HEADS, HEAD_DIM
    wk = w_kv[:, :nh * hd]
    wv = w_kv[:, nh * hd:].reshape(d, nh, hd)
    z = jnp.zeros((d, hd), w_kv.dtype)
    cols = [wk]
    for h in range(nh):
        cols += ([wv[:, h], z] if h % 2 == 0 else [z, wv[:, h]])
    w = jnp.concatenate(cols, axis=1).astype(BF16)
    kv = _norm_matmul(mem.reshape(b * m, d), g, w, BF16)
    kv = kv.reshape(b, m, -1)
    return kv[:, :, :nh * hd], kv[:, :, nh * hd:]


VR = 80

A_Q, A_K, A_V, A_QI, A_KI, A_WI, A_QM, A_END = 0, 768, 1024, 1280, 1792, 1856, 1872, 2128


def _rope_heads(p, nheads, out_ref, half, c, s, scale):
    for hh in range(nheads):
        r0 = hh * HEAD_DIM
        x1, x2 = p[r0:r0 + half], p[r0 + half:r0 + 2 * half]
        out_ref[r0:r0 + half, :] = ((x1 * c - x2 * s) * scale).astype(BF16)
        out_ref[r0 + half:r0 + 2 * half, :] = ((x2 * c + x1 * s) * scale).astype(BF16)
        if 2 * half < HEAD_DIM:
            out_ref[r0 + 2 * half:r0 + HEAD_DIM, :] = (p[r0 + 2 * half:r0 + HEAD_DIM] * scale).astype(BF16)


def _write_values(pv, nheads, va_ref):
    tm = pv.shape[1]
    ones_rows = jnp.where(lax.broadcasted_iota(I32, (VR - HEAD_DIM, tm), 0) == 0, 1.0, 0.0).astype(BF16)
    for g in range(nheads):
        va_ref[g * VR:g * VR + HEAD_DIM, :] = pv[g * HEAD_DIM:(g + 1) * HEAD_DIM].astype(BF16)
        va_ref[g * VR + HEAD_DIM:(g + 1) * VR, :] = ones_rows


def _inproj_a_kernel(x_ref, g_ref, wt_ref, chd_ref, shd_ref, cix_ref, six_ref,
                     q_ref, k_ref, va_ref, qi_ref, ki_ref, wi_ref, qm_ref):
    h = _rms(x_ref[...], g_ref[...]).astype(BF16)
    chd, shd = chd_ref[...], shd_ref[...]
    cix, six = cix_ref[...], six_ref[...]

    def proj(a, b):
        return _dot_nt(wt_ref[a:b, :], h)

    _rope_heads(proj(A_Q, A_K), N_HEADS_A, q_ref, H_HD, chd, shd, Q_SCALE * LOG2E)
    _rope_heads(proj(A_K, A_V), N_KV_A, k_ref, H_HD, chd, shd, 1.0)
    _write_values(proj(A_V, A_QI), N_KV_A, va_ref)
    _rope_heads(proj(A_QI, A_KI), IDX_HEADS, qi_ref, H_IX, cix, six, 1.0)
    pkw = proj(A_KI, A_QM)
    _rope_heads(pkw, 1, ki_ref, H_IX, cix, six, 1.0)
    wi_ref[...] = pkw[IDX_DIM:IDX_DIM + IDX_HEADS] * WI_SCALE
    qm_ref[...] = (proj(A_QM, A_END) * Q_SCALE).astype(BF16)


def _inproj_a(x, g, w_in, tabs):
    t, d = x.shape
    wt = w_in.T
    pad = jnp.zeros((A_QM - A_WI - IDX_HEADS, d), w_in.dtype)
    split = A_WI + IDX_HEADS
    wt = jnp.concatenate([wt[:split], pad, wt[split:]], axis=0).astype(BF16)
    tm = min(512, t)
    col = lambda r: pl.BlockSpec((r, tm), lambda i: (0, i))
    outs = [(N_HEADS_A * HEAD_DIM, BF16), (N_KV_A * HEAD_DIM, BF16), (N_KV_A * VR, BF16),
            (IDX_HEADS * IDX_DIM, BF16), (IDX_DIM, BF16), (IDX_HEADS, F32), (N_MEM_HEADS * HEAD_DIM, BF16)]
    return pl.pallas_call(
        _inproj_a_kernel,
        grid=(t // tm,),
        in_specs=[pl.BlockSpec((tm, d), lambda i: (i, 0)), pl.BlockSpec((1, d), lambda i: (0, 0)),
                  pl.BlockSpec((A_END, d), lambda i: (0, 0)),
                  col(H_HD), col(H_HD), col(H_IX), col(H_IX)],
        out_specs=[col(r) for r, _ in outs],
        out_shape=[jax.ShapeDtypeStruct((r, t), dt) for r, dt in outs],
        compiler_params=_params("parallel"),
        name="inproj_a",
    )(x, g.reshape(1, d), wt, *tabs)


def _inproj_b_kernel(x_ref, g_ref, wt_ref, chd_ref, shd_ref, *out_refs):
    h = _rms(x_ref[...], g_ref[...]).astype(BF16)
    chd, shd = chd_ref[...], shd_ref[...]
    ng = len(DIL_PATTERNS)
    gw = HEADS_PER_DIL * HEAD_DIM
    for g in range(ng):
        q_ref, k_ref, va_ref = out_refs[3 * g:3 * g + 3]
        base = 3 * g * gw
        _rope_heads(_dot_nt(wt_ref[base:base + gw, :], h), HEADS_PER_DIL, q_ref, H_HD, chd, shd, Q_SCALE * LOG2E)
        _rope_heads(_dot_nt(wt_ref[base + gw:base + 2 * gw, :], h), HEADS_PER_DIL, k_ref, H_HD, chd, shd, 1.0)
        _write_values(_dot_nt(wt_ref[base + 2 * gw:base + 3 * gw, :], h), HEADS_PER_DIL, va_ref)
    out_refs[3 * ng][...] = (_dot_nt(wt_ref[3 * ng * gw:3 * ng * gw + N_MEM_HEADS * HEAD_DIM, :], h)
                            * Q_SCALE).astype(BF16)


def _inproj_b(x, g, w_in, tabs):
    t, d = x.shape
    wt = w_in.T.astype(BF16)
    tm = min(512, t)
    col = lambda r: pl.BlockSpec((r, tm), lambda i: (0, i))
    gw = HEADS_PER_DIL * HEAD_DIM
    rows = [gw, gw, HEADS_PER_DIL * VR] * len(DIL_PATTERNS) + [N_MEM_HEADS * HEAD_DIM]
    return pl.pallas_call(
        _inproj_b_kernel,
        grid=(t // tm,),
        in_specs=[pl.BlockSpec((tm, d), lambda i: (i, 0)), pl.BlockSpec((1, d), lambda i: (0, 0)),
                  pl.BlockSpec(wt.shape, lambda i: (0, 0)), col(H_HD), col(H_HD)],
        out_specs=[col(r) for r in rows],
        out_shape=[jax.ShapeDtypeStruct((r, t), BF16) for r in rows],
        compiler_params=_params("parallel"),
        name="inproj_b",
    )(x, g.reshape(1, d), wt, tabs[0], tabs[1])


TQ = 256
CR = 32
SCORE_BITS = 32


def _key_to_f32(key):
    bits = jnp.where(key < 0, key ^ jnp.int32(0x7FFFFFFF), key)
    return pltpu.bitcast(bits, F32)


def _dsa_kernel(q_ref, qi_ref, wi_ref, ki_ref, k_ref, va_ref, o_ref,
                sc_scr, tau_scr, need_scr, tie_scr, tri_scr, m_scr, alpha_scr, acc_scr, s0_scr, s1_scr,
                *, seq, n_sel):
    i = pl.program_id(1)
    nch = i + 1
    krow = lax.broadcasted_iota(I32, (TQ, TQ), 0)
    qcol = lax.broadcasted_iota(I32, (TQ, TQ), 1)

    def chunk_off(c):
        return pl.multiple_of(c * TQ, TQ)

    def score_chunk(c, diag):
        off = chunk_off(c)
        kic = ki_ref[pl.ds(off, TQ), :]
        sc = jnp.zeros((TQ, TQ), F32)
        for h in range(IDX_HEADS):
            lg = _dot(kic, qi_ref[h * IDX_DIM:(h + 1) * IDX_DIM, :])
            sc = sc + jnp.maximum(lg, 0.0) * wi_ref[h:h + 1, :]
        if diag:
            sc = jnp.where(krow > qcol, -jnp.inf, sc)
        sc_scr[pl.ds(off, TQ), :] = sc

    def score_body(c, carry):
        score_chunk(c, False)
        return carry

    lax.fori_loop(0, i, score_body, 0)
    score_chunk(i, True)

    def count(pred):
        def body(c, acc):
            off = chunk_off(c)
            ind = pred(sc_scr[pl.ds(off, TQ), :], off)
            return acc + jnp.sum(ind.reshape(TQ // CR, CR, TQ), axis=0)
        acc = lax.fori_loop(0, nch, body, jnp.zeros((CR, TQ), F32))
        return jnp.sum(acc, axis=0, keepdims=True)

    @pl.when(i * TQ < n_sel)
    def _():
        tau_scr[...] = jnp.full((1, TQ), -jnp.inf, F32)
        need_scr[...] = jnp.zeros((1, TQ), F32)

    @pl.when(i * TQ >= n_sel)
    def _():
        def body(step, carry):
            tau, cge, crej = carry
            cand = tau + jnp.left_shift(jnp.int32(1), SCORE_BITS - 1 - step)
            cand_f = _key_to_f32(cand)
            cnt = count(lambda blk, _: jnp.where(blk >= cand_f, 1.0, 0.0))
            ok = cnt >= float(n_sel)
            return jnp.where(ok, cand, tau), jnp.where(ok, cnt, cge), jnp.where(ok, crej, cnt)

        init = (jnp.full((1, TQ), INT_MIN, I32), jnp.full((1, TQ), 2.0 * n_sel, F32), jnp.zeros((1, TQ), F32))
        tau, cge, crej = lax.fori_loop(0, SCORE_BITS, body, init)
        tau_scr[...] = _key_to_f32(tau)
        need_scr[...] = jnp.where(cge > float(n_sel), float(n_sel) - crej, 2.0 * seq)

    m_scr[...] = jnp.full(m_scr.shape, NEG, F32)
    acc_scr[...] = jnp.zeros(acc_scr.shape, F32)
    tie_scr[...] = jnp.zeros((1, TQ), F32)
    tri_scr[...] = jnp.where(krow >= qcol, 1.0, 0.0).astype(BF16)
    grp = N_HEADS_A // N_KV_A
    tau_f = tau_scr[...]
    need = need_scr[...]

    def stage_a(c, s_dst):
        off = chunk_off(c)
        blk = sc_scr[pl.ds(off, TQ), :]
        eq = blk == tau_f
        rank = _dot(tri_scr[...], jnp.where(eq, 1.0, 0.0).astype(BF16)) + tie_scr[...]
        tie_scr[...] = rank[TQ - 1:TQ, :]
        bias = jnp.where(eq, jnp.where(rank <= need, 0.0, NEG), jnp.where(blk > tau_f, 0.0, NEG))
        for g in range(N_KV_A):
            kc = k_ref[pl.ds(off, TQ), g * HEAD_DIM:(g + 1) * HEAD_DIM]
            for j in range(grp):
                h = g * grp + j
                s = _dot(kc, q_ref[h * HEAD_DIM:(h + 1) * HEAD_DIM, :]) + bias
                s_dst[h] = s
                m_old = m_scr[h]
                m_new = jnp.maximum(m_old, jnp.max(s, axis=0, keepdims=True))
                alpha_scr[h] = jnp.exp2(m_old - m_new)
                m_scr[h] = m_new

    def stage_b(c, s_src):
        off = chunk_off(c)
        for g in range(N_KV_A):
            vt = va_ref[g * VR:(g + 1) * VR, pl.ds(off, TQ)]
            for j in range(grp):
                h = g * grp + j
                p = jnp.exp2(s_src[h] - m_scr[h]).astype(BF16)
                acc_scr[h] = alpha_scr[h] * acc_scr[h] + _dot(vt, p)

    def step(c, s_src, s_dst):
        stage_b(c - 1, s_src)
        stage_a(c, s_dst)

    stage_a(0, s0_scr)
    npairs = (nch - 1) // 2

    def pair_body(t, carry):
        step(2 * t + 1, s0_scr, s1_scr)
        step(2 * t + 2, s1_scr, s0_scr)
        return carry

    lax.fori_loop(0, npairs, pair_body, 0)

    @pl.when((nch - 1) % 2 == 1)
    def _():
        step(nch - 1, s0_scr, s1_scr)
        stage_b(nch - 1, s1_scr)

    @pl.when((nch - 1) % 2 == 0)
    def _():
        stage_b(nch - 1, s0_scr)

    for h in range(N_HEADS_A):
        a = acc_scr[h]
        o_ref[h * HEAD_DIM:(h + 1) * HEAD_DIM, :] = (a[:HEAD_DIM] / a[HEAD_DIM:HEAD_DIM + 1]).astype(BF16)


def _dsa_attention(qt, qit, wit, ki, k, vat, b, s):
    n_sel = min(TOPK_MAX, s // 4)
    assert s % TQ == 0 and n_sel % TQ == 0
    nq = s // TQ
    qblk = lambda r: pl.BlockSpec((r, TQ), lambda bi, i: (0, bi * nq + i))
    tok = lambda c: pl.BlockSpec((s, c), lambda bi, i: (bi, 0))
    return pl.pallas_call(
        functools.partial(_dsa_kernel, seq=s, n_sel=n_sel),
        grid=(b, nq),
        in_specs=[qblk(qt.shape[0]), qblk(qit.shape[0]), qblk(wit.shape[0]),
                  tok(ki.shape[1]), tok(k.shape[1]),
                  pl.BlockSpec((vat.shape[0], s), lambda bi, i: (0, bi))],
        out_specs=qblk(qt.shape[0]),
        out_shape=jax.ShapeDtypeStruct(qt.shape, BF16),
        scratch_shapes=[
            pltpu.VMEM((s, TQ), F32),
            pltpu.VMEM((1, TQ), F32),
            pltpu.VMEM((1, TQ), F32),
            pltpu.VMEM((1, TQ), F32),
            pltpu.VMEM((TQ, TQ), BF16),
            pltpu.VMEM((N_HEADS_A, 1, TQ), F32),
            pltpu.VMEM((N_HEADS_A, 1, TQ), F32),
            pltpu.VMEM((N_HEADS_A, VR, TQ), F32),
            pltpu.VMEM((N_HEADS_A, TQ, TQ), F32),
            pltpu.VMEM((N_HEADS_A, TQ, TQ), F32),
        ],
        compiler_params=_params("parallel", "arbitrary"),
        name="dsa_attention",
    )(qt, qit, wit, ki, k, vat)


def _mem_attn_kernel(q_ref, k_ref, v_ref, o_ref):
    nh = N_MEM_HEADS
    for hp in range(nh // 2):
        out = None
        for h in (2 * hp, 2 * hp + 1):
            sl = slice(h * HEAD_DIM, (h + 1) * HEAD_DIM)
            s = _dot_nt(q_ref[:, sl], k_ref[:, sl])
            p = jnp.exp(s - jnp.max(s, axis=1, keepdims=True))
            l = jnp.sum(p, axis=1, keepdims=True)
            o = _dot(p.astype(BF16), v_ref[:, h * LANES:(h + 1) * LANES]) / l
            out = o if out is None else out + o
        o_ref[:, hp * LANES:(hp + 1) * LANES] = out.astype(BF16)


def _mem_attention(qm, km, vm):
    b, s, c = qm.shape
    m = km.shape[1]
    tm = min(512, s)
    return pl.pallas_call(
        _mem_attn_kernel,
        grid=(b, s // tm),
        in_specs=[pl.BlockSpec((None, tm, c), lambda bi, i: (bi, i, 0)),
                  pl.BlockSpec((None, m, km.shape[2]), lambda bi, i: (bi, 0, 0)),
                  pl.BlockSpec((None, m, vm.shape[2]), lambda bi, i: (bi, 0, 0))],
        out_specs=pl.BlockSpec((None, tm, c), lambda bi, i: (bi, i, 0)),
        out_shape=jax.ShapeDtypeStruct((b, s, c), BF16),
        compiler_params=_params("parallel", "parallel"),
        name="mem_attention",
    )(qm, km, vm)


QB = 512
LN2 = 0.6931471805599453


def _band_kernel(q_ref, kp_ref, kc_ref, vp_ref, vc_ref, o_ref, lse_ref, s_scr, m_scr, *, qb):
    j = pl.program_id(1)
    nsub = qb // BLK
    krow = lax.broadcasted_iota(I32, (BLK, BLK), 0)
    qcol = lax.broadcasted_iota(I32, (BLK, BLK), 1)
    bias_prev = jnp.where(krow >= qcol, 0.0, NEG)
    bias_cur = jnp.where(krow <= qcol, 0.0, NEG)
    no_prev = jnp.where(j > 0, 0.0, NEG)

    def stage_a(sb):
        qs = slice(sb * BLK, (sb + 1) * BLK)
        for h in range(HEADS_PER_DIL):
            hs = slice(h * HEAD_DIM, (h + 1) * HEAD_DIM)
            qh = q_ref[hs, qs]
            if sb == 0:
                s_p = _dot(kp_ref[:, hs], qh) + (bias_prev + no_prev)
            else:
                s_p = _dot(kc_ref[(sb - 1) * BLK:sb * BLK, hs], qh) + bias_prev
            s_c = _dot(kc_ref[qs, hs], qh) + bias_cur
            s_scr[sb % 2, h, 0:BLK] = s_p
            s_scr[sb % 2, h, BLK:2 * BLK] = s_c
            m_scr[sb % 2, h] = jnp.maximum(jnp.max(s_p, axis=0, keepdims=True),
                                           jnp.max(s_c, axis=0, keepdims=True))

    def stage_b(sb):
        qs = slice(sb * BLK, (sb + 1) * BLK)
        for h in range(HEADS_PER_DIL):
            vs = slice(h * VR, (h + 1) * VR)
            m = m_scr[sb % 2, h]
            p_p = jnp.exp2(s_scr[sb % 2, h, 0:BLK] - m).astype(BF16)
            p_c = jnp.exp2(s_scr[sb % 2, h, BLK:2 * BLK] - m).astype(BF16)
            v_p = vp_ref[vs, :] if sb == 0 else vc_ref[vs, (sb - 1) * BLK:sb * BLK]
            acc = _dot(v_p, p_p) + _dot(vc_ref[vs, qs], p_c)
            l = acc[HEAD_DIM:HEAD_DIM + 1]
            o_ref[h * HEAD_DIM:(h + 1) * HEAD_DIM, qs] = (acc[:HEAD_DIM] / l).astype(o_ref.dtype)
            lse_ref[h:h + 1, qs] = m * LN2 + jnp.log(l)

    stage_a(0)
    for sb in range(1, nsub):
        stage_b(sb - 1)
        stage_a(sb)
    stage_b(nsub - 1)


def _band_attention(qt, k, vat, n):
    c, total = qt.shape
    qb = min(QB, n)
    assert n % qb == 0 and qb % BLK == 0 and total % n == 0
    r, nj = qb // BLK, n // qb
    prev_blk = lambda si, j: si * (n // BLK) + jnp.maximum(j * r - 1, 0)
    fm = lambda rows: pl.BlockSpec((rows, qb), lambda si, j: (0, si * nj + j))
    fm_prev = lambda rows: pl.BlockSpec((rows, BLK), lambda si, j: (0, prev_blk(si, j)))
    return pl.pallas_call(
        functools.partial(_band_kernel, qb=qb),
        grid=(total // n, nj),
        in_specs=[fm(c),
                  pl.BlockSpec((BLK, c), lambda si, j: (prev_blk(si, j), 0)),
                  pl.BlockSpec((qb, c), lambda si, j: (si * nj + j, 0)),
                  fm_prev(vat.shape[0]), fm(vat.shape[0])],
        out_specs=[fm(c), fm(HEADS_PER_DIL)],
        out_shape=[jax.ShapeDtypeStruct((c, total), BF16), jax.ShapeDtypeStruct((HEADS_PER_DIL, total), F32)],
        scratch_shapes=[pltpu.VMEM((2, HEADS_PER_DIL, 2 * BLK, BLK), F32),
                        pltpu.VMEM((2, HEADS_PER_DIL, 1, BLK), F32)],
        compiler_params=_params("parallel", "parallel"),
        name="band_attention",
    )(qt, k, k, vat, vat)


def _merge_kernel(*refs):
    ng = len(DIL_PATTERNS)
    o_refs, l_refs, out_ref = refs[:ng], refs[ng:2 * ng], refs[2 * ng]
    lses = [r[...] for r in l_refs]
    m = functools.reduce(jnp.maximum, lses)
    es = [jnp.exp(l - m) for l in lses]
    den = sum(es)
    ws = [e / den for e in es]
    for h in range(HEADS_PER_DIL):
        hs = slice(h * HEAD_DIM, (h + 1) * HEAD_DIM)
        out_ref[hs, :] = sum(w[h:h + 1] * o[hs, :].astype(F32) for w, o in zip(ws, o_refs)).astype(out_ref.dtype)


def _merge_groups(os_, lses):
    c, t = os_[0].shape
    tm = min(2048, t)
    spec = lambda r: pl.BlockSpec((r, tm), lambda i: (0, i))
    return pl.pallas_call(
        _merge_kernel,
        grid=(t // tm,),
        in_specs=[spec(c)] * len(os_) + [spec(HEADS_PER_DIL)] * len(lses),
        out_specs=spec(c),
        out_shape=jax.ShapeDtypeStruct((c, t), BF16),
        compiler_params=_params("parallel"),
        name="merge_groups",
    )(*os_, *lses)


def _split_residues(a, b, dil):
    f, t = a.shape
    return a.reshape(f, b, t // (b * dil), dil).transpose(0, 1, 3, 2)


def _to_sub_fm(a, b, dil):
    return a if dil == 1 else _split_residues(a, b, dil).reshape(a.shape)


def _to_sub_tok(a, b, dil):
    return a.T if dil == 1 else _split_residues(a, b, dil).transpose(1, 2, 3, 0).reshape(a.shape[1], a.shape[0])


def _from_sub_fm(a, b, dil):
    if dil == 1:
        return a
    f, t = a.shape
    return a.reshape(f, b, dil, t // (b * dil)).transpose(0, 1, 3, 2).reshape(f, t)


def _ffn_kernel(x_ref, mix_ref, mo_ref, wo1_ref, wo2_ref, g_ref, wgu_ref, wd_ref, gf_ref,
                o_ref, act_scr, *, final_norm, tf):
    dff = wd_ref.shape[0]
    x2 = x_ref[...] + _dot(mix_ref[...], wo1_ref[...]) + _dot(mo_ref[...], wo2_ref[...])
    h = _rms(x2, g_ref[...]).astype(BF16)
    for f in range(dff // tf):
        gate = _dot(h, wgu_ref[:, f * tf:(f + 1) * tf])
        up = _dot(h, wgu_ref[:, dff + f * tf:dff + (f + 1) * tf])
        act_scr[:, f * tf:(f + 1) * tf] = (gate * jax.nn.sigmoid(gate) * up).astype(BF16)
    y = x2 + _dot(act_scr[...], wd_ref[...])
    if final_norm:
        y = _rms(y, gf_ref[...])
    o_ref[...] = y


def _out_ffn(x, mix, mo, w_out, g_ffn, w_gate_up, w_down, g_final, final_norm):
    t, d = x.shape
    cm, cmo = mix.shape[1], mo.shape[1]
    dff = w_down.shape[0]
    wo1 = w_out[:cm].astype(BF16)
    wo2 = w_out[cm:].astype(BF16)
    wgu = w_gate_up.astype(BF16)
    wd = w_down.astype(BF16)
    tm = min(512, t)
    tf = 256 if dff % 256 == 0 else dff
    row = lambda c: pl.BlockSpec((tm, c), lambda i: (i, 0))
    const = lambda r, c: pl.BlockSpec((r, c), lambda i: (0, 0), pipeline_mode=pl.Buffered(1))
    return pl.pallas_call(
        functools.partial(_ffn_kernel, final_norm=final_norm, tf=tf),
        grid=(t // tm,),
        in_specs=[row(d), row(cm), row(cmo), const(cm, d), const(cmo, d), const(1, d),
                  const(d, 2 * dff), const(dff, d), const(1, d)],
        out_specs=row(d),
        out_shape=jax.ShapeDtypeStruct((t, d), F32),
        scratch_shapes=[pltpu.VMEM((tm, dff), BF16)],
        compiler_params=_params("parallel"),
        name="out_ffn",
    )(x, mix, mo, wo1, wo2, g_ffn.reshape(1, d), wgu, wd, g_final.reshape(1, d))


def kernel(x, mem, positions,
           l0_norm_mix, l0_norm_mem, l0_w_in, l0_w_mem_kv, l0_w_out, l0_norm_ffn, l0_w_gate_up, l0_w_down,
           l1_norm_mix, l1_norm_mem, l1_w_in, l1_w_mem_kv, l1_w_out, l1_norm_ffn, l1_w_gate_up, l1_w_down,
           final_norm):
    b, s, d = x.shape
    t = b * s
    xt = x.reshape(t, d)
    tabs = _trig_tables(positions.reshape(1, t))
    r3 = lambda a: a.reshape(b, s, a.shape[-1])

    qt, kt, vat, qit, kit, wit, qmt = _inproj_a(xt, l0_norm_mix, l0_w_in, tabs)
    mix = _dsa_attention(qt, qit, wit, kit.T, kt.T, vat, b, s).T
    km, vm = _mem_kv(mem, l0_norm_mem, l0_w_mem_kv)
    mo = _mem_attention(r3(qmt.T), km, vm)
    xt = _out_ffn(xt, mix, mo.reshape(t, -1), l0_w_out, l0_norm_ffn,
                  l0_w_gate_up, l0_w_down, final_norm, False)

    outs = _inproj_b(xt, l1_norm_mix, l1_w_in, tabs)
    os_, lses = [], []
    for g, (window, dil) in enumerate(DIL_PATTERNS):
        assert window // dil == BLK
        qg, kg, vag = outs[3 * g:3 * g + 3]
        o, lse = _band_attention(_to_sub_fm(qg, b, dil), _to_sub_tok(kg, b, dil), _to_sub_fm(vag, b, dil), s // dil)
        os_.append(_from_sub_fm(o, b, dil))
        lses.append(_from_sub_fm(lse, b, dil))
    mix = _merge_groups(os_, lses).T
    km, vm = _mem_kv(mem, l1_norm_mem, l1_w_mem_kv)
    mo = _mem_attention(r3(outs[-1].T), km, vm)
    xt = _out_ffn(xt, mix, mo.reshape(t, -1), l1_w_out, l1_norm_ffn,
                  l1_w_gate_up, l1_w_down, final_norm, True)
    return xt.reshape(b, s, d)
```

```python
import functools

import jax
import jax.numpy as jnp
from jax import lax
from jax.experimental import pallas as pl
from jax.experimental.pallas import tpu as pltpu

F32 = jnp.float32
BF16 = jnp.bfloat16
I32 = jnp.int32

HEAD_DIM = 64
N_HEADS_A = 12
N_KV_A = 4
IDX_HEADS = 8
IDX_DIM = 64
IDX_ROPE_DIM = 32
TOPK_MAX = 256
DIL_PATTERNS = ((128, 1), (512, 4), (2048, 16))
HEADS_PER_DIL = 4
N_MEM_HEADS = 4
BLK = 128
ROPE_THETA = 10000.0
EPS = 1e-6
NEG = -1e30
INT_MIN = -2147483648

LANES = 128
VMEM_LIMIT = 56 * 1024 * 1024

Q_SCALE = HEAD_DIM ** -0.5
WI_SCALE = IDX_HEADS ** -0.5 * IDX_DIM ** -0.5
LOG2E = 1.4426950408889634


def _dot(a, b):
    return jnp.dot(a, b, preferred_element_type=F32)


def _dot_nt(a, b):
    return lax.dot_general(a, b, (((1,), (1,)), ((), ())), preferred_element_type=F32)


def _dot_tn(a, b):
    return lax.dot_general(a, b, (((0,), (0,)), ((), ())), preferred_element_type=F32)


def _params(*sem):
    return pltpu.CompilerParams(dimension_semantics=sem, vmem_limit_bytes=VMEM_LIMIT)


def _rms(x, g):
    ms = jnp.mean(x * x, axis=-1, keepdims=True)
    return x * lax.rsqrt(ms + EPS) * g


H_HD = HEAD_DIM // 2
H_IX = IDX_ROPE_DIM // 2


def _trig_kernel(pos_ref, f_ref, chd_ref, shd_ref, cix_ref, six_ref):
    tm = pos_ref.shape[1]
    pos = pos_ref[...].astype(F32)
    f = jnp.concatenate([f_ref[...]] * (tm // LANES), axis=1)
    ang = f * pos
    c, s = jnp.cos(ang), jnp.sin(ang)
    chd_ref[...] = c[:H_HD]
    shd_ref[...] = s[:H_HD]
    cix_ref[...] = c[H_HD:]
    six_ref[...] = s[H_HD:]


def _trig_tables(pos_row):
    t = pos_row.shape[1]
    tm = min(2048, t)
    f_hd = ROPE_THETA ** (-jnp.arange(H_HD, dtype=F32) / H_HD)
    f_ix = ROPE_THETA ** (-jnp.arange(H_IX, dtype=F32) / H_IX)
    f = jnp.broadcast_to(jnp.concatenate([f_hd, f_ix])[:, None], (H_HD + H_IX, LANES))
    spec = lambda r: pl.BlockSpec((r, tm), lambda i: (0, i))
    rows = [H_HD, H_HD, H_IX, H_IX]
    return pl.pallas_call(
        _trig_kernel,
        grid=(t // tm,),
        in_specs=[spec(1), pl.BlockSpec((H_HD + H_IX, LANES), lambda i: (0, 0))],
        out_specs=[spec(r) for r in rows],
        out_shape=[jax.ShapeDtypeStruct((r, t), F32) for r in rows],
        compiler_params=_params("parallel"),
        name="rope_tables",
    )(pos_row, f)


def _norm_matmul_kernel(x_ref, g_ref, w_ref, o_ref):
    h = _rms(x_ref[...], g_ref[...]).astype(BF16)
    o_ref[...] = _dot(h, w_ref[...]).astype(o_ref.dtype)


def _norm_matmul(x, g, w, out_dtype):
    t, d = x.shape
    n = w.shape[1]
    tm = min(512, t)
    return pl.pallas_call(
        _norm_matmul_kernel,
        grid=(t // tm,),
        in_specs=[pl.BlockSpec((tm, d), lambda i: (i, 0)),
                  pl.BlockSpec((1, d), lambda i: (0, 0)),
                  pl.BlockSpec((d, n), lambda i: (0, 0))],
        out_specs=pl.BlockSpec((tm, n), lambda i: (i, 0)),
        out_shape=jax.ShapeDtypeStruct((t, n), out_dtype),
        compiler_params=_params("parallel"),
        name="norm_matmul",
    )(x, g.reshape(1, d), w)


def _mem_kv(mem, g, w_kv):
    b, m, d = mem.shape
    kv = _norm_matmul(mem.reshape(b * m, d), g, w_kv.astype(BF16), BF16).reshape(b, m, -1)
    nk = N_MEM_HEADS * HEAD_DIM
    return kv[:, :, :nk], kv[:, :, nk:].transpose(0, 2, 1)


VR = 80

A_Q, A_K, A_V, A_QI, A_KI, A_WI, A_QM, A_END = 0, 768, 1024, 1280, 1792, 1856, 1872, 2128


def _rope_heads(p, nheads, out_ref, half, c, s, scale):
    for hh in range(nheads):
        r0 = hh * HEAD_DIM
        x1, x2 = p[r0:r0 + half], p[r0 + half:r0 + 2 * half]
        out_ref[r0:r0 + half, :] = ((x1 * c - x2 * s) * scale).astype(out_ref.dtype)
        out_ref[r0 + half:r0 + 2 * half, :] = ((x2 * c + x1 * s) * scale).astype(out_ref.dtype)
        if 2 * half < HEAD_DIM:
            out_ref[r0 + 2 * half:r0 + HEAD_DIM, :] = (p[r0 + 2 * half:r0 + HEAD_DIM] * scale).astype(out_ref.dtype)


def _write_values(pv, nheads, va_ref):
    tm = pv.shape[1]
    ones_rows = jnp.where(lax.broadcasted_iota(I32, (VR - HEAD_DIM, tm), 0) == 0, 1.0, 0.0).astype(BF16)
    for g in range(nheads):
        va_ref[g * VR:g * VR + HEAD_DIM, :] = pv[g * HEAD_DIM:(g + 1) * HEAD_DIM].astype(BF16)
        va_ref[g * VR + HEAD_DIM:(g + 1) * VR, :] = ones_rows


def _inproj_a_kernel(x_ref, g_ref, wt_ref, chd_ref, shd_ref, cix_ref, six_ref,
                     q_ref, k_ref, va_ref, qi_ref, ki_ref, wi_ref, qm_ref, kt_scr):
    h = _rms(x_ref[...], g_ref[...]).astype(BF16)
    chd, shd = chd_ref[...], shd_ref[...]
    cix, six = cix_ref[...], six_ref[...]

    def proj(a, b):
        return _dot_nt(wt_ref[a:b, :], h)

    _rope_heads(proj(A_Q, A_K), N_HEADS_A, q_ref, H_HD, chd, shd, Q_SCALE * LOG2E)
    nk = N_KV_A * HEAD_DIM
    _rope_heads(proj(A_K, A_V), N_KV_A, kt_scr.at[0:nk], H_HD, chd, shd, 1.0)
    _write_values(proj(A_V, A_QI), N_KV_A, va_ref)
    _rope_heads(proj(A_QI, A_KI), IDX_HEADS, qi_ref, H_IX, cix, six, 1.0)
    pkw = proj(A_KI, A_QM)
    _rope_heads(pkw, 1, kt_scr.at[nk:nk + IDX_DIM], H_IX, cix, six, 1.0)
    k_ref[...] = kt_scr[0:nk, :].T.astype(BF16)
    ki_ref[...] = kt_scr[nk:nk + IDX_DIM, :].T.astype(BF16)
    wi_ref[...] = pkw[IDX_DIM:IDX_DIM + IDX_HEADS] * WI_SCALE
    qm_ref[...] = (proj(A_QM, A_END) * Q_SCALE).astype(BF16)


def _inproj_a(x, g, w_in, tabs):
    t, d = x.shape
    wt = w_in.T
    pad = jnp.zeros((A_QM - A_WI - IDX_HEADS, d), w_in.dtype)
    split = A_WI + IDX_HEADS
    wt = jnp.concatenate([wt[:split], pad, wt[split:]], axis=0).astype(BF16)
    tm = min(512, t)
    col = lambda r: pl.BlockSpec((r, tm), lambda i: (0, i))
    outs = [(N_HEADS_A * HEAD_DIM, BF16), (N_KV_A * HEAD_DIM, BF16), (N_KV_A * VR, BF16),
            (IDX_HEADS * IDX_DIM, BF16), (IDX_DIM, BF16), (IDX_HEADS, F32), (N_MEM_HEADS * HEAD_DIM, BF16)]
    return pl.pallas_call(
        _inproj_a_kernel,
        grid=(t // tm,),
        in_specs=[pl.BlockSpec((tm, d), lambda i: (i, 0)), pl.BlockSpec((1, d), lambda i: (0, 0)),
                  pl.BlockSpec((A_END, d), lambda i: (0, 0)),
                  col(H_HD), col(H_HD), col(H_IX), col(H_IX)],
        out_specs=[pl.BlockSpec((tm, r), lambda i: (i, 0)) if k in (1, 4) else col(r)
                   for k, (r, _) in enumerate(outs)],
        out_shape=[jax.ShapeDtypeStruct((t, r) if k in (1, 4) else (r, t), dt) for k, (r, dt) in enumerate(outs)],
        scratch_shapes=[pltpu.VMEM((N_KV_A * HEAD_DIM + IDX_DIM, tm), F32)],
        compiler_params=_params("parallel"),
        name="inproj_a",
    )(x, g.reshape(1, d), wt, *tabs)


def _inproj_b_kernel(x_ref, g_ref, wt_ref, chd_ref, shd_ref, *out_refs):
    h = _rms(x_ref[...], g_ref[...]).astype(BF16)
    chd, shd = chd_ref[...], shd_ref[...]
    ng = len(DIL_PATTERNS)
    gw = HEADS_PER_DIL * HEAD_DIM
    for g in range(ng):
        q_ref, k_ref, va_ref = out_refs[3 * g:3 * g + 3]
        base = 3 * g * gw
        _rope_heads(_dot_nt(wt_ref[base:base + gw, :], h), HEADS_PER_DIL, q_ref, H_HD, chd, shd, Q_SCALE * LOG2E)
        _rope_heads(_dot_nt(wt_ref[base + gw:base + 2 * gw, :], h), HEADS_PER_DIL, k_ref, H_HD, chd, shd, 1.0)
        _write_values(_dot_nt(wt_ref[base + 2 * gw:base + 3 * gw, :], h), HEADS_PER_DIL, va_ref)
    out_refs[3 * ng][...] = (_dot_nt(wt_ref[3 * ng * gw:3 * ng * gw + N_MEM_HEADS * HEAD_DIM, :], h)
                            * Q_SCALE).astype(BF16)


def _inproj_b(x, g, w_in, tabs):
    t, d = x.shape
    wt = w_in.T.astype(BF16)
    tm = min(512, t)
    col = lambda r: pl.BlockSpec((r, tm), lambda i: (0, i))
    gw = HEADS_PER_DIL * HEAD_DIM
    rows = [gw, gw, HEADS_PER_DIL * VR] * len(DIL_PATTERNS) + [N_MEM_HEADS * HEAD_DIM]
    return pl.pallas_call(
        _inproj_b_kernel,
        grid=(t // tm,),
        in_specs=[pl.BlockSpec((tm, d), lambda i: (i, 0)), pl.BlockSpec((1, d), lambda i: (0, 0)),
                  pl.BlockSpec(wt.shape, lambda i: (0, 0)), col(H_HD), col(H_HD)],
        out_specs=[col(r) for r in rows],
        out_shape=[jax.ShapeDtypeStruct((r, t), BF16) for r in rows],
        compiler_params=_params("parallel"),
        name="inproj_b",
    )(x, g.reshape(1, d), wt, tabs[0], tabs[1])


TQ = 256
CR = 32
SCORE_BITS = 32


def _key_to_f32(key):
    bits = jnp.where(key < 0, key ^ jnp.int32(0x7FFFFFFF), key)
    return pltpu.bitcast(bits, F32)


def _dsa_kernel(q_ref, qi_ref, wi_ref, ki_ref, k_ref, va_ref, o_ref,
                sc_scr, tau_scr, need_scr, tie_scr, tri_scr, m_scr, alpha_scr, acc_scr, s0_scr, s1_scr,
                *, seq, n_sel):
    i = pl.program_id(1)
    nch = i + 1
    krow = lax.broadcasted_iota(I32, (TQ, TQ), 0)
    qcol = lax.broadcasted_iota(I32, (TQ, TQ), 1)

    def chunk_off(c):
        return pl.multiple_of(c * TQ, TQ)

    def score_chunk(c, diag):
        off = chunk_off(c)
        kic = ki_ref[pl.ds(off, TQ), :]
        sc = jnp.zeros((TQ, TQ), F32)
        for h in range(IDX_HEADS):
            lg = _dot(kic, qi_ref[h * IDX_DIM:(h + 1) * IDX_DIM, :])
            sc = sc + jnp.maximum(lg, 0.0) * wi_ref[h:h + 1, :]
        if diag:
            sc = jnp.where(krow > qcol, -jnp.inf, sc)
        sc_scr[pl.ds(off, TQ), :] = sc

    def score_body(c, carry):
        score_chunk(c, False)
        return carry

    lax.fori_loop(0, i, score_body, 0)
    score_chunk(i, True)

    def count(pred):
        def body(c, acc):
            off = chunk_off(c)
            ind = pred(sc_scr[pl.ds(off, TQ), :], off)
            return acc + jnp.sum(ind.reshape(TQ // CR, CR, TQ), axis=0)
        acc = lax.fori_loop(0, nch, body, jnp.zeros((CR, TQ), F32))
        return jnp.sum(acc, axis=0, keepdims=True)

    @pl.when(i * TQ < n_sel)
    def _():
        tau_scr[...] = jnp.full((1, TQ), -jnp.inf, F32)
        need_scr[...] = jnp.zeros((1, TQ), F32)

    @pl.when(i * TQ >= n_sel)
    def _():
        def body(step, carry):
            tau, cge, crej = carry
            cand = tau + jnp.left_shift(jnp.int32(1), SCORE_BITS - 1 - step)
            cand_f = _key_to_f32(cand)
            cnt = count(lambda blk, _: jnp.where(blk >= cand_f, 1.0, 0.0))
            ok = cnt >= float(n_sel)
            return jnp.where(ok, cand, tau), jnp.where(ok, cnt, cge), jnp.where(ok, crej, cnt)

        init = (jnp.full((1, TQ), INT_MIN, I32), jnp.full((1, TQ), 2.0 * n_sel, F32), jnp.zeros((1, TQ), F32))
        tau, cge, crej = lax.fori_loop(0, SCORE_BITS, body, init)
        tau_scr[...] = _key_to_f32(tau)
        need_scr[...] = jnp.where(cge > float(n_sel), float(n_sel) - crej, 2.0 * seq)

    m_scr[...] = jnp.full(m_scr.shape, NEG, F32)
    acc_scr[...] = jnp.zeros(acc_scr.shape, F32)
    tie_scr[...] = jnp.zeros((1, TQ), F32)
    tri_scr[...] = jnp.where(krow >= qcol, 1.0, 0.0).astype(BF16)
    grp = N_HEADS_A // N_KV_A
    tau_f = tau_scr[...]
    need = need_scr[...]

    def stage_a(c, s_dst):
        off = chunk_off(c)
        blk = sc_scr[pl.ds(off, TQ), :]
        eq = blk == tau_f
        rank = _dot(tri_scr[...], jnp.where(eq, 1.0, 0.0).astype(BF16)) + tie_scr[...]
        tie_scr[...] = rank[TQ - 1:TQ, :]
        bias = jnp.where(eq, jnp.where(rank <= need, 0.0, NEG), jnp.where(blk > tau_f, 0.0, NEG))
        for g in range(N_KV_A):
            kc = k_ref[pl.ds(off, TQ), g * HEAD_DIM:(g + 1) * HEAD_DIM]
            for j in range(grp):
                h = g * grp + j
                s = _dot(kc, q_ref[h * HEAD_DIM:(h + 1) * HEAD_DIM, :]) + bias
                s_dst[h] = s
                m_old = m_scr[h]
                m_new = jnp.maximum(m_old, jnp.max(s, axis=0, keepdims=True))
                alpha_scr[h] = jnp.exp2(m_old - m_new)
                m_scr[h] = m_new

    def stage_b(c, s_src):
        off = chunk_off(c)
        for g in range(N_KV_A):
            vt = va_ref[g * VR:(g + 1) * VR, pl.ds(off, TQ)]
            for j in range(grp):
                h = g * grp + j
                p = jnp.exp2(s_src[h] - m_scr[h]).astype(BF16)
                acc_scr[h] = alpha_scr[h] * acc_scr[h] + _dot(vt, p)

    def step(c, s_src, s_dst):
        stage_b(c - 1, s_src)
        stage_a(c, s_dst)

    stage_a(0, s0_scr)
    npairs = (nch - 1) // 2

    def pair_body(t, carry):
        step(2 * t + 1, s0_scr, s1_scr)
        step(2 * t + 2, s1_scr, s0_scr)
        return carry

    lax.fori_loop(0, npairs, pair_body, 0)

    @pl.when((nch - 1) % 2 == 1)
    def _():
        step(nch - 1, s0_scr, s1_scr)
        stage_b(nch - 1, s1_scr)

    @pl.when((nch - 1) % 2 == 0)
    def _():
        stage_b(nch - 1, s0_scr)

    for h in range(N_HEADS_A):
        a = acc_scr[h]
        o_ref[h * HEAD_DIM:(h + 1) * HEAD_DIM, :] = (a[:HEAD_DIM] / a[HEAD_DIM:HEAD_DIM + 1]).astype(BF16)


def _dsa_attention(qt, qit, wit, ki, k, vat, b, s):
    n_sel = min(TOPK_MAX, s // 4)
    assert s % TQ == 0 and n_sel % TQ == 0
    nq = s // TQ
    qblk = lambda r: pl.BlockSpec((r, TQ), lambda bi, i: (0, bi * nq + i))
    tok = lambda c: pl.BlockSpec((s, c), lambda bi, i: (bi, 0))
    return pl.pallas_call(
        functools.partial(_dsa_kernel, seq=s, n_sel=n_sel),
        grid=(b, nq),
        in_specs=[qblk(qt.shape[0]), qblk(qit.shape[0]), qblk(wit.shape[0]),
                  tok(ki.shape[1]), tok(k.shape[1]),
                  pl.BlockSpec((vat.shape[0], s), lambda bi, i: (0, bi))],
        out_specs=qblk(qt.shape[0]),
        out_shape=jax.ShapeDtypeStruct(qt.shape, BF16),
        scratch_shapes=[
            pltpu.VMEM((s, TQ), F32),
            pltpu.VMEM((1, TQ), F32),
            pltpu.VMEM((1, TQ), F32),
            pltpu.VMEM((1, TQ), F32),
            pltpu.VMEM((TQ, TQ), BF16),
            pltpu.VMEM((N_HEADS_A, 1, TQ), F32),
            pltpu.VMEM((N_HEADS_A, 1, TQ), F32),
            pltpu.VMEM((N_HEADS_A, VR, TQ), F32),
            pltpu.VMEM((N_HEADS_A, TQ, TQ), F32),
            pltpu.VMEM((N_HEADS_A, TQ, TQ), F32),
        ],
        compiler_params=_params("parallel", "arbitrary"),
        name="dsa_attention",
    )(qt, qit, wit, ki, k, vat)


def _mem_attn_kernel(q_ref, k_ref, v_ref, o_ref):
    for h in range(N_MEM_HEADS):
        sl = slice(h * HEAD_DIM, (h + 1) * HEAD_DIM)
        s = _dot(k_ref[:, sl], q_ref[sl, :])
        p = jnp.exp(s - jnp.max(s, axis=0, keepdims=True))
        l = jnp.sum(p, axis=0, keepdims=True)
        o_ref[sl, :] = (_dot(v_ref[sl, :], p.astype(BF16)) / l).astype(BF16)


def _mem_attention(qmt, km, vmt, s):
    c, t = qmt.shape
    m = km.shape[1]
    tm = min(512, s)
    nq = s // tm
    blk = pl.BlockSpec((c, tm), lambda bi, i: (0, bi * nq + i))
    return pl.pallas_call(
        _mem_attn_kernel,
        grid=(t // s, nq),
        in_specs=[blk,
                  pl.BlockSpec((None, m, km.shape[2]), lambda bi, i: (bi, 0, 0)),
                  pl.BlockSpec((None, vmt.shape[1], m), lambda bi, i: (bi, 0, 0))],
        out_specs=blk,
        out_shape=jax.ShapeDtypeStruct((c, t), BF16),
        compiler_params=_params("parallel", "parallel"),
        name="mem_attention",
    )(qmt, km, vmt)


QB = 512
LN2 = 0.6931471805599453


def _band_kernel(q_ref, kp_ref, kc_ref, vp_ref, vc_ref, o_ref, lse_ref, s_scr, m_scr, *, qb):
    j = pl.program_id(1)
    nsub = qb // BLK
    krow = lax.broadcasted_iota(I32, (BLK, BLK), 0)
    qcol = lax.broadcasted_iota(I32, (BLK, BLK), 1)
    bias_prev = jnp.where(krow >= qcol, 0.0, NEG)
    bias_cur = jnp.where(krow <= qcol, 0.0, NEG)
    no_prev = jnp.where(j > 0, 0.0, NEG)

    def stage_a(sb):
        qs = slice(sb * BLK, (sb + 1) * BLK)
        for h in range(HEADS_PER_DIL):
            hs = slice(h * HEAD_DIM, (h + 1) * HEAD_DIM)
            qh = q_ref[hs, qs]
            if sb == 0:
                s_p = _dot(kp_ref[:, hs], qh) + (bias_prev + no_prev)
            else:
                s_p = _dot(kc_ref[(sb - 1) * BLK:sb * BLK, hs], qh) + bias_prev
            s_c = _dot(kc_ref[qs, hs], qh) + bias_cur
            s_scr[sb % 2, h, 0:BLK] = s_p
            s_scr[sb % 2, h, BLK:2 * BLK] = s_c
            m_scr[sb % 2, h] = jnp.maximum(jnp.max(s_p, axis=0, keepdims=True),
                                           jnp.max(s_c, axis=0, keepdims=True))

    def stage_b(sb):
        qs = slice(sb * BLK, (sb + 1) * BLK)
        for h in range(HEADS_PER_DIL):
            vs = slice(h * VR, (h + 1) * VR)
            m = m_scr[sb % 2, h]
            p_p = jnp.exp2(s_scr[sb % 2, h, 0:BLK] - m).astype(BF16)
            p_c = jnp.exp2(s_scr[sb % 2, h, BLK:2 * BLK] - m).astype(BF16)
            v_p = vp_ref[vs, :] if sb == 0 else vc_ref[vs, (sb - 1) * BLK:sb * BLK]
            acc = _dot(v_p, p_p) + _dot(vc_ref[vs, qs], p_c)
            l = acc[HEAD_DIM:HEAD_DIM + 1]
            o_ref[h * HEAD_DIM:(h + 1) * HEAD_DIM, qs] = (acc[:HEAD_DIM] / l).astype(o_ref.dtype)
            lse_ref[h:h + 1, qs] = m * LN2 + jnp.log(l)

    stage_a(0)
    for sb in range(1, nsub):
        stage_b(sb - 1)
        stage_a(sb)
    stage_b(nsub - 1)


def _band_attention(qt, k, vat, n):
    c, total = qt.shape
    qb = min(QB, n)
    assert n % qb == 0 and qb % BLK == 0 and total % n == 0
    r, nj = qb // BLK, n // qb
    prev_blk = lambda si, j: si * (n // BLK) + jnp.maximum(j * r - 1, 0)
    fm = lambda rows: pl.BlockSpec((rows, qb), lambda si, j: (0, si * nj + j))
    fm_prev = lambda rows: pl.BlockSpec((rows, BLK), lambda si, j: (0, prev_blk(si, j)))
    return pl.pallas_call(
        functools.partial(_band_kernel, qb=qb),
        grid=(total // n, nj),
        in_specs=[fm(c),
                  pl.BlockSpec((BLK, c), lambda si, j: (prev_blk(si, j), 0)),
                  pl.BlockSpec((qb, c), lambda si, j: (si * nj + j, 0)),
                  fm_prev(vat.shape[0]), fm(vat.shape[0])],
        out_specs=[fm(c), fm(HEADS_PER_DIL)],
        out_shape=[jax.ShapeDtypeStruct((c, total), BF16), jax.ShapeDtypeStruct((HEADS_PER_DIL, total), F32)],
        scratch_shapes=[pltpu.VMEM((2, HEADS_PER_DIL, 2 * BLK, BLK), F32),
                        pltpu.VMEM((2, HEADS_PER_DIL, 1, BLK), F32)],
        compiler_params=_params("parallel", "parallel"),
        name="band_attention",
    )(qt, k, k, vat, vat)


def _merge_kernel(*refs):
    ng = len(DIL_PATTERNS)
    o_refs, l_refs, out_ref = refs[:ng], refs[ng:2 * ng], refs[2 * ng]
    lses = [r[...] for r in l_refs]
    m = functools.reduce(jnp.maximum, lses)
    es = [jnp.exp(l - m) for l in lses]
    den = sum(es)
    ws = [e / den for e in es]
    for h in range(HEADS_PER_DIL):
        hs = slice(h * HEAD_DIM, (h + 1) * HEAD_DIM)
        out_ref[hs, :] = sum(w[h:h + 1] * o[hs, :].astype(F32) for w, o in zip(ws, o_refs)).astype(out_ref.dtype)


def _merge_groups(os_, lses):
    c, t = os_[0].shape
    tm = min(2048, t)
    spec = lambda r: pl.BlockSpec((r, tm), lambda i: (0, i))
    return pl.pallas_call(
        _merge_kernel,
        grid=(t // tm,),
        in_specs=[spec(c)] * len(os_) + [spec(HEADS_PER_DIL)] * len(lses),
        out_specs=spec(c),
        out_shape=jax.ShapeDtypeStruct((c, t), BF16),
        compiler_params=_params("parallel"),
        name="merge_groups",
    )(*os_, *lses)


def _split_residues(a, b, dil):
    f, t = a.shape
    return a.reshape(f, b, t // (b * dil), dil).transpose(0, 1, 3, 2)


def _to_sub_fm(a, b, dil):
    return a if dil == 1 else _split_residues(a, b, dil).reshape(a.shape)


def _to_sub_tok(a, b, dil):
    return a.T if dil == 1 else _split_residues(a, b, dil).transpose(1, 2, 3, 0).reshape(a.shape[1], a.shape[0])


def _from_sub_fm(a, b, dil):
    if dil == 1:
        return a
    f, t = a.shape
    return a.reshape(f, b, dil, t // (b * dil)).transpose(0, 1, 3, 2).reshape(f, t)


def _ffn_kernel(x_ref, mix_ref, mo_ref, wo1_ref, wo2_ref, g_ref, wgu_ref, wd_ref, gf_ref,
                o_ref, act_scr, *, final_norm, tf):
    dff = wd_ref.shape[0]
    x2 = x_ref[...] + _dot_tn(mix_ref[...], wo1_ref[...]) + _dot_tn(mo_ref[...], wo2_ref[...])
    h = _rms(x2, g_ref[...]).astype(BF16)
    for f in range(dff // tf):
        gate = _dot(h, wgu_ref[:, f * tf:(f + 1) * tf])
        up = _dot(h, wgu_ref[:, dff + f * tf:dff + (f + 1) * tf])
        act_scr[:, f * tf:(f + 1) * tf] = (gate * jax.nn.sigmoid(gate) * up).astype(BF16)
    y = x2 + _dot(act_scr[...], wd_ref[...])
    if final_norm:
        y = _rms(y, gf_ref[...])
    o_ref[...] = y


def _out_ffn(x, mix, mo, w_out, g_ffn, w_gate_up, w_down, g_final, final_norm):
    t, d = x.shape
    cm, cmo = mix.shape[0], mo.shape[0]
    dff = w_down.shape[0]
    wo1 = w_out[:cm].astype(BF16)
    wo2 = w_out[cm:].astype(BF16)
    wgu = w_gate_up.astype(BF16)
    wd = w_down.astype(BF16)
    tm = min(512, t)
    tf = 256 if dff % 256 == 0 else dff
    row = lambda c: pl.BlockSpec((tm, c), lambda i: (i, 0))
    const = lambda r, c: pl.BlockSpec((r, c), lambda i: (0, 0), pipeline_mode=pl.Buffered(1))
    return pl.pallas_call(
        functools.partial(_ffn_kernel, final_norm=final_norm, tf=tf),
        grid=(t // tm,),
        in_specs=[row(d), pl.BlockSpec((cm, tm), lambda i: (0, i)), pl.BlockSpec((cmo, tm), lambda i: (0, i)),
                  const(cm, d), const(cmo, d), const(1, d),
                  const(d, 2 * dff), const(dff, d), const(1, d)],
        out_specs=row(d),
        out_shape=jax.ShapeDtypeStruct((t, d), F32),
        scratch_shapes=[pltpu.VMEM((tm, dff), BF16)],
        compiler_params=_params("parallel"),
        name="out_ffn",
    )(x, mix, mo, wo1, wo2, g_ffn.reshape(1, d), wgu, wd, g_final.reshape(1, d))


def kernel(x, mem, positions,
           l0_norm_mix, l0_norm_mem, l0_w_in, l0_w_mem_kv, l0_w_out, l0_norm_ffn, l0_w_gate_up, l0_w_down,
           l1_norm_mix, l1_norm_mem, l1_w_in, l1_w_mem_kv, l1_w_out, l1_norm_ffn, l1_w_gate_up, l1_w_down,
           final_norm):
    b, s, d = x.shape
    t = b * s
    xt = x.reshape(t, d)
    tabs = _trig_tables(positions.reshape(1, t))

    qt, k, vat, qit, ki, wit, qmt = _inproj_a(xt, l0_norm_mix, l0_w_in, tabs)
    mix = _dsa_attention(qt, qit, wit, ki, k, vat, b, s)
    mo = _mem_attention(qmt, *_mem_kv(mem, l0_norm_mem, l0_w_mem_kv), s)
    xt = _out_ffn(xt, mix, mo, l0_w_out, l0_norm_ffn, l0_w_gate_up, l0_w_down, final_norm, False)

    outs = _inproj_b(xt, l1_norm_mix, l1_w_in, tabs)
    os_, lses = [], []
    for g, (window, dil) in enumerate(DIL_PATTERNS):
        assert window // dil == BLK
        qg, kg, vag = outs[3 * g:3 * g + 3]
        o, lse = _band_attention(_to_sub_fm(qg, b, dil), _to_sub_tok(kg, b, dil), _to_sub_fm(vag, b, dil), s // dil)
        os_.append(_from_sub_fm(o, b, dil))
        lses.append(_from_sub_fm(lse, b, dil))
    mix = _merge_groups(os_, lses)
    mo = _mem_attention(outs[-1], *_mem_kv(mem, l1_norm_mem, l1_w_mem_kv), s)
    xt = _out_ffn(xt, mix, mo, l1_w_out, l1_norm_ffn, l1_w_gate_up, l1_w_down, final_norm, True)
    return xt.reshape(b, s, d)
```

```python
import functools

import jax
import jax.numpy as jnp
from jax import lax
from jax.experimental import pallas as pl
from jax.experimental.pallas import tpu as pltpu

F32 = jnp.float32
BF16 = jnp.bfloat16
I32 = jnp.int32

HEAD_DIM = 64
N_HEADS_A = 12
N_KV_A = 4
IDX_HEADS = 8
IDX_DIM = 64
IDX_ROPE_DIM = 32
TOPK_MAX = 256
DIL_PATTERNS = ((128, 1), (512, 4), (2048, 16))
HEADS_PER_DIL = 4
N_MEM_HEADS = 4
BLK = 128
ROPE_THETA = 10000.0
EPS = 1e-6
NEG = -1e30
INT_MIN = -2147483648

LANES = 128
VMEM_LIMIT = 56 * 1024 * 1024

Q_SCALE = HEAD_DIM ** -0.5
WI_SCALE = IDX_HEADS ** -0.5 * IDX_DIM ** -0.5
LOG2E = 1.4426950408889634


def _dot(a, b):
    return jnp.dot(a, b, preferred_element_type=F32)


def _dot_nt(a, b):
    return lax.dot_general(a, b, (((1,), (1,)), ((), ())), preferred_element_type=F32)


def _dot_tn(a, b):
    return lax.dot_general(a, b, (((0,), (0,)), ((), ())), preferred_element_type=F32)


def _params(*sem):
    return pltpu.CompilerParams(dimension_semantics=sem, vmem_limit_bytes=VMEM_LIMIT)


def _rms(x, g):
    ms = jnp.mean(x * x, axis=-1, keepdims=True)
    return x * lax.rsqrt(ms + EPS) * g


H_HD = HEAD_DIM // 2
H_IX = IDX_ROPE_DIM // 2


def _trig_kernel(pos_ref, f_ref, chd_ref, shd_ref, cix_ref, six_ref):
    tm = pos_ref.shape[1]
    pos = pos_ref[...].astype(F32)
    f = jnp.concatenate([f_ref[...]] * (tm // LANES), axis=1)
    ang = f * pos
    c, s = jnp.cos(ang), jnp.sin(ang)
    chd_ref[...] = c[:H_HD]
    shd_ref[...] = s[:H_HD]
    cix_ref[...] = c[H_HD:]
    six_ref[...] = s[H_HD:]


def _trig_tables(pos_row):
    t = pos_row.shape[1]
    tm = min(2048, t)
    f_hd = ROPE_THETA ** (-jnp.arange(H_HD, dtype=F32) / H_HD)
    f_ix = ROPE_THETA ** (-jnp.arange(H_IX, dtype=F32) / H_IX)
    f = jnp.broadcast_to(jnp.concatenate([f_hd, f_ix])[:, None], (H_HD + H_IX, LANES))
    spec = lambda r: pl.BlockSpec((r, tm), lambda i: (0, i))
    rows = [H_HD, H_HD, H_IX, H_IX]
    return pl.pallas_call(
        _trig_kernel,
        grid=(t // tm,),
        in_specs=[spec(1), pl.BlockSpec((H_HD + H_IX, LANES), lambda i: (0, 0))],
        out_specs=[spec(r) for r in rows],
        out_shape=[jax.ShapeDtypeStruct((r, t), F32) for r in rows],
        compiler_params=_params("parallel"),
        name="rope_tables",
    )(pos_row, f)


def _norm_matmul_kernel(x_ref, g_ref, w_ref, o_ref):
    h = _rms(x_ref[...], g_ref[...]).astype(BF16)
    o_ref[...] = _dot(h, w_ref[...]).astype(o_ref.dtype)


def _norm_matmul(x, g, w, out_dtype):
    t, d = x.shape
    n = w.shape[1]
    tm = min(512, t)
    return pl.pallas_call(
        _norm_matmul_kernel,
        grid=(t // tm,),
        in_specs=[pl.BlockSpec((tm, d), lambda i: (i, 0)),
                  pl.BlockSpec((1, d), lambda i: (0, 0)),
                  pl.BlockSpec((d, n), lambda i: (0, 0))],
        out_specs=pl.BlockSpec((tm, n), lambda i: (i, 0)),
        out_shape=jax.ShapeDtypeStruct((t, n), out_dtype),
        compiler_params=_params("parallel"),
        name="norm_matmul",
    )(x, g.reshape(1, d), w)


def _mem_kv(mem, g, w_kv):
    b, m, d = mem.shape
    kv = _norm_matmul(mem.reshape(b * m, d), g, w_kv.astype(BF16), BF16).reshape(b, m, -1)
    nk = N_MEM_HEADS * HEAD_DIM
    return kv[:, :, :nk], kv[:, :, nk:].transpose(0, 2, 1)


VR = 80

A_Q, A_K, A_V, A_QI, A_KI, A_WI, A_QM, A_END = 0, 768, 1024, 1280, 1792, 1856, 1872, 2128


def _rope_heads(p, nheads, out_ref, half, c, s, scale):
    for hh in range(nheads):
        r0 = hh * HEAD_DIM
        x1, x2 = p[r0:r0 + half], p[r0 + half:r0 + 2 * half]
        out_ref[r0:r0 + half, :] = ((x1 * c - x2 * s) * scale).astype(out_ref.dtype)
        out_ref[r0 + half:r0 + 2 * half, :] = ((x2 * c + x1 * s) * scale).astype(out_ref.dtype)
        if 2 * half < HEAD_DIM:
            out_ref[r0 + 2 * half:r0 + HEAD_DIM, :] = (p[r0 + 2 * half:r0 + HEAD_DIM] * scale).astype(out_ref.dtype)


def _write_values(pv, nheads, va_ref):
    tm = pv.shape[1]
    ones_rows = jnp.where(lax.broadcasted_iota(I32, (VR - HEAD_DIM, tm), 0) == 0, 1.0, 0.0).astype(BF16)
    for g in range(nheads):
        va_ref[g * VR:g * VR + HEAD_DIM, :] = pv[g * HEAD_DIM:(g + 1) * HEAD_DIM].astype(BF16)
        va_ref[g * VR + HEAD_DIM:(g + 1) * VR, :] = ones_rows


def _inproj_a_kernel(x_ref, g_ref, wt_ref, chd_ref, shd_ref, cix_ref, six_ref,
                     q_ref, k_ref, va_ref, qi_ref, ki_ref, wi_ref, qm_ref, kn_ref, kt_scr):
    h = _rms(x_ref[...], g_ref[...]).astype(BF16)
    chd, shd = chd_ref[...], shd_ref[...]
    cix, six = cix_ref[...], six_ref[...]

    def proj(a, b):
        return _dot_nt(wt_ref[a:b, :], h)

    _rope_heads(proj(A_Q, A_K), N_HEADS_A, q_ref, H_HD, chd, shd, Q_SCALE * LOG2E)
    nk = N_KV_A * HEAD_DIM
    _rope_heads(proj(A_K, A_V), N_KV_A, kt_scr.at[0:nk], H_HD, chd, shd, 1.0)
    _write_values(proj(A_V, A_QI), N_KV_A, va_ref)
    _rope_heads(proj(A_QI, A_KI), IDX_HEADS, qi_ref, H_IX, cix, six, 1.0)
    pkw = proj(A_KI, A_QM)
    _rope_heads(pkw, 1, kt_scr.at[nk:nk + IDX_DIM], H_IX, cix, six, 1.0)
    k_ref[...] = kt_scr[0:nk, :].T.astype(BF16)
    ki_ref[...] = kt_scr[nk:nk + IDX_DIM, :].T.astype(BF16)
    for g in range(N_KV_A):
        kg = kt_scr[g * HEAD_DIM:(g + 1) * HEAD_DIM, :]
        kn_ref[g:g + 1, :] = jnp.sum(kg * kg, axis=0, keepdims=True)
    wi_ref[...] = pkw[IDX_DIM:IDX_DIM + IDX_HEADS] * WI_SCALE
    qm_ref[...] = (proj(A_QM, A_END) * Q_SCALE).astype(BF16)


def _inproj_a(x, g, w_in, tabs):
    t, d = x.shape
    wt = w_in.T
    pad = jnp.zeros((A_QM - A_WI - IDX_HEADS, d), w_in.dtype)
    split = A_WI + IDX_HEADS
    wt = jnp.concatenate([wt[:split], pad, wt[split:]], axis=0).astype(BF16)
    tm = min(512, t)
    col = lambda r: pl.BlockSpec((r, tm), lambda i: (0, i))
    outs = [(N_HEADS_A * HEAD_DIM, BF16), (N_KV_A * HEAD_DIM, BF16), (N_KV_A * VR, BF16),
            (IDX_HEADS * IDX_DIM, BF16), (IDX_DIM, BF16), (IDX_HEADS, F32), (N_MEM_HEADS * HEAD_DIM, BF16),
            (N_KV_A, F32)]
    return pl.pallas_call(
        _inproj_a_kernel,
        grid=(t // tm,),
        in_specs=[pl.BlockSpec((tm, d), lambda i: (i, 0)), pl.BlockSpec((1, d), lambda i: (0, 0)),
                  pl.BlockSpec((A_END, d), lambda i: (0, 0)),
                  col(H_HD), col(H_HD), col(H_IX), col(H_IX)],
        out_specs=[pl.BlockSpec((tm, r), lambda i: (i, 0)) if k in (1, 4) else col(r)
                   for k, (r, _) in enumerate(outs)],
        out_shape=[jax.ShapeDtypeStruct((t, r) if k in (1, 4) else (r, t), dt) for k, (r, dt) in enumerate(outs)],
        scratch_shapes=[pltpu.VMEM((N_KV_A * HEAD_DIM + IDX_DIM, tm), F32)],
        compiler_params=_params("parallel"),
        name="inproj_a",
    )(x, g.reshape(1, d), wt, *tabs)


def _inproj_b_kernel(x_ref, g_ref, wt_ref, chd_ref, shd_ref, *out_refs):
    h = _rms(x_ref[...], g_ref[...]).astype(BF16)
    chd, shd = chd_ref[...], shd_ref[...]
    ng = len(DIL_PATTERNS)
    gw = HEADS_PER_DIL * HEAD_DIM
    for g in range(ng):
        q_ref, k_ref, va_ref = out_refs[3 * g:3 * g + 3]
        base = 3 * g * gw
        _rope_heads(_dot_nt(wt_ref[base:base + gw, :], h), HEADS_PER_DIL, q_ref, H_HD, chd, shd, Q_SCALE * LOG2E)
        _rope_heads(_dot_nt(wt_ref[base + gw:base + 2 * gw, :], h), HEADS_PER_DIL, k_ref, H_HD, chd, shd, 1.0)
        _write_values(_dot_nt(wt_ref[base + 2 * gw:base + 3 * gw, :], h), HEADS_PER_DIL, va_ref)
    out_refs[3 * ng][...] = (_dot_nt(wt_ref[3 * ng * gw:3 * ng * gw + N_MEM_HEADS * HEAD_DIM, :], h)
                            * Q_SCALE).astype(BF16)


def _inproj_b(x, g, w_in, tabs):
    t, d = x.shape
    wt = w_in.T.astype(BF16)
    tm = min(512, t)
    col = lambda r: pl.BlockSpec((r, tm), lambda i: (0, i))
    gw = HEADS_PER_DIL * HEAD_DIM
    rows = [gw, gw, HEADS_PER_DIL * VR] * len(DIL_PATTERNS) + [N_MEM_HEADS * HEAD_DIM]
    return pl.pallas_call(
        _inproj_b_kernel,
        grid=(t // tm,),
        in_specs=[pl.BlockSpec((tm, d), lambda i: (i, 0)), pl.BlockSpec((1, d), lambda i: (0, 0)),
                  pl.BlockSpec(wt.shape, lambda i: (0, 0)), col(H_HD), col(H_HD)],
        out_specs=[col(r) for r in rows],
        out_shape=[jax.ShapeDtypeStruct((r, t), BF16) for r in rows],
        compiler_params=_params("parallel"),
        name="inproj_b",
    )(x, g.reshape(1, d), wt, tabs[0], tabs[1])


TQ = 256
CR = 32
SCORE_BITS = 32
BOUND_SLACK = 1.01
UNDERFLOW_GUARD = 2.0 ** -100


def _key_to_f32(key):
    bits = jnp.where(key < 0, key ^ jnp.int32(0x7FFFFFFF), key)
    return pltpu.bitcast(bits, F32)


def _dsa_kernel(q_ref, qi_ref, wi_ref, ki_ref, k_ref, va_ref, kn_ref, o_ref,
                sc_scr, tau_scr, need_scr, tie_scr, tri_scr, bound_scr, m_scr, alpha_scr, acc_scr, s0_scr, s1_scr, p0_scr, p1_scr,
                *, seq, n_sel):
    i = pl.program_id(1)
    nch = i + 1
    krow = lax.broadcasted_iota(I32, (TQ, TQ), 0)
    qcol = lax.broadcasted_iota(I32, (TQ, TQ), 1)

    def chunk_off(c):
        return pl.multiple_of(c * TQ, TQ)

    def score_chunk(c, diag):
        off = chunk_off(c)
        kic = ki_ref[pl.ds(off, TQ), :]
        sc = jnp.zeros((TQ, TQ), F32)
        for h in range(IDX_HEADS):
            lg = _dot(kic, qi_ref[h * IDX_DIM:(h + 1) * IDX_DIM, :])
            sc = sc + jnp.maximum(lg, 0.0) * wi_ref[h:h + 1, :]
        if diag:
            sc = jnp.where(krow > qcol, -jnp.inf, sc)
        sc_scr[pl.ds(off, TQ), :] = sc

    def score_body(c, carry):
        score_chunk(c, False)
        return carry

    lax.fori_loop(0, i, score_body, 0)
    score_chunk(i, True)

    def count(pred):
        def body(c, acc):
            off = chunk_off(c)
            ind = pred(sc_scr[pl.ds(off, TQ), :], off)
            return acc + jnp.sum(ind.reshape(TQ // CR, CR, TQ), axis=0)
        acc = lax.fori_loop(0, nch, body, jnp.zeros((CR, TQ), F32))
        return jnp.sum(acc, axis=0, keepdims=True)

    @pl.when(i * TQ < n_sel)
    def _():
        tau_scr[...] = jnp.full((1, TQ), -jnp.inf, F32)
        need_scr[...] = jnp.zeros((1, TQ), F32)

    @pl.when(i * TQ >= n_sel)
    def _():
        def body(step, carry):
            tau, cge, crej = carry
            cand = tau + jnp.left_shift(jnp.int32(1), SCORE_BITS - 1 - step)
            cand_f = _key_to_f32(cand)
            cnt = count(lambda blk, _: jnp.where(blk >= cand_f, 1.0, 0.0))
            ok = cnt >= float(n_sel)
            return jnp.where(ok, cand, tau), jnp.where(ok, cnt, cge), jnp.where(ok, crej, cnt)

        init = (jnp.full((1, TQ), INT_MIN, I32), jnp.full((1, TQ), 2.0 * n_sel, F32), jnp.zeros((1, TQ), F32))
        tau, cge, crej = lax.fori_loop(0, SCORE_BITS, body, init)
        tau_scr[...] = _key_to_f32(tau)
        need_scr[...] = jnp.where(cge > float(n_sel), float(n_sel) - crej, 2.0 * seq)

    tri_scr[...] = jnp.where(krow >= qcol, 1.0, 0.0).astype(BF16)
    grp = N_HEADS_A // N_KV_A
    tau_f = tau_scr[...]
    need = need_scr[...]

    def chunk_select(c):
        blk = sc_scr[pl.ds(chunk_off(c), TQ), :]
        eq = blk == tau_f
        rank = _dot(tri_scr[...], jnp.where(eq, 1.0, 0.0).astype(BF16)) + tie_scr[...]
        tie_scr[...] = rank[TQ - 1:TQ, :]
        return jnp.where(eq, jnp.where(rank <= need, 1.0, 0.0), jnp.where(blk > tau_f, 1.0, 0.0))

    kmax2 = jnp.max(kn_ref[...], axis=1, keepdims=True)
    for h in range(N_HEADS_A):
        qf = q_ref[h * HEAD_DIM:(h + 1) * HEAD_DIM, :].astype(F32)
        qn2 = jnp.sum(qf * qf, axis=0, keepdims=True)
        bound_scr[h] = jnp.sqrt(qn2 * kmax2[h // grp:h // grp + 1]) * BOUND_SLACK
    acc_scr[...] = jnp.zeros(acc_scr.shape, F32)
    tie_scr[...] = jnp.zeros((1, TQ), F32)

    def pipeline(stage_a, stage_b, buf0, buf1):
        def step(c, src, dst):
            stage_b(c - 1, src)
            stage_a(c, dst)

        stage_a(0, buf0)

        def pair_body(t, carry):
            step(2 * t + 1, buf0, buf1)
            step(2 * t + 2, buf1, buf0)
            return carry

        lax.fori_loop(0, (nch - 1) // 2, pair_body, 0)

        @pl.when((nch - 1) % 2 == 1)
        def _():
            step(nch - 1, buf0, buf1)
            stage_b(nch - 1, buf1)

        @pl.when((nch - 1) % 2 == 0)
        def _():
            stage_b(nch - 1, buf0)

    def fast_a(c, p_dst):
        off = chunk_off(c)
        sel = chunk_select(c).astype(BF16)
        for g in range(N_KV_A):
            kc = k_ref[pl.ds(off, TQ), g * HEAD_DIM:(g + 1) * HEAD_DIM]
            for j in range(grp):
                h = g * grp + j
                s = _dot(kc, q_ref[h * HEAD_DIM:(h + 1) * HEAD_DIM, :])
                p_dst[h] = jnp.exp2(s - bound_scr[h]).astype(BF16) * sel

    def fast_b(c, p_src):
        off = chunk_off(c)
        for g in range(N_KV_A):
            vt = va_ref[g * VR:(g + 1) * VR, pl.ds(off, TQ)]
            for j in range(grp):
                h = g * grp + j
                acc_scr[h] += _dot(vt, p_src[h])

    pipeline(fast_a, fast_b, p0_scr, p1_scr)
    lmin = functools.reduce(jnp.minimum, [acc_scr[h, HEAD_DIM:HEAD_DIM + 1, :] for h in range(N_HEADS_A)])

    def stage_a(c, s_dst):
        off = chunk_off(c)
        bias = (chunk_select(c) - 1.0) * (-NEG)
        for g in range(N_KV_A):
            kc = k_ref[pl.ds(off, TQ), g * HEAD_DIM:(g + 1) * HEAD_DIM]
            for j in range(grp):
                h = g * grp + j
                s = _dot(kc, q_ref[h * HEAD_DIM:(h + 1) * HEAD_DIM, :]) + bias
                s_dst[h] = s
                m_old = m_scr[h]
                m_new = jnp.maximum(m_old, jnp.max(s, axis=0, keepdims=True))
                alpha_scr[h] = jnp.exp2(m_old - m_new)
                m_scr[h] = m_new

    def stage_b(c, s_src):
        off = chunk_off(c)
        for g in range(N_KV_A):
            vt = va_ref[g * VR:(g + 1) * VR, pl.ds(off, TQ)]
            for j in range(grp):
                h = g * grp + j
                p = jnp.exp2(s_src[h] - m_scr[h]).astype(BF16)
                acc_scr[h] = alpha_scr[h] * acc_scr[h] + _dot(vt, p)

    @pl.when(jnp.logical_not(jnp.min(lmin) > UNDERFLOW_GUARD))
    def _():
        m_scr[...] = jnp.full(m_scr.shape, NEG, F32)
        acc_scr[...] = jnp.zeros(acc_scr.shape, F32)
        tie_scr[...] = jnp.zeros((1, TQ), F32)
        pipeline(stage_a, stage_b, s0_scr, s1_scr)

    for h in range(N_HEADS_A):
        a = acc_scr[h]
        o_ref[h * HEAD_DIM:(h + 1) * HEAD_DIM, :] = (a[:HEAD_DIM] / a[HEAD_DIM:HEAD_DIM + 1]).astype(BF16)


def _dsa_attention(qt, qit, wit, ki, k, vat, kn, b, s):
    n_sel = min(TOPK_MAX, s // 4)
    assert s % TQ == 0 and n_sel % TQ == 0
    nq = s // TQ
    qblk = lambda r: pl.BlockSpec((r, TQ), lambda bi, i: (0, bi * nq + i))
    tok = lambda c: pl.BlockSpec((s, c), lambda bi, i: (bi, 0))
    return pl.pallas_call(
        functools.partial(_dsa_kernel, seq=s, n_sel=n_sel),
        grid=(b, nq),
        in_specs=[qblk(qt.shape[0]), qblk(qit.shape[0]), qblk(wit.shape[0]),
                  tok(ki.shape[1]), tok(k.shape[1]),
                  pl.BlockSpec((vat.shape[0], s), lambda bi, i: (0, bi)),
                  pl.BlockSpec((kn.shape[0], s), lambda bi, i: (0, bi))],
        out_specs=qblk(qt.shape[0]),
        out_shape=jax.ShapeDtypeStruct(qt.shape, BF16),
        scratch_shapes=[
            pltpu.VMEM((s, TQ), F32),
            pltpu.VMEM((1, TQ), F32),
            pltpu.VMEM((1, TQ), F32),
            pltpu.VMEM((1, TQ), F32),
            pltpu.VMEM((TQ, TQ), BF16),
            pltpu.VMEM((N_HEADS_A, 1, TQ), F32),
            pltpu.VMEM((N_HEADS_A, 1, TQ), F32),
            pltpu.VMEM((N_HEADS_A, 1, TQ), F32),
            pltpu.VMEM((N_HEADS_A, VR, TQ), F32),
            pltpu.VMEM((N_HEADS_A, TQ, TQ), F32),
            pltpu.VMEM((N_HEADS_A, TQ, TQ), F32),
            pltpu.VMEM((N_HEADS_A, TQ, TQ), BF16),
            pltpu.VMEM((N_HEADS_A, TQ, TQ), BF16),
        ],
        compiler_params=_params("parallel", "arbitrary"),
        name="dsa_attention",
    )(qt, qit, wit, ki, k, vat, kn)


def _mem_attn_kernel(q_ref, k_ref, v_ref, o_ref):
    for h in range(N_MEM_HEADS):
        sl = slice(h * HEAD_DIM, (h + 1) * HEAD_DIM)
        s = _dot(k_ref[:, sl], q_ref[sl, :])
        p = jnp.exp(s - jnp.max(s, axis=0, keepdims=True))
        l = jnp.sum(p, axis=0, keepdims=True)
        o_ref[sl, :] = (_dot(v_ref[sl, :], p.astype(BF16)) / l).astype(BF16)


def _mem_attention(qmt, km, vmt, s):
    c, t = qmt.shape
    m = km.shape[1]
    tm = min(512, s)
    nq = s // tm
    blk = pl.BlockSpec((c, tm), lambda bi, i: (0, bi * nq + i))
    return pl.pallas_call(
        _mem_attn_kernel,
        grid=(t // s, nq),
        in_specs=[blk,
                  pl.BlockSpec((None, m, km.shape[2]), lambda bi, i: (bi, 0, 0)),
                  pl.BlockSpec((None, vmt.shape[1], m), lambda bi, i: (bi, 0, 0))],
        out_specs=blk,
        out_shape=jax.ShapeDtypeStruct((c, t), BF16),
        compiler_params=_params("parallel", "parallel"),
        name="mem_attention",
    )(qmt, km, vmt)


QB = 512
LN2 = 0.6931471805599453


def _band_kernel(q_ref, kp_ref, kc_ref, vp_ref, vc_ref, o_ref, lse_ref, s_scr, m_scr, *, qb):
    j = pl.program_id(1)
    nsub = qb // BLK
    krow = lax.broadcasted_iota(I32, (BLK, BLK), 0)
    qcol = lax.broadcasted_iota(I32, (BLK, BLK), 1)
    bias_prev = jnp.where(krow >= qcol, 0.0, NEG)
    bias_cur = jnp.where(krow <= qcol, 0.0, NEG)
    no_prev = jnp.where(j > 0, 0.0, NEG)

    def stage_a(sb):
        qs = slice(sb * BLK, (sb + 1) * BLK)
        for h in range(HEADS_PER_DIL):
            hs = slice(h * HEAD_DIM, (h + 1) * HEAD_DIM)
            qh = q_ref[hs, qs]
            if sb == 0:
                s_p = _dot(kp_ref[:, hs], qh) + (bias_prev + no_prev)
            else:
                s_p = _dot(kc_ref[(sb - 1) * BLK:sb * BLK, hs], qh) + bias_prev
            s_c = _dot(kc_ref[qs, hs], qh) + bias_cur
            s_scr[sb % 2, h, 0:BLK] = s_p
            s_scr[sb % 2, h, BLK:2 * BLK] = s_c
            m_scr[sb % 2, h] = jnp.maximum(jnp.max(s_p, axis=0, keepdims=True),
                                           jnp.max(s_c, axis=0, keepdims=True))

    def stage_b(sb):
        qs = slice(sb * BLK, (sb + 1) * BLK)
        for h in range(HEADS_PER_DIL):
            vs = slice(h * VR, (h + 1) * VR)
            m = m_scr[sb % 2, h]
            p_p = jnp.exp2(s_scr[sb % 2, h, 0:BLK] - m).astype(BF16)
            p_c = jnp.exp2(s_scr[sb % 2, h, BLK:2 * BLK] - m).astype(BF16)
            v_p = vp_ref[vs, :] if sb == 0 else vc_ref[vs, (sb - 1) * BLK:sb * BLK]
            acc = _dot(v_p, p_p) + _dot(vc_ref[vs, qs], p_c)
            l = acc[HEAD_DIM:HEAD_DIM + 1]
            o_ref[h * HEAD_DIM:(h + 1) * HEAD_DIM, qs] = (acc[:HEAD_DIM] / l).astype(o_ref.dtype)
            lse_ref[h:h + 1, qs] = m * LN2 + jnp.log(l)

    stage_a(0)
    for sb in range(1, nsub):
        stage_b(sb - 1)
        stage_a(sb)
    stage_b(nsub - 1)


def _band_attention(qt, k, vat, n):
    c, total = qt.shape
    qb = min(QB, n)
    assert n % qb == 0 and qb % BLK == 0 and total % n == 0
    r, nj = qb // BLK, n // qb
    prev_blk = lambda si, j: si * (n // BLK) + jnp.maximum(j * r - 1, 0)
    fm = lambda rows: pl.BlockSpec((rows, qb), lambda si, j: (0, si * nj + j))
    fm_prev = lambda rows: pl.BlockSpec((rows, BLK), lambda si, j: (0, prev_blk(si, j)))
    return pl.pallas_call(
        functools.partial(_band_kernel, qb=qb),
        grid=(total // n, nj),
        in_specs=[fm(c),
                  pl.BlockSpec((BLK, c), lambda si, j: (prev_blk(si, j), 0)),
                  pl.BlockSpec((qb, c), lambda si, j: (si * nj + j, 0)),
                  fm_prev(vat.shape[0]), fm(vat.shape[0])],
        out_specs=[fm(c), fm(HEADS_PER_DIL)],
        out_shape=[jax.ShapeDtypeStruct((c, total), BF16), jax.ShapeDtypeStruct((HEADS_PER_DIL, total), F32)],
        scratch_shapes=[pltpu.VMEM((2, HEADS_PER_DIL, 2 * BLK, BLK), F32),
                        pltpu.VMEM((2, HEADS_PER_DIL, 1, BLK), F32)],
        compiler_params=_params("parallel", "parallel"),
        name="band_attention",
    )(qt, k, k, vat, vat)


def _merge_kernel(*refs):
    ng = len(DIL_PATTERNS)
    o_refs, l_refs, out_ref = refs[:ng], refs[ng:2 * ng], refs[2 * ng]
    lses = [r[...] for r in l_refs]
    m = functools.reduce(jnp.maximum, lses)
    es = [jnp.exp(l - m) for l in lses]
    den = sum(es)
    ws = [e / den for e in es]
    for h in range(HEADS_PER_DIL):
        hs = slice(h * HEAD_DIM, (h + 1) * HEAD_DIM)
        out_ref[hs, :] = sum(w[h:h + 1] * o[hs, :].astype(F32) for w, o in zip(ws, o_refs)).astype(out_ref.dtype)


def _merge_groups(os_, lses):
    c, t = os_[0].shape
    tm = min(2048, t)
    spec = lambda r: pl.BlockSpec((r, tm), lambda i: (0, i))
    return pl.pallas_call(
        _merge_kernel,
        grid=(t // tm,),
        in_specs=[spec(c)] * len(os_) + [spec(HEADS_PER_DIL)] * len(lses),
        out_specs=spec(c),
        out_shape=jax.ShapeDtypeStruct((c, t), BF16),
        compiler_params=_params("parallel"),
        name="merge_groups",
    )(*os_, *lses)


def _split_residues(a, b, dil):
    f, t = a.shape
    return a.reshape(f, b, t // (b * dil), dil).transpose(0, 1, 3, 2)


def _to_sub_fm(a, b, dil):
    return a if dil == 1 else _split_residues(a, b, dil).reshape(a.shape)


def _to_sub_tok(a, b, dil):
    return a.T if dil == 1 else _split_residues(a, b, dil).transpose(1, 2, 3, 0).reshape(a.shape[1], a.shape[0])


def _from_sub_fm(a, b, dil):
    if dil == 1:
        return a
    f, t = a.shape
    return a.reshape(f, b, dil, t // (b * dil)).transpose(0, 1, 3, 2).reshape(f, t)


def _ffn_kernel(x_ref, mix_ref, mo_ref, wo1_ref, wo2_ref, g_ref, wgu_ref, wd_ref, gf_ref,
                o_ref, act_scr, *, final_norm, tf):
    dff = wd_ref.shape[0]
    x2 = x_ref[...] + _dot_tn(mix_ref[...], wo1_ref[...]) + _dot_tn(mo_ref[...], wo2_ref[...])
    h = _rms(x2, g_ref[...]).astype(BF16)
    for f in range(dff // tf):
        gate = _dot(h, wgu_ref[:, f * tf:(f + 1) * tf])
        up = _dot(h, wgu_ref[:, dff + f * tf:dff + (f + 1) * tf])
        act_scr[:, f * tf:(f + 1) * tf] = (gate * jax.nn.sigmoid(gate) * up).astype(BF16)
    y = x2 + _dot(act_scr[...], wd_ref[...])
    if final_norm:
        y = _rms(y, gf_ref[...])
    o_ref[...] = y


def _out_ffn(x, mix, mo, w_out, g_ffn, w_gate_up, w_down, g_final, final_norm):
    t, d = x.shape
    cm, cmo = mix.shape[0], mo.shape[0]
    dff = w_down.shape[0]
    wo1 = w_out[:cm].astype(BF16)
    wo2 = w_out[cm:].astype(BF16)
    wgu = w_gate_up.astype(BF16)
    wd = w_down.astype(BF16)
    tm = min(512, t)
    tf = 256 if dff % 256 == 0 else dff
    row = lambda c: pl.BlockSpec((tm, c), lambda i: (i, 0))
    const = lambda r, c: pl.BlockSpec((r, c), lambda i: (0, 0), pipeline_mode=pl.Buffered(1))
    return pl.pallas_call(
        functools.partial(_ffn_kernel, final_norm=final_norm, tf=tf),
        grid=(t // tm,),
        in_specs=[row(d), pl.BlockSpec((cm, tm), lambda i: (0, i)), pl.BlockSpec((cmo, tm), lambda i: (0, i)),
                  const(cm, d), const(cmo, d), const(1, d),
                  const(d, 2 * dff), const(dff, d), const(1, d)],
        out_specs=row(d),
        out_shape=jax.ShapeDtypeStruct((t, d), F32),
        scratch_shapes=[pltpu.VMEM((tm, dff), BF16)],
        compiler_params=_params("parallel"),
        name="out_ffn",
    )(x, mix, mo, wo1, wo2, g_ffn.reshape(1, d), wgu, wd, g_final.reshape(1, d))


def kernel(x, mem, positions,
           l0_norm_mix, l0_norm_mem, l0_w_in, l0_w_mem_kv, l0_w_out, l0_norm_ffn, l0_w_gate_up, l0_w_down,
           l1_norm_mix, l1_norm_mem, l1_w_in, l1_w_mem_kv, l1_w_out, l1_norm_ffn, l1_w_gate_up, l1_w_down,
           final_norm):
    b, s, d = x.shape
    t = b * s
    xt = x.reshape(t, d)
    tabs = _trig_tables(positions.reshape(1, t))

    qt, k, vat, qit, ki, wit, qmt, kn = _inproj_a(xt, l0_norm_mix, l0_w_in, tabs)
    mix = _dsa_attention(qt, qit, wit, ki, k, vat, kn, b, s)
    mo = _mem_attention(qmt, *_mem_kv(mem, l0_norm_mem, l0_w_mem_kv), s)
    xt = _out_ffn(xt, mix, mo, l0_w_out, l0_norm_ffn, l0_w_gate_up, l0_w_down, final_norm, False)

    outs = _inproj_b(xt, l1_norm_mix, l1_w_in, tabs)
    os_, lses = [], []
    for g, (window, dil) in enumerate(DIL_PATTERNS):
        assert window // dil == BLK
        qg, kg, vag = outs[3 * g:3 * g + 3]
        o, lse = _band_attention(_to_sub_fm(qg, b, dil), _to_sub_tok(kg, b, dil), _to_sub_fm(vag, b, dil), s // dil)
        os_.append(_from_sub_fm(o, b, dil))
        lses.append(_from_sub_fm(lse, b, dil))
    mix = _merge_groups(os_, lses)
    mo = _mem_attention(outs[-1], *_mem_kv(mem, l1_norm_mem, l1_w_mem_kv), s)
    xt = _out_ffn(xt, mix, mo, l1_w_out, l1_norm_ffn, l1_w_gate_up, l1_w_down, final_norm, True)
    return xt.reshape(b, s, d)
```

```python
import functools

import jax
import jax.numpy as jnp
from jax import lax
from jax.experimental import pallas as pl
from jax.experimental.pallas import tpu as pltpu

F32 = jnp.float32
BF16 = jnp.bfloat16
I32 = jnp.int32

HEAD_DIM = 64
N_HEADS_A = 12
N_KV_A = 4
IDX_HEADS = 8
IDX_DIM = 64
IDX_ROPE_DIM = 32
TOPK_MAX = 256
DIL_PATTERNS = ((128, 1), (512, 4), (2048, 16))
HEADS_PER_DIL = 4
N_MEM_HEADS = 4
BLK = 128
ROPE_THETA = 10000.0
EPS = 1e-6
NEG = -1e30
INT_MIN = -2147483648

LANES = 128
VMEM_LIMIT = 56 * 1024 * 1024

Q_SCALE = HEAD_DIM ** -0.5
WI_SCALE = IDX_HEADS ** -0.5 * IDX_DIM ** -0.5
LOG2E = 1.4426950408889634


def _dot(a, b):
    return jnp.dot(a, b, preferred_element_type=F32)


def _dot_nt(a, b):
    return lax.dot_general(a, b, (((1,), (1,)), ((), ())), preferred_element_type=F32)


def _dot_tn(a, b):
    return lax.dot_general(a, b, (((0,), (0,)), ((), ())), preferred_element_type=F32)


def _params(*sem):
    return pltpu.CompilerParams(dimension_semantics=sem, vmem_limit_bytes=VMEM_LIMIT)


def _rms(x, g):
    ms = jnp.mean(x * x, axis=-1, keepdims=True)
    return x * lax.rsqrt(ms + EPS) * g


H_HD = HEAD_DIM // 2
H_IX = IDX_ROPE_DIM // 2


def _trig_kernel(pos_ref, f_ref, chd_ref, shd_ref, cix_ref, six_ref):
    tm = pos_ref.shape[1]
    pos = pos_ref[...].astype(F32)
    f = jnp.concatenate([f_ref[...]] * (tm // LANES), axis=1)
    ang = f * pos
    c, s = jnp.cos(ang), jnp.sin(ang)
    chd_ref[...] = c[:H_HD]
    shd_ref[...] = s[:H_HD]
    cix_ref[...] = c[H_HD:]
    six_ref[...] = s[H_HD:]


def _trig_tables(pos_row):
    t = pos_row.shape[1]
    tm = min(2048, t)
    f_hd = ROPE_THETA ** (-jnp.arange(H_HD, dtype=F32) / H_HD)
    f_ix = ROPE_THETA ** (-jnp.arange(H_IX, dtype=F32) / H_IX)
    f = jnp.broadcast_to(jnp.concatenate([f_hd, f_ix])[:, None], (H_HD + H_IX, LANES))
    spec = lambda r: pl.BlockSpec((r, tm), lambda i: (0, i))
    rows = [H_HD, H_HD, H_IX, H_IX]
    return pl.pallas_call(
        _trig_kernel,
        grid=(t // tm,),
        in_specs=[spec(1), pl.BlockSpec((H_HD + H_IX, LANES), lambda i: (0, 0))],
        out_specs=[spec(r) for r in rows],
        out_shape=[jax.ShapeDtypeStruct((r, t), F32) for r in rows],
        compiler_params=_params("parallel"),
        name="rope_tables",
    )(pos_row, f)


def _norm_matmul_kernel(x_ref, g_ref, w_ref, o_ref):
    h = _rms(x_ref[...], g_ref[...]).astype(BF16)
    o_ref[...] = _dot(h, w_ref[...]).astype(o_ref.dtype)


def _norm_matmul(x, g, w, out_dtype):
    t, d = x.shape
    n = w.shape[1]
    tm = min(512, t)
    return pl.pallas_call(
        _norm_matmul_kernel,
        grid=(t // tm,),
        in_specs=[pl.BlockSpec((tm, d), lambda i: (i, 0)),
                  pl.BlockSpec((1, d), lambda i: (0, 0)),
                  pl.BlockSpec((d, n), lambda i: (0, 0))],
        out_specs=pl.BlockSpec((tm, n), lambda i: (i, 0)),
        out_shape=jax.ShapeDtypeStruct((t, n), out_dtype),
        compiler_params=_params("parallel"),
        name="norm_matmul",
    )(x, g.reshape(1, d), w)


def _mem_kv(mem, g, w_kv):
    b, m, d = mem.shape
    kv = _norm_matmul(mem.reshape(b * m, d), g, w_kv.astype(BF16), BF16).reshape(b, m, -1)
    nk = N_MEM_HEADS * HEAD_DIM
    return kv[:, :, :nk], kv[:, :, nk:].transpose(0, 2, 1)


VR = 80

A_Q, A_K, A_V, A_QI, A_KI, A_WI, A_QM, A_END = 0, 768, 1024, 1280, 1792, 1856, 1872, 2128


def _rope_heads(p, nheads, out_ref, half, c, s, scale):
    for hh in range(nheads):
        r0 = hh * HEAD_DIM
        x1, x2 = p[r0:r0 + half], p[r0 + half:r0 + 2 * half]
        out_ref[r0:r0 + half, :] = ((x1 * c - x2 * s) * scale).astype(out_ref.dtype)
        out_ref[r0 + half:r0 + 2 * half, :] = ((x2 * c + x1 * s) * scale).astype(out_ref.dtype)
        if 2 * half < HEAD_DIM:
            out_ref[r0 + 2 * half:r0 + HEAD_DIM, :] = (p[r0 + 2 * half:r0 + HEAD_DIM] * scale).astype(out_ref.dtype)


def _write_values(pv, nheads, va_ref):
    tm = pv.shape[1]
    ones_rows = jnp.where(lax.broadcasted_iota(I32, (VR - HEAD_DIM, tm), 0) == 0, 1.0, 0.0).astype(BF16)
    for g in range(nheads):
        va_ref[g * VR:g * VR + HEAD_DIM, :] = pv[g * HEAD_DIM:(g + 1) * HEAD_DIM].astype(BF16)
        va_ref[g * VR + HEAD_DIM:(g + 1) * VR, :] = ones_rows


def _inproj_a_kernel(x_ref, g_ref, wt_ref, chd_ref, shd_ref, cix_ref, six_ref,
                     q_ref, k_ref, va_ref, qi_ref, ki_ref, wi_ref, qm_ref, kn_ref, kt_scr):
    h = _rms(x_ref[...], g_ref[...]).astype(BF16)
    chd, shd = chd_ref[...], shd_ref[...]
    cix, six = cix_ref[...], six_ref[...]

    def proj(a, b):
        return _dot_nt(wt_ref[a:b, :], h)

    _rope_heads(proj(A_Q, A_K), N_HEADS_A, q_ref, H_HD, chd, shd, Q_SCALE * LOG2E)
    nk = N_KV_A * HEAD_DIM
    _rope_heads(proj(A_K, A_V), N_KV_A, kt_scr.at[0:nk], H_HD, chd, shd, 1.0)
    _write_values(proj(A_V, A_QI), N_KV_A, va_ref)
    _rope_heads(proj(A_QI, A_KI), IDX_HEADS, qi_ref, H_IX, cix, six, 1.0)
    pkw = proj(A_KI, A_QM)
    _rope_heads(pkw, 1, kt_scr.at[nk:nk + IDX_DIM], H_IX, cix, six, 1.0)
    k_ref[...] = kt_scr[0:nk, :].T.astype(BF16)
    ki_ref[...] = kt_scr[nk:nk + IDX_DIM, :].T.astype(BF16)
    for g in range(N_KV_A):
        kg = kt_scr[g * HEAD_DIM:(g + 1) * HEAD_DIM, :]
        kn_ref[g:g + 1, :] = jnp.sum(kg * kg, axis=0, keepdims=True)
    wi_ref[...] = pkw[IDX_DIM:IDX_DIM + IDX_HEADS] * WI_SCALE
    qm_ref[...] = (proj(A_QM, A_END) * Q_SCALE).astype(BF16)


def _inproj_a(x, g, w_in, tabs):
    t, d = x.shape
    wt = w_in.T
    pad = jnp.zeros((A_QM - A_WI - IDX_HEADS, d), w_in.dtype)
    split = A_WI + IDX_HEADS
    wt = jnp.concatenate([wt[:split], pad, wt[split:]], axis=0).astype(BF16)
    tm = min(512, t)
    col = lambda r: pl.BlockSpec((r, tm), lambda i: (0, i))
    outs = [(N_HEADS_A * HEAD_DIM, BF16), (N_KV_A * HEAD_DIM, BF16), (N_KV_A * VR, BF16),
            (IDX_HEADS * IDX_DIM, BF16), (IDX_DIM, BF16), (IDX_HEADS, F32), (N_MEM_HEADS * HEAD_DIM, BF16),
            (N_KV_A, F32)]
    return pl.pallas_call(
        _inproj_a_kernel,
        grid=(t // tm,),
        in_specs=[pl.BlockSpec((tm, d), lambda i: (i, 0)), pl.BlockSpec((1, d), lambda i: (0, 0)),
                  pl.BlockSpec((A_END, d), lambda i: (0, 0)),
                  col(H_HD), col(H_HD), col(H_IX), col(H_IX)],
        out_specs=[pl.BlockSpec((tm, r), lambda i: (i, 0)) if k in (1, 4) else col(r)
                   for k, (r, _) in enumerate(outs)],
        out_shape=[jax.ShapeDtypeStruct((t, r) if k in (1, 4) else (r, t), dt) for k, (r, dt) in enumerate(outs)],
        scratch_shapes=[pltpu.VMEM((N_KV_A * HEAD_DIM + IDX_DIM, tm), F32)],
        compiler_params=_params("parallel"),
        name="inproj_a",
    )(x, g.reshape(1, d), wt, *tabs)


def _inproj_b_kernel(x_ref, g_ref, wt_ref, chd_ref, shd_ref, *refs):
    ng = len(DIL_PATTERNS)
    out_refs, (rope_scr, tok_scr) = refs[:3 * ng + 1], refs[3 * ng + 1:]
    h = _rms(x_ref[...], g_ref[...]).astype(BF16)
    chd, shd = chd_ref[...], shd_ref[...]
    gw = HEADS_PER_DIL * HEAD_DIM
    tm = h.shape[0]

    def emit(out_ref, slot, dil, value_t):
        tok = value_t.T
        if dil == 1:
            out_ref[...] = tok.astype(BF16)
        else:
            for cb in range(gw // LANES):
                tok_scr[slot, cb] = tok[:, cb * LANES:(cb + 1) * LANES]
            for r in range(dil):
                for cb in range(gw // LANES):
                    out_ref[:, r * gw + cb * LANES:r * gw + (cb + 1) * LANES] = (
                        tok_scr[slot, cb, pl.ds(r, tm // dil, stride=dil), :].astype(BF16))

    for g, (_, dil) in enumerate(DIL_PATTERNS):
        q_ref, k_ref, v_ref = out_refs[3 * g:3 * g + 3]
        base = 3 * g * gw
        _rope_heads(_dot_nt(wt_ref[base:base + gw, :], h), HEADS_PER_DIL, rope_scr.at[0], H_HD, chd, shd,
                    Q_SCALE * LOG2E)
        emit(q_ref, 0, dil, rope_scr[0])
        _rope_heads(_dot_nt(wt_ref[base + gw:base + 2 * gw, :], h), HEADS_PER_DIL, rope_scr.at[1], H_HD, chd, shd, 1.0)
        emit(k_ref, 1, dil, rope_scr[1])
        emit(v_ref, 2, dil, _dot_nt(wt_ref[base + 2 * gw:base + 3 * gw, :], h))
    out_refs[3 * ng][...] = (_dot_nt(wt_ref[3 * ng * gw:3 * ng * gw + N_MEM_HEADS * HEAD_DIM, :], h)
                            * Q_SCALE).astype(BF16)


def _inproj_b(x, g, w_in, tabs):
    t, d = x.shape
    wt = w_in.T.astype(BF16)
    tm = min(512, t)
    col = lambda r: pl.BlockSpec((r, tm), lambda i: (0, i))
    gw = HEADS_PER_DIL * HEAD_DIM
    sub_specs, sub_shapes = [], []
    for _, dil in DIL_PATTERNS:
        assert tm % (16 * dil) == 0
        sub_specs += [pl.BlockSpec((tm // dil, dil * gw), lambda i: (i, 0))] * 3
        sub_shapes += [jax.ShapeDtypeStruct((t // dil, dil * gw), BF16)] * 3
    return pl.pallas_call(
        _inproj_b_kernel,
        grid=(t // tm,),
        in_specs=[pl.BlockSpec((tm, d), lambda i: (i, 0)), pl.BlockSpec((1, d), lambda i: (0, 0)),
                  pl.BlockSpec(wt.shape, lambda i: (0, 0)), col(H_HD), col(H_HD)],
        out_specs=sub_specs + [col(N_MEM_HEADS * HEAD_DIM)],
        out_shape=sub_shapes + [jax.ShapeDtypeStruct((N_MEM_HEADS * HEAD_DIM, t), BF16)],
        scratch_shapes=[pltpu.VMEM((2, gw, tm), F32), pltpu.VMEM((3, gw // LANES, tm, LANES), F32)],
        compiler_params=_params("parallel"),
        name="inproj_b",
    )(x, g.reshape(1, d), wt, tabs[0], tabs[1])


TQ = 256
CR = 32
SCORE_BITS = 32
BOUND_SLACK = 1.01
UNDERFLOW_GUARD = 2.0 ** -100


def _key_to_f32(key):
    bits = jnp.where(key < 0, key ^ jnp.int32(0x7FFFFFFF), key)
    return pltpu.bitcast(bits, F32)


def _dsa_kernel(q_ref, qi_ref, wi_ref, ki_ref, k_ref, va_ref, kn_ref, o_ref,
                sc_scr, tau_scr, need_scr, tie_scr, tri_scr, bound_scr, m_scr, alpha_scr, acc_scr, s0_scr, s1_scr, p0_scr, p1_scr,
                *, seq, n_sel):
    i = pl.program_id(1)
    nch = i + 1
    krow = lax.broadcasted_iota(I32, (TQ, TQ), 0)
    qcol = lax.broadcasted_iota(I32, (TQ, TQ), 1)

    def chunk_off(c):
        return pl.multiple_of(c * TQ, TQ)

    def score_chunk(c, diag):
        off = chunk_off(c)
        kic = ki_ref[pl.ds(off, TQ), :]
        sc = jnp.zeros((TQ, TQ), F32)
        for h in range(IDX_HEADS):
            lg = _dot(kic, qi_ref[h * IDX_DIM:(h + 1) * IDX_DIM, :])
            sc = sc + jnp.maximum(lg, 0.0) * wi_ref[h:h + 1, :]
        if diag:
            sc = jnp.where(krow > qcol, -jnp.inf, sc)
        sc_scr[pl.ds(off, TQ), :] = sc

    def score_body(c, carry):
        score_chunk(c, False)
        return carry

    lax.fori_loop(0, i, score_body, 0)
    score_chunk(i, True)

    def count(pred):
        def body(c, acc):
            off = chunk_off(c)
            ind = pred(sc_scr[pl.ds(off, TQ), :], off)
            return acc + jnp.sum(ind.reshape(TQ // CR, CR, TQ), axis=0)
        acc = lax.fori_loop(0, nch, body, jnp.zeros((CR, TQ), F32))
        return jnp.sum(acc, axis=0, keepdims=True)

    @pl.when(i * TQ < n_sel)
    def _():
        tau_scr[...] = jnp.full((1, TQ), -jnp.inf, F32)
        need_scr[...] = jnp.zeros((1, TQ), F32)

    @pl.when(i * TQ >= n_sel)
    def _():
        def body(step, carry):
            tau, cge, crej = carry
            cand = tau + jnp.left_shift(jnp.int32(1), SCORE_BITS - 1 - step)
            cand_f = _key_to_f32(cand)
            cnt = count(lambda blk, _: jnp.where(blk >= cand_f, 1.0, 0.0))
            ok = cnt >= float(n_sel)
            return jnp.where(ok, cand, tau), jnp.where(ok, cnt, cge), jnp.where(ok, crej, cnt)

        init = (jnp.full((1, TQ), INT_MIN, I32), jnp.full((1, TQ), 2.0 * n_sel, F32), jnp.zeros((1, TQ), F32))
        tau, cge, crej = lax.fori_loop(0, SCORE_BITS, body, init)
        tau_scr[...] = _key_to_f32(tau)
        need_scr[...] = jnp.where(cge > float(n_sel), float(n_sel) - crej, 2.0 * seq)

    tri_scr[...] = jnp.where(krow >= qcol, 1.0, 0.0).astype(BF16)
    grp = N_HEADS_A // N_KV_A
    tau_f = tau_scr[...]
    need = need_scr[...]

    def chunk_select(c):
        blk = sc_scr[pl.ds(chunk_off(c), TQ), :]
        eq = blk == tau_f
        rank = _dot(tri_scr[...], jnp.where(eq, 1.0, 0.0).astype(BF16)) + tie_scr[...]
        tie_scr[...] = rank[TQ - 1:TQ, :]
        return jnp.where(eq, jnp.where(rank <= need, 1.0, 0.0), jnp.where(blk > tau_f, 1.0, 0.0))

    kmax2 = jnp.max(kn_ref[...], axis=1, keepdims=True)
    for h in range(N_HEADS_A):
        qf = q_ref[h * HEAD_DIM:(h + 1) * HEAD_DIM, :].astype(F32)
        qn2 = jnp.sum(qf * qf, axis=0, keepdims=True)
        bound_scr[h] = jnp.sqrt(qn2 * kmax2[h // grp:h // grp + 1]) * BOUND_SLACK
    acc_scr[...] = jnp.zeros(acc_scr.shape, F32)
    tie_scr[...] = jnp.zeros((1, TQ), F32)

    def pipeline(stage_a, stage_b, buf0, buf1):
        def step(c, src, dst):
            stage_b(c - 1, src)
            stage_a(c, dst)

        stage_a(0, buf0)

        def pair_body(t, carry):
            step(2 * t + 1, buf0, buf1)
            step(2 * t + 2, buf1, buf0)
            return carry

        lax.fori_loop(0, (nch - 1) // 2, pair_body, 0)

        @pl.when((nch - 1) % 2 == 1)
        def _():
            step(nch - 1, buf0, buf1)
            stage_b(nch - 1, buf1)

        @pl.when((nch - 1) % 2 == 0)
        def _():
            stage_b(nch - 1, buf0)

    def fast_a(c, p_dst):
        off = chunk_off(c)
        sel = chunk_select(c).astype(BF16)
        for g in range(N_KV_A):
            kc = k_ref[pl.ds(off, TQ), g * HEAD_DIM:(g + 1) * HEAD_DIM]
            for j in range(grp):
                h = g * grp + j
                s = _dot(kc, q_ref[h * HEAD_DIM:(h + 1) * HEAD_DIM, :])
                p_dst[h] = jnp.exp2(s - bound_scr[h]).astype(BF16) * sel

    def fast_b(c, p_src):
        off = chunk_off(c)
        for g in range(N_KV_A):
            vt = va_ref[g * VR:(g + 1) * VR, pl.ds(off, TQ)]
            for j in range(grp):
                h = g * grp + j
                acc_scr[h] += _dot(vt, p_src[h])

    pipeline(fast_a, fast_b, p0_scr, p1_scr)
    lmin = functools.reduce(jnp.minimum, [acc_scr[h, HEAD_DIM:HEAD_DIM + 1, :] for h in range(N_HEADS_A)])

    def stage_a(c, s_dst):
        off = chunk_off(c)
        bias = (chunk_select(c) - 1.0) * (-NEG)
        for g in range(N_KV_A):
            kc = k_ref[pl.ds(off, TQ), g * HEAD_DIM:(g + 1) * HEAD_DIM]
            for j in range(grp):
                h = g * grp + j
                s = _dot(kc, q_ref[h * HEAD_DIM:(h + 1) * HEAD_DIM, :]) + bias
                s_dst[h] = s
                m_old = m_scr[h]
                m_new = jnp.maximum(m_old, jnp.max(s, axis=0, keepdims=True))
                alpha_scr[h] = jnp.exp2(m_old - m_new)
                m_scr[h] = m_new

    def stage_b(c, s_src):
        off = chunk_off(c)
        for g in range(N_KV_A):
            vt = va_ref[g * VR:(g + 1) * VR, pl.ds(off, TQ)]
            for j in range(grp):
                h = g * grp + j
                p = jnp.exp2(s_src[h] - m_scr[h]).astype(BF16)
                acc_scr[h] = alpha_scr[h] * acc_scr[h] + _dot(vt, p)

    @pl.when(jnp.logical_not(jnp.min(lmin) > UNDERFLOW_GUARD))
    def _():
        m_scr[...] = jnp.full(m_scr.shape, NEG, F32)
        acc_scr[...] = jnp.zeros(acc_scr.shape, F32)
        tie_scr[...] = jnp.zeros((1, TQ), F32)
        pipeline(stage_a, stage_b, s0_scr, s1_scr)

    for h in range(N_HEADS_A):
        a = acc_scr[h]
        o_ref[h * HEAD_DIM:(h + 1) * HEAD_DIM, :] = (a[:HEAD_DIM] / a[HEAD_DIM:HEAD_DIM + 1]).astype(BF16)


def _dsa_attention(qt, qit, wit, ki, k, vat, kn, b, s):
    n_sel = min(TOPK_MAX, s // 4)
    assert s % TQ == 0 and n_sel % TQ == 0
    nq = s // TQ
    qblk = lambda r: pl.BlockSpec((r, TQ), lambda bi, i: (0, bi * nq + i))
    tok = lambda c: pl.BlockSpec((s, c), lambda bi, i: (bi, 0))
    return pl.pallas_call(
        functools.partial(_dsa_kernel, seq=s, n_sel=n_sel),
        grid=(b, nq),
        in_specs=[qblk(qt.shape[0]), qblk(qit.shape[0]), qblk(wit.shape[0]),
                  tok(ki.shape[1]), tok(k.shape[1]),
                  pl.BlockSpec((vat.shape[0], s), lambda bi, i: (0, bi)),
                  pl.BlockSpec((kn.shape[0], s), lambda bi, i: (0, bi))],
        out_specs=qblk(qt.shape[0]),
        out_shape=jax.ShapeDtypeStruct(qt.shape, BF16),
        scratch_shapes=[
            pltpu.VMEM((s, TQ), F32),
            pltpu.VMEM((1, TQ), F32),
            pltpu.VMEM((1, TQ), F32),
            pltpu.VMEM((1, TQ), F32),
            pltpu.VMEM((TQ, TQ), BF16),
            pltpu.VMEM((N_HEADS_A, 1, TQ), F32),
            pltpu.VMEM((N_HEADS_A, 1, TQ), F32),
            pltpu.VMEM((N_HEADS_A, 1, TQ), F32),
            pltpu.VMEM((N_HEADS_A, VR, TQ), F32),
            pltpu.VMEM((N_HEADS_A, TQ, TQ), F32),
            pltpu.VMEM((N_HEADS_A, TQ, TQ), F32),
            pltpu.VMEM((N_HEADS_A, TQ, TQ), BF16),
            pltpu.VMEM((N_HEADS_A, TQ, TQ), BF16),
        ],
        compiler_params=_params("parallel", "arbitrary"),
        name="dsa_attention",
    )(qt, qit, wit, ki, k, vat, kn)


def _mem_attn_kernel(q_ref, k_ref, v_ref, o_ref):
    for h in range(N_MEM_HEADS):
        sl = slice(h * HEAD_DIM, (h + 1) * HEAD_DIM)
        s = _dot(k_ref[:, sl], q_ref[sl, :])
        p = jnp.exp(s - jnp.max(s, axis=0, keepdims=True))
        l = jnp.sum(p, axis=0, keepdims=True)
        o_ref[sl, :] = (_dot(v_ref[sl, :], p.astype(BF16)) / l).astype(BF16)


def _mem_attention(qmt, km, vmt, s):
    c, t = qmt.shape
    m = km.shape[1]
    tm = min(512, s)
    nq = s // tm
    blk = pl.BlockSpec((c, tm), lambda bi, i: (0, bi * nq + i))
    return pl.pallas_call(
        _mem_attn_kernel,
        grid=(t // s, nq),
        in_specs=[blk,
                  pl.BlockSpec((None, m, km.shape[2]), lambda bi, i: (bi, 0, 0)),
                  pl.BlockSpec((None, vmt.shape[1], m), lambda bi, i: (bi, 0, 0))],
        out_specs=blk,
        out_shape=jax.ShapeDtypeStruct((c, t), BF16),
        compiler_params=_params("parallel", "parallel"),
        name="mem_attention",
    )(qmt, km, vmt)


QB = 512
LN2 = 0.6931471805599453


def _band_kernel(q_ref, kp_ref, kc_ref, vp_ref, vc_ref, o_ref, lse_ref, s_scr, m_scr, *, qb):
    j = pl.program_id(1)
    nsub = qb // BLK
    krow = lax.broadcasted_iota(I32, (BLK, BLK), 0)
    qcol = lax.broadcasted_iota(I32, (BLK, BLK), 1)
    bias_prev = jnp.where(krow >= qcol, 0.0, NEG)
    bias_cur = jnp.where(krow <= qcol, 0.0, NEG)
    no_prev = jnp.where(j > 0, 0.0, NEG)

    def stage_a(sb):
        qs = slice(sb * BLK, (sb + 1) * BLK)
        for h in range(HEADS_PER_DIL):
            hs = slice(h * HEAD_DIM, (h + 1) * HEAD_DIM)
            qh = q_ref[qs, hs]
            if sb == 0:
                s_p = _dot_nt(kp_ref[:, hs], qh) + (bias_prev + no_prev)
            else:
                s_p = _dot_nt(kc_ref[(sb - 1) * BLK:sb * BLK, hs], qh) + bias_prev
            s_c = _dot_nt(kc_ref[qs, hs], qh) + bias_cur
            s_scr[sb % 2, h, 0:BLK] = s_p
            s_scr[sb % 2, h, BLK:2 * BLK] = s_c
            m_scr[sb % 2, h] = jnp.maximum(jnp.max(s_p, axis=0, keepdims=True),
                                           jnp.max(s_c, axis=0, keepdims=True))

    def stage_b(sb):
        qs = slice(sb * BLK, (sb + 1) * BLK)
        for h in range(HEADS_PER_DIL):
            hs = slice(h * HEAD_DIM, (h + 1) * HEAD_DIM)
            m = m_scr[sb % 2, h]
            p_p = jnp.exp2(s_scr[sb % 2, h, 0:BLK] - m)
            p_c = jnp.exp2(s_scr[sb % 2, h, BLK:2 * BLK] - m)
            l = jnp.sum(p_p, axis=0, keepdims=True) + jnp.sum(p_c, axis=0, keepdims=True)
            v_p = vp_ref[:, hs] if sb == 0 else vc_ref[(sb - 1) * BLK:sb * BLK, hs]
            acc = _dot_tn(v_p, p_p.astype(BF16)) + _dot_tn(vc_ref[qs, hs], p_c.astype(BF16))
            o_ref[hs, qs] = (acc / l).astype(o_ref.dtype)
            lse_ref[h:h + 1, qs] = m * LN2 + jnp.log(l)

    stage_a(0)
    for sb in range(1, nsub):
        stage_b(sb - 1)
        stage_a(sb)
    stage_b(nsub - 1)


def _band_attention(q, k, v, b, dil):
    rows, width = q.shape
    c = width // dil
    t = rows * dil
    n = t // (b * dil)
    qb = min(QB, n)
    assert n % qb == 0 and qb % BLK == 0
    rr, nj = qb // BLK, n // qb
    cur = pl.BlockSpec((qb, c), lambda si, j: ((si // dil) * nj + j, si % dil))
    prev = pl.BlockSpec((BLK, c), lambda si, j: ((si // dil) * (n // BLK) + jnp.maximum(j * rr - 1, 0), si % dil))
    fm = lambda r: pl.BlockSpec((r, qb), lambda si, j: (0, si * nj + j))
    return pl.pallas_call(
        functools.partial(_band_kernel, qb=qb),
        grid=(t // n, nj),
        in_specs=[cur, prev, cur, prev, cur],
        out_specs=[fm(c), fm(HEADS_PER_DIL)],
        out_shape=[jax.ShapeDtypeStruct((c, t), BF16), jax.ShapeDtypeStruct((HEADS_PER_DIL, t), F32)],
        scratch_shapes=[pltpu.VMEM((2, HEADS_PER_DIL, 2 * BLK, BLK), F32),
                        pltpu.VMEM((2, HEADS_PER_DIL, 1, BLK), F32)],
        compiler_params=_params("parallel", "parallel"),
        name="band_attention",
    )(q, k, k, v, v)


def _merge_kernel(*refs):
    ng = len(DIL_PATTERNS)
    o_refs, l_refs, out_ref = refs[:ng], refs[ng:2 * ng], refs[2 * ng]
    lses = [r[...] for r in l_refs]
    m = functools.reduce(jnp.maximum, lses)
    es = [jnp.exp(l - m) for l in lses]
    den = sum(es)
    ws = [e / den for e in es]
    for h in range(HEADS_PER_DIL):
        hs = slice(h * HEAD_DIM, (h + 1) * HEAD_DIM)
        out_ref[hs, :] = sum(w[h:h + 1] * o[hs, :].astype(F32) for w, o in zip(ws, o_refs)).astype(out_ref.dtype)


def _merge_groups(os_, lses):
    c, t = os_[0].shape
    tm = min(2048, t)
    spec = lambda r: pl.BlockSpec((r, tm), lambda i: (0, i))
    return pl.pallas_call(
        _merge_kernel,
        grid=(t // tm,),
        in_specs=[spec(c)] * len(os_) + [spec(HEADS_PER_DIL)] * len(lses),
        out_specs=spec(c),
        out_shape=jax.ShapeDtypeStruct((c, t), BF16),
        compiler_params=_params("parallel"),
        name="merge_groups",
    )(*os_, *lses)


def _split_residues(a, b, dil):
    f, t = a.shape
    return a.reshape(f, b, t // (b * dil), dil).transpose(0, 1, 3, 2)


def _to_sub_fm(a, b, dil):
    return a if dil == 1 else _split_residues(a, b, dil).reshape(a.shape)


def _to_sub_tok(a, b, dil):
    return a.T if dil == 1 else _split_residues(a, b, dil).transpose(1, 2, 3, 0).reshape(a.shape[1], a.shape[0])


def _from_sub_fm(a, b, dil):
    if dil == 1:
        return a
    f, t = a.shape
    return a.reshape(f, b, dil, t // (b * dil)).transpose(0, 1, 3, 2).reshape(f, t)


def _ffn_kernel(x_ref, mix_ref, mo_ref, wo1_ref, wo2_ref, g_ref, wgu_ref, wd_ref, gf_ref,
                o_ref, act_scr, *, final_norm, tf):
    dff = wd_ref.shape[0]
    x2 = x_ref[...] + _dot_tn(mix_ref[...], wo1_ref[...]) + _dot_tn(mo_ref[...], wo2_ref[...])
    h = _rms(x2, g_ref[...]).astype(BF16)
    for f in range(dff // tf):
        gate = _dot(h, wgu_ref[:, f * tf:(f + 1) * tf])
        up = _dot(h, wgu_ref[:, dff + f * tf:dff + (f + 1) * tf])
        act_scr[:, f * tf:(f + 1) * tf] = (gate * jax.nn.sigmoid(gate) * up).astype(BF16)
    y = x2 + _dot(act_scr[...], wd_ref[...])
    if final_norm:
        y = _rms(y, gf_ref[...])
    o_ref[...] = y


def _out_ffn(x, mix, mo, w_out, g_ffn, w_gate_up, w_down, g_final, final_norm):
    t, d = x.shape
    cm, cmo = mix.shape[0], mo.shape[0]
    dff = w_down.shape[0]
    wo1 = w_out[:cm].astype(BF16)
    wo2 = w_out[cm:].astype(BF16)
    wgu = w_gate_up.astype(BF16)
    wd = w_down.astype(BF16)
    tm = min(512, t)
    tf = 256 if dff % 256 == 0 else dff
    row = lambda c: pl.BlockSpec((tm, c), lambda i: (i, 0))
    const = lambda r, c: pl.BlockSpec((r, c), lambda i: (0, 0), pipeline_mode=pl.Buffered(1))
    return pl.pallas_call(
        functools.partial(_ffn_kernel, final_norm=final_norm, tf=tf),
        grid=(t // tm,),
        in_specs=[row(d), pl.BlockSpec((cm, tm), lambda i: (0, i)), pl.BlockSpec((cmo, tm), lambda i: (0, i)),
                  const(cm, d), const(cmo, d), const(1, d),
                  const(d, 2 * dff), const(dff, d), const(1, d)],
        out_specs=row(d),
        out_shape=jax.ShapeDtypeStruct((t, d), F32),
        scratch_shapes=[pltpu.VMEM((tm, dff), BF16)],
        compiler_params=_params("parallel"),
        name="out_ffn",
    )(x, mix, mo, wo1, wo2, g_ffn.reshape(1, d), wgu, wd, g_final.reshape(1, d))


def kernel(x, mem, positions,
           l0_norm_mix, l0_norm_mem, l0_w_in, l0_w_mem_kv, l0_w_out, l0_norm_ffn, l0_w_gate_up, l0_w_down,
           l1_norm_mix, l1_norm_mem, l1_w_in, l1_w_mem_kv, l1_w_out, l1_norm_ffn, l1_w_gate_up, l1_w_down,
           final_norm):
    b, s, d = x.shape
    t = b * s
    xt = x.reshape(t, d)
    tabs = _trig_tables(positions.reshape(1, t))

    qt, k, vat, qit, ki, wit, qmt, kn = _inproj_a(xt, l0_norm_mix, l0_w_in, tabs)
    mix = _dsa_attention(qt, qit, wit, ki, k, vat, kn, b, s)
    mo = _mem_attention(qmt, *_mem_kv(mem, l0_norm_mem, l0_w_mem_kv), s)
    xt = _out_ffn(xt, mix, mo, l0_w_out, l0_norm_ffn, l0_w_gate_up, l0_w_down, final_norm, False)

    outs = _inproj_b(xt, l1_norm_mix, l1_w_in, tabs)
    os_, lses = [], []
    for g, (window, dil) in enumerate(DIL_PATTERNS):
        assert window // dil == BLK
        o, lse = _band_attention(*outs[3 * g:3 * g + 3], b, dil)
        os_.append(_from_sub_fm(o, b, dil))
        lses.append(_from_sub_fm(lse, b, dil))
    mix = _merge_groups(os_, lses)
    mo = _mem_attention(outs[-1], *_mem_kv(mem, l1_norm_mem, l1_w_mem_kv), s)
    xt = _out_ffn(xt, mix, mo, l1_w_out, l1_norm_ffn, l1_w_gate_up, l1_w_down, final_norm, True)
    return xt.reshape(b, s, d)
```

```python
import functools

import jax
import jax.numpy as jnp
from jax import lax
from jax.experimental import pallas as pl
from jax.experimental.pallas import tpu as pltpu

F32 = jnp.float32
BF16 = jnp.bfloat16
I32 = jnp.int32

HEAD_DIM = 64
N_HEADS_A = 12
N_KV_A = 4
IDX_HEADS = 8
IDX_DIM = 64
IDX_ROPE_DIM = 32
TOPK_MAX = 256
DIL_PATTERNS = ((128, 1), (512, 4), (2048, 16))
HEADS_PER_DIL = 4
N_MEM_HEADS = 4
BLK = 128
ROPE_THETA = 10000.0
EPS = 1e-6
NEG = -1e30
INT_MIN = -2147483648

LANES = 128
VMEM_LIMIT = 56 * 1024 * 1024

Q_SCALE = HEAD_DIM ** -0.5
WI_SCALE = IDX_HEADS ** -0.5 * IDX_DIM ** -0.5
LOG2E = 1.4426950408889634


def _dot(a, b):
    return jnp.dot(a, b, preferred_element_type=F32)


def _dot_nt(a, b):
    return lax.dot_general(a, b, (((1,), (1,)), ((), ())), preferred_element_type=F32)


def _dot_tn(a, b):
    return lax.dot_general(a, b, (((0,), (0,)), ((), ())), preferred_element_type=F32)


def _params(*sem):
    return pltpu.CompilerParams(dimension_semantics=sem, vmem_limit_bytes=VMEM_LIMIT)


def _rms(x, g):
    ms = jnp.mean(x * x, axis=-1, keepdims=True)
    return x * lax.rsqrt(ms + EPS) * g


H_HD = HEAD_DIM // 2
H_IX = IDX_ROPE_DIM // 2


def _trig_kernel(pos_ref, f_ref, chd_ref, shd_ref, cix_ref, six_ref):
    tm = pos_ref.shape[1]
    pos = pos_ref[...].astype(F32)
    f = jnp.concatenate([f_ref[...]] * (tm // LANES), axis=1)
    ang = f * pos
    c, s = jnp.cos(ang), jnp.sin(ang)
    chd_ref[...] = c[:H_HD]
    shd_ref[...] = s[:H_HD]
    cix_ref[...] = c[H_HD:]
    six_ref[...] = s[H_HD:]


def _trig_tables(pos_row):
    t = pos_row.shape[1]
    tm = min(2048, t)
    f_hd = ROPE_THETA ** (-jnp.arange(H_HD, dtype=F32) / H_HD)
    f_ix = ROPE_THETA ** (-jnp.arange(H_IX, dtype=F32) / H_IX)
    f = jnp.broadcast_to(jnp.concatenate([f_hd, f_ix])[:, None], (H_HD + H_IX, LANES))
    spec = lambda r: pl.BlockSpec((r, tm), lambda i: (0, i))
    rows = [H_HD, H_HD, H_IX, H_IX]
    return pl.pallas_call(
        _trig_kernel,
        grid=(t // tm,),
        in_specs=[spec(1), pl.BlockSpec((H_HD + H_IX, LANES), lambda i: (0, 0))],
        out_specs=[spec(r) for r in rows],
        out_shape=[jax.ShapeDtypeStruct((r, t), F32) for r in rows],
        compiler_params=_params("parallel"),
        name="rope_tables",
    )(pos_row, f)


def _norm_matmul_kernel(x_ref, g_ref, w_ref, o_ref):
    h = _rms(x_ref[...], g_ref[...]).astype(BF16)
    o_ref[...] = _dot(h, w_ref[...]).astype(o_ref.dtype)


def _norm_matmul(x, g, w, out_dtype):
    t, d = x.shape
    n = w.shape[1]
    tm = min(512, t)
    return pl.pallas_call(
        _norm_matmul_kernel,
        grid=(t // tm,),
        in_specs=[pl.BlockSpec((tm, d), lambda i: (i, 0)),
                  pl.BlockSpec((1, d), lambda i: (0, 0)),
                  pl.BlockSpec((d, n), lambda i: (0, 0))],
        out_specs=pl.BlockSpec((tm, n), lambda i: (i, 0)),
        out_shape=jax.ShapeDtypeStruct((t, n), out_dtype),
        compiler_params=_params("parallel"),
        name="norm_matmul",
    )(x, g.reshape(1, d), w)


def _mem_kv(mem, g, w_kv):
    b, m, d = mem.shape
    kv = _norm_matmul(mem.reshape(b * m, d), g, w_kv.astype(BF16), BF16).reshape(b, m, -1)
    nk = N_MEM_HEADS * HEAD_DIM
    return kv[:, :, :nk], kv[:, :, nk:].transpose(0, 2, 1)


VR = 80

A_Q, A_K, A_V, A_QI, A_KI, A_WI, A_QM, A_END = 0, 768, 1024, 1280, 1792, 1856, 1872, 2128


def _rope_heads(p, nheads, out_ref, half, c, s, scale):
    for hh in range(nheads):
        r0 = hh * HEAD_DIM
        x1, x2 = p[r0:r0 + half], p[r0 + half:r0 + 2 * half]
        out_ref[r0:r0 + half, :] = ((x1 * c - x2 * s) * scale).astype(out_ref.dtype)
        out_ref[r0 + half:r0 + 2 * half, :] = ((x2 * c + x1 * s) * scale).astype(out_ref.dtype)
        if 2 * half < HEAD_DIM:
            out_ref[r0 + 2 * half:r0 + HEAD_DIM, :] = (p[r0 + 2 * half:r0 + HEAD_DIM] * scale).astype(out_ref.dtype)


def _write_values(pv, nheads, va_ref):
    tm = pv.shape[1]
    ones_rows = jnp.where(lax.broadcasted_iota(I32, (VR - HEAD_DIM, tm), 0) == 0, 1.0, 0.0).astype(BF16)
    for g in range(nheads):
        va_ref[g * VR:g * VR + HEAD_DIM, :] = pv[g * HEAD_DIM:(g + 1) * HEAD_DIM].astype(BF16)
        va_ref[g * VR + HEAD_DIM:(g + 1) * VR, :] = ones_rows


def _inproj_a_kernel(x_ref, g_ref, wt_ref, chd_ref, shd_ref, cix_ref, six_ref,
                     q_ref, k_ref, va_ref, qi_ref, ki_ref, wi_ref, qm_ref, kn_ref, kt_scr):
    h = _rms(x_ref[...], g_ref[...]).astype(BF16)
    chd, shd = chd_ref[...], shd_ref[...]
    cix, six = cix_ref[...], six_ref[...]

    def proj(a, b):
        return _dot_nt(wt_ref[a:b, :], h)

    _rope_heads(proj(A_Q, A_K), N_HEADS_A, q_ref, H_HD, chd, shd, Q_SCALE * LOG2E)
    nk = N_KV_A * HEAD_DIM
    _rope_heads(proj(A_K, A_V), N_KV_A, kt_scr.at[0:nk], H_HD, chd, shd, 1.0)
    _write_values(proj(A_V, A_QI), N_KV_A, va_ref)
    _rope_heads(proj(A_QI, A_KI), IDX_HEADS, qi_ref, H_IX, cix, six, 1.0)
    pkw = proj(A_KI, A_QM)
    _rope_heads(pkw, 1, kt_scr.at[nk:nk + IDX_DIM], H_IX, cix, six, 1.0)
    k_ref[...] = kt_scr[0:nk, :].T.astype(BF16)
    ki_ref[...] = kt_scr[nk:nk + IDX_DIM, :].T.astype(BF16)
    for g in range(N_KV_A):
        kg = kt_scr[g * HEAD_DIM:(g + 1) * HEAD_DIM, :]
        kn_ref[g:g + 1, :] = jnp.sum(kg * kg, axis=0, keepdims=True)
    wi_ref[...] = pkw[IDX_DIM:IDX_DIM + IDX_HEADS] * WI_SCALE
    qm_ref[...] = (proj(A_QM, A_END) * Q_SCALE).astype(BF16)


def _inproj_a(x, g, w_in, tabs):
    t, d = x.shape
    wt = w_in.T
    pad = jnp.zeros((A_QM - A_WI - IDX_HEADS, d), w_in.dtype)
    split = A_WI + IDX_HEADS
    wt = jnp.concatenate([wt[:split], pad, wt[split:]], axis=0).astype(BF16)
    tm = min(512, t)
    col = lambda r: pl.BlockSpec((r, tm), lambda i: (0, i))
    outs = [(N_HEADS_A * HEAD_DIM, BF16), (N_KV_A * HEAD_DIM, BF16), (N_KV_A * VR, BF16),
            (IDX_HEADS * IDX_DIM, BF16), (IDX_DIM, BF16), (IDX_HEADS, F32), (N_MEM_HEADS * HEAD_DIM, BF16),
            (N_KV_A, F32)]
    return pl.pallas_call(
        _inproj_a_kernel,
        grid=(t // tm,),
        in_specs=[pl.BlockSpec((tm, d), lambda i: (i, 0)), pl.BlockSpec((1, d), lambda i: (0, 0)),
                  pl.BlockSpec((A_END, d), lambda i: (0, 0)),
                  col(H_HD), col(H_HD), col(H_IX), col(H_IX)],
        out_specs=[pl.BlockSpec((tm, r), lambda i: (i, 0)) if k in (1, 4) else col(r)
                   for k, (r, _) in enumerate(outs)],
        out_shape=[jax.ShapeDtypeStruct((t, r) if k in (1, 4) else (r, t), dt) for k, (r, dt) in enumerate(outs)],
        scratch_shapes=[pltpu.VMEM((N_KV_A * HEAD_DIM + IDX_DIM, tm), F32)],
        compiler_params=_params("parallel"),
        name="inproj_a",
    )(x, g.reshape(1, d), wt, *tabs)


def _inproj_b_kernel(x_ref, g_ref, wt_ref, chd_ref, shd_ref, *refs):
    ng = len(DIL_PATTERNS)
    out_refs, (rope_scr, tok_scr) = refs[:3 * ng + 1], refs[3 * ng + 1:]
    h = _rms(x_ref[...], g_ref[...]).astype(BF16)
    chd, shd = chd_ref[...], shd_ref[...]
    gw = HEADS_PER_DIL * HEAD_DIM
    tm = h.shape[0]

    def emit(out_ref, slot, dil, value_t):
        tok = value_t.T
        if dil == 1:
            out_ref[...] = tok.astype(BF16)
        else:
            for cb in range(gw // LANES):
                tok_scr[slot, cb] = tok[:, cb * LANES:(cb + 1) * LANES]
            for r in range(dil):
                for cb in range(gw // LANES):
                    out_ref[:, r * gw + cb * LANES:r * gw + (cb + 1) * LANES] = (
                        tok_scr[slot, cb, pl.ds(r, tm // dil, stride=dil), :].astype(BF16))

    for g, (_, dil) in enumerate(DIL_PATTERNS):
        q_ref, k_ref, v_ref = out_refs[3 * g:3 * g + 3]
        base = 3 * g * gw
        _rope_heads(_dot_nt(wt_ref[base:base + gw, :], h), HEADS_PER_DIL, rope_scr.at[0], H_HD, chd, shd,
                    Q_SCALE * LOG2E)
        emit(q_ref, 0, dil, rope_scr[0])
        _rope_heads(_dot_nt(wt_ref[base + gw:base + 2 * gw, :], h), HEADS_PER_DIL, rope_scr.at[1], H_HD, chd, shd, 1.0)
        emit(k_ref, 1, dil, rope_scr[1])
        emit(v_ref, 2, dil, _dot_nt(wt_ref[base + 2 * gw:base + 3 * gw, :], h))
    out_refs[3 * ng][...] = (_dot_nt(wt_ref[3 * ng * gw:3 * ng * gw + N_MEM_HEADS * HEAD_DIM, :], h)
                            * Q_SCALE).astype(BF16)


def _inproj_b(x, g, w_in, tabs):
    t, d = x.shape
    wt = w_in.T.astype(BF16)
    tm = min(512, t)
    col = lambda r: pl.BlockSpec((r, tm), lambda i: (0, i))
    gw = HEADS_PER_DIL * HEAD_DIM
    sub_specs, sub_shapes = [], []
    for _, dil in DIL_PATTERNS:
        assert tm % (16 * dil) == 0
        sub_specs += [pl.BlockSpec((tm // dil, dil * gw), lambda i: (i, 0))] * 3
        sub_shapes += [jax.ShapeDtypeStruct((t // dil, dil * gw), BF16)] * 3
    return pl.pallas_call(
        _inproj_b_kernel,
        grid=(t // tm,),
        in_specs=[pl.BlockSpec((tm, d), lambda i: (i, 0)), pl.BlockSpec((1, d), lambda i: (0, 0)),
                  pl.BlockSpec(wt.shape, lambda i: (0, 0)), col(H_HD), col(H_HD)],
        out_specs=sub_specs + [col(N_MEM_HEADS * HEAD_DIM)],
        out_shape=sub_shapes + [jax.ShapeDtypeStruct((N_MEM_HEADS * HEAD_DIM, t), BF16)],
        scratch_shapes=[pltpu.VMEM((2, gw, tm), F32), pltpu.VMEM((3, gw // LANES, tm, LANES), F32)],
        compiler_params=_params("parallel"),
        name="inproj_b",
    )(x, g.reshape(1, d), wt, tabs[0], tabs[1])


TQ = 256
CR = 32
COARSE_BITS = 16
FINE_BITS = 17
BOUND_SLACK = 1.01
UNDERFLOW_GUARD = 2.0 ** -100


def _key_to_f32(key):
    bits = jnp.where(key < 0, key ^ jnp.int32(0x7FFFFFFF), key)
    return pltpu.bitcast(bits, F32)


def _dsa_kernel(q_ref, qi_ref, wi_ref, ki_ref, k_ref, va_ref, kn_ref, o_ref,
                sc_scr, sc16_scr, tau_scr, need_scr, tie_scr, tri_scr, bound_scr, m_scr, alpha_scr, acc_scr, s0_scr, s1_scr, p0_scr, p1_scr,
                *, seq, n_sel):
    i = pl.program_id(1)
    nch = i + 1
    krow = lax.broadcasted_iota(I32, (TQ, TQ), 0)
    qcol = lax.broadcasted_iota(I32, (TQ, TQ), 1)

    def chunk_off(c):
        return pl.multiple_of(c * TQ, TQ)

    def score_chunk(c, diag):
        off = chunk_off(c)
        kic = ki_ref[pl.ds(off, TQ), :]
        sc = jnp.zeros((TQ, TQ), F32)
        for h in range(IDX_HEADS):
            lg = _dot(kic, qi_ref[h * IDX_DIM:(h + 1) * IDX_DIM, :])
            sc = sc + jnp.maximum(lg, 0.0) * wi_ref[h:h + 1, :]
        if diag:
            sc = jnp.where(krow > qcol, -jnp.inf, sc)
        sc_scr[pl.ds(off, TQ), :] = sc
        sc16_scr[pl.ds(off, TQ), :] = sc.astype(BF16)

    def score_body(c, carry):
        score_chunk(c, False)
        return carry

    lax.fori_loop(0, i, score_body, 0)
    score_chunk(i, True)

    def count(pred):
        def body(c, acc):
            off = chunk_off(c)
            ind = pred(sc_scr[pl.ds(off, TQ), :], off)
            return acc + jnp.sum(ind.reshape(TQ // CR, CR, TQ), axis=0)
        acc = lax.fori_loop(0, nch, body, jnp.zeros((CR, TQ), F32))
        return jnp.sum(acc, axis=0, keepdims=True)

    @pl.when(i * TQ < n_sel)
    def _():
        tau_scr[...] = jnp.full((1, TQ), -jnp.inf, F32)
        need_scr[...] = jnp.zeros((1, TQ), F32)

    @pl.when(i * TQ >= n_sel)
    def _():
        def body16(step, tau16):
            cand = tau16 + jnp.left_shift(jnp.int32(1), COARSE_BITS - 1 - step)
            bits = jnp.where(cand < 0, cand ^ jnp.int32(0x7FFF), cand)
            cand_f = pltpu.bitcast(jnp.left_shift(bits, COARSE_BITS), F32).astype(BF16)

            def chunk(c, acc):
                ind = jnp.where(sc16_scr[pl.ds(chunk_off(c), TQ), :] >= cand_f,
                                jnp.bfloat16(1.0), jnp.bfloat16(0.0))
                return functools.reduce(lambda a, k: a + ind[k * CR:(k + 1) * CR], range(TQ // CR), acc)
            acc = lax.fori_loop(0, nch, chunk, jnp.zeros((CR, TQ), BF16))
            ok = jnp.sum(acc.astype(F32), axis=0, keepdims=True) >= float(n_sel)
            return jnp.where(ok, cand, tau16)

        tau16 = lax.fori_loop(0, COARSE_BITS, body16, jnp.full((1, TQ), -(1 << (COARSE_BITS - 1)), I32))

        key16 = jnp.left_shift(tau16, COARSE_BITS) + jnp.where(tau16 < 0, (1 << COARSE_BITS) - 1, 0)
        lo = key16 - ((1 << (COARSE_BITS - 1)) + 1)

        def body(step, carry):
            delta, cge, crej = carry
            cand = delta + jnp.left_shift(jnp.int32(1), FINE_BITS - 1 - step)
            cand_f = _key_to_f32(lo + cand)
            cnt = count(lambda blk, _: jnp.where(blk >= cand_f, 1.0, 0.0))
            ok = cnt >= float(n_sel)
            return jnp.where(ok, cand, delta), jnp.where(ok, cnt, cge), jnp.where(ok, crej, cnt)

        init = (jnp.zeros((1, TQ), I32), jnp.full((1, TQ), 2.0 * n_sel, F32), jnp.zeros((1, TQ), F32))
        delta, cge, crej = lax.fori_loop(0, FINE_BITS, body, init)
        tau = lo + delta
        tau_scr[...] = _key_to_f32(tau)
        need_scr[...] = jnp.where(cge > float(n_sel), float(n_sel) - crej, 2.0 * seq)

    tri_scr[...] = jnp.where(krow >= qcol, 1.0, 0.0).astype(BF16)
    grp = N_HEADS_A // N_KV_A
    tau_f = tau_scr[...]
    need = need_scr[...]

    def chunk_select(c):
        blk = sc_scr[pl.ds(chunk_off(c), TQ), :]
        eq = blk == tau_f
        rank = _dot(tri_scr[...], jnp.where(eq, 1.0, 0.0).astype(BF16)) + tie_scr[...]
        tie_scr[...] = rank[TQ - 1:TQ, :]
        return jnp.where(eq, jnp.where(rank <= need, 1.0, 0.0), jnp.where(blk > tau_f, 1.0, 0.0))

    kmax2 = jnp.max(kn_ref[...], axis=1, keepdims=True)
    for h in range(N_HEADS_A):
        qf = q_ref[h * HEAD_DIM:(h + 1) * HEAD_DIM, :].astype(F32)
        qn2 = jnp.sum(qf * qf, axis=0, keepdims=True)
        bound_scr[h] = jnp.sqrt(qn2 * kmax2[h // grp:h // grp + 1]) * BOUND_SLACK
    acc_scr[...] = jnp.zeros(acc_scr.shape, F32)
    tie_scr[...] = jnp.zeros((1, TQ), F32)

    def pipeline(stage_a, stage_b, buf0, buf1):
        def step(c, src, dst):
            stage_b(c - 1, src)
            stage_a(c, dst)

        stage_a(0, buf0)

        def pair_body(t, carry):
            step(2 * t + 1, buf0, buf1)
            step(2 * t + 2, buf1, buf0)
            return carry

        lax.fori_loop(0, (nch - 1) // 2, pair_body, 0)

        @pl.when((nch - 1) % 2 == 1)
        def _():
            step(nch - 1, buf0, buf1)
            stage_b(nch - 1, buf1)

        @pl.when((nch - 1) % 2 == 0)
        def _():
            stage_b(nch - 1, buf0)

    def fast_a(c, p_dst):
        off = chunk_off(c)
        sel = chunk_select(c).astype(BF16)
        for g in range(N_KV_A):
            kc = k_ref[pl.ds(off, TQ), g * HEAD_DIM:(g + 1) * HEAD_DIM]
            for j in range(grp):
                h = g * grp + j
                s = _dot(kc, q_ref[h * HEAD_DIM:(h + 1) * HEAD_DIM, :])
                p_dst[h] = jnp.exp2(s - bound_scr[h]).astype(BF16) * sel

    def fast_b(c, p_src):
        off = chunk_off(c)
        for g in range(N_KV_A):
            vt = va_ref[g * VR:(g + 1) * VR, pl.ds(off, TQ)]
            for j in range(grp):
                h = g * grp + j
                acc_scr[h] += _dot(vt, p_src[h])

    pipeline(fast_a, fast_b, p0_scr, p1_scr)
    lmin = functools.reduce(jnp.minimum, [acc_scr[h, HEAD_DIM:HEAD_DIM + 1, :] for h in range(N_HEADS_A)])

    def stage_a(c, s_dst):
        off = chunk_off(c)
        bias = (chunk_select(c) - 1.0) * (-NEG)
        for g in range(N_KV_A):
            kc = k_ref[pl.ds(off, TQ), g * HEAD_DIM:(g + 1) * HEAD_DIM]
            for j in range(grp):
                h = g * grp + j
                s = _dot(kc, q_ref[h * HEAD_DIM:(h + 1) * HEAD_DIM, :]) + bias
                s_dst[h] = s
                m_old = m_scr[h]
                m_new = jnp.maximum(m_old, jnp.max(s, axis=0, keepdims=True))
                alpha_scr[h] = jnp.exp2(m_old - m_new)
                m_scr[h] = m_new

    def stage_b(c, s_src):
        off = chunk_off(c)
        for g in range(N_KV_A):
            vt = va_ref[g * VR:(g + 1) * VR, pl.ds(off, TQ)]
            for j in range(grp):
                h = g * grp + j
                p = jnp.exp2(s_src[h] - m_scr[h]).astype(BF16)
                acc_scr[h] = alpha_scr[h] * acc_scr[h] + _dot(vt, p)

    @pl.when(jnp.logical_not(jnp.min(lmin) > UNDERFLOW_GUARD))
    def _():
        m_scr[...] = jnp.full(m_scr.shape, NEG, F32)
        acc_scr[...] = jnp.zeros(acc_scr.shape, F32)
        tie_scr[...] = jnp.zeros((1, TQ), F32)
        pipeline(stage_a, stage_b, s0_scr, s1_scr)

    for h in range(N_HEADS_A):
        a = acc_scr[h]
        o_ref[h * HEAD_DIM:(h + 1) * HEAD_DIM, :] = (a[:HEAD_DIM] / a[HEAD_DIM:HEAD_DIM + 1]).astype(BF16)


def _dsa_attention(qt, qit, wit, ki, k, vat, kn, b, s):
    n_sel = min(TOPK_MAX, s // 4)
    assert s % TQ == 0 and n_sel % TQ == 0
    nq = s // TQ
    qblk = lambda r: pl.BlockSpec((r, TQ), lambda bi, i: (0, bi * nq + i))
    tok = lambda c: pl.BlockSpec((s, c), lambda bi, i: (bi, 0))
    return pl.pallas_call(
        functools.partial(_dsa_kernel, seq=s, n_sel=n_sel),
        grid=(b, nq),
        in_specs=[qblk(qt.shape[0]), qblk(qit.shape[0]), qblk(wit.shape[0]),
                  tok(ki.shape[1]), tok(k.shape[1]),
                  pl.BlockSpec((vat.shape[0], s), lambda bi, i: (0, bi)),
                  pl.BlockSpec((kn.shape[0], s), lambda bi, i: (0, bi))],
        out_specs=qblk(qt.shape[0]),
        out_shape=jax.ShapeDtypeStruct(qt.shape, BF16),
        scratch_shapes=[
            pltpu.VMEM((s, TQ), F32),
            pltpu.VMEM((s, TQ), BF16),
            pltpu.VMEM((1, TQ), F32),
            pltpu.VMEM((1, TQ), F32),
            pltpu.VMEM((1, TQ), F32),
            pltpu.VMEM((TQ, TQ), BF16),
            pltpu.VMEM((N_HEADS_A, 1, TQ), F32),
            pltpu.VMEM((N_HEADS_A, 1, TQ), F32),
            pltpu.VMEM((N_HEADS_A, 1, TQ), F32),
            pltpu.VMEM((N_HEADS_A, VR, TQ), F32),
            pltpu.VMEM((N_HEADS_A, TQ, TQ), F32),
            pltpu.VMEM((N_HEADS_A, TQ, TQ), F32),
            pltpu.VMEM((N_HEADS_A, TQ, TQ), BF16),
            pltpu.VMEM((N_HEADS_A, TQ, TQ), BF16),
        ],
        compiler_params=_params("parallel", "arbitrary"),
        name="dsa_attention",
    )(qt, qit, wit, ki, k, vat, kn)


def _mem_attn_kernel(q_ref, k_ref, v_ref, o_ref):
    for h in range(N_MEM_HEADS):
        sl = slice(h * HEAD_DIM, (h + 1) * HEAD_DIM)
        s = _dot(k_ref[:, sl], q_ref[sl, :])
        p = jnp.exp(s - jnp.max(s, axis=0, keepdims=True))
        l = jnp.sum(p, axis=0, keepdims=True)
        o_ref[sl, :] = (_dot(v_ref[sl, :], p.astype(BF16)) / l).astype(BF16)


def _mem_attention(qmt, km, vmt, s):
    c, t = qmt.shape
    m = km.shape[1]
    tm = min(512, s)
    nq = s // tm
    blk = pl.BlockSpec((c, tm), lambda bi, i: (0, bi * nq + i))
    return pl.pallas_call(
        _mem_attn_kernel,
        grid=(t // s, nq),
        in_specs=[blk,
                  pl.BlockSpec((None, m, km.shape[2]), lambda bi, i: (bi, 0, 0)),
                  pl.BlockSpec((None, vmt.shape[1], m), lambda bi, i: (bi, 0, 0))],
        out_specs=blk,
        out_shape=jax.ShapeDtypeStruct((c, t), BF16),
        compiler_params=_params("parallel", "parallel"),
        name="mem_attention",
    )(qmt, km, vmt)


QB = 512
LN2 = 0.6931471805599453


def _band_kernel(q_ref, kp_ref, kc_ref, vp_ref, vc_ref, o_ref, lse_ref, s_scr, m_scr, *, qb):
    j = pl.program_id(1)
    nsub = qb // BLK
    krow = lax.broadcasted_iota(I32, (BLK, BLK), 0)
    qcol = lax.broadcasted_iota(I32, (BLK, BLK), 1)
    bias_prev = jnp.where(krow >= qcol, 0.0, NEG)
    bias_cur = jnp.where(krow <= qcol, 0.0, NEG)
    no_prev = jnp.where(j > 0, 0.0, NEG)

    def stage_a(sb):
        qs = slice(sb * BLK, (sb + 1) * BLK)
        for h in range(HEADS_PER_DIL):
            hs = slice(h * HEAD_DIM, (h + 1) * HEAD_DIM)
            qh = q_ref[qs, hs]
            if sb == 0:
                s_p = _dot_nt(kp_ref[:, hs], qh) + (bias_prev + no_prev)
            else:
                s_p = _dot_nt(kc_ref[(sb - 1) * BLK:sb * BLK, hs], qh) + bias_prev
            s_c = _dot_nt(kc_ref[qs, hs], qh) + bias_cur
            s_scr[sb % 2, h, 0:BLK] = s_p
            s_scr[sb % 2, h, BLK:2 * BLK] = s_c
            m_scr[sb % 2, h] = jnp.maximum(jnp.max(s_p, axis=0, keepdims=True),
                                           jnp.max(s_c, axis=0, keepdims=True))

    def stage_b(sb):
        qs = slice(sb * BLK, (sb + 1) * BLK)
        for h in range(HEADS_PER_DIL):
            hs = slice(h * HEAD_DIM, (h + 1) * HEAD_DIM)
            m = m_scr[sb % 2, h]
            p_p = jnp.exp2(s_scr[sb % 2, h, 0:BLK] - m)
            p_c = jnp.exp2(s_scr[sb % 2, h, BLK:2 * BLK] - m)
            l = jnp.sum(p_p, axis=0, keepdims=True) + jnp.sum(p_c, axis=0, keepdims=True)
            v_p = vp_ref[:, hs] if sb == 0 else vc_ref[(sb - 1) * BLK:sb * BLK, hs]
            acc = _dot_tn(v_p, p_p.astype(BF16)) + _dot_tn(vc_ref[qs, hs], p_c.astype(BF16))
            o_ref[hs, qs] = (acc / l).astype(o_ref.dtype)
            lse_ref[h:h + 1, qs] = m * LN2 + jnp.log(l)

    stage_a(0)
    for sb in range(1, nsub):
        stage_b(sb - 1)
        stage_a(sb)
    stage_b(nsub - 1)


def _band_attention(q, k, v, b, dil):
    rows, width = q.shape
    c = width // dil
    t = rows * dil
    n = t // (b * dil)
    qb = min(QB, n)
    assert n % qb == 0 and qb % BLK == 0
    rr, nj = qb // BLK, n // qb
    cur = pl.BlockSpec((qb, c), lambda si, j: ((si // dil) * nj + j, si % dil))
    prev = pl.BlockSpec((BLK, c), lambda si, j: ((si // dil) * (n // BLK) + jnp.maximum(j * rr - 1, 0), si % dil))
    fm = lambda r: pl.BlockSpec((r, qb), lambda si, j: (0, si * nj + j))
    return pl.pallas_call(
        functools.partial(_band_kernel, qb=qb),
        grid=(t // n, nj),
        in_specs=[cur, prev, cur, prev, cur],
        out_specs=[fm(c), fm(HEADS_PER_DIL)],
        out_shape=[jax.ShapeDtypeStruct((c, t), BF16), jax.ShapeDtypeStruct((HEADS_PER_DIL, t), F32)],
        scratch_shapes=[pltpu.VMEM((2, HEADS_PER_DIL, 2 * BLK, BLK), F32),
                        pltpu.VMEM((2, HEADS_PER_DIL, 1, BLK), F32)],
        compiler_params=_params("parallel", "parallel"),
        name="band_attention",
    )(q, k, k, v, v)


def _merge_kernel(*refs):
    ng = len(DIL_PATTERNS)
    o_refs, l_refs, out_ref = refs[:ng], refs[ng:2 * ng], refs[2 * ng]
    lses = [r[...] for r in l_refs]
    m = functools.reduce(jnp.maximum, lses)
    es = [jnp.exp(l - m) for l in lses]
    den = sum(es)
    ws = [e / den for e in es]
    for h in range(HEADS_PER_DIL):
        hs = slice(h * HEAD_DIM, (h + 1) * HEAD_DIM)
        out_ref[hs, :] = sum(w[h:h + 1] * o[hs, :].astype(F32) for w, o in zip(ws, o_refs)).astype(out_ref.dtype)


def _merge_groups(os_, lses):
    c, t = os_[0].shape
    tm = min(2048, t)
    spec = lambda r: pl.BlockSpec((r, tm), lambda i: (0, i))
    return pl.pallas_call(
        _merge_kernel,
        grid=(t // tm,),
        in_specs=[spec(c)] * len(os_) + [spec(HEADS_PER_DIL)] * len(lses),
        out_specs=spec(c),
        out_shape=jax.ShapeDtypeStruct((c, t), BF16),
        compiler_params=_params("parallel"),
        name="merge_groups",
    )(*os_, *lses)


def _split_residues(a, b, dil):
    f, t = a.shape
    return a.reshape(f, b, t // (b * dil), dil).transpose(0, 1, 3, 2)


def _to_sub_fm(a, b, dil):
    return a if dil == 1 else _split_residues(a, b, dil).reshape(a.shape)


def _to_sub_tok(a, b, dil):
    return a.T if dil == 1 else _split_residues(a, b, dil).transpose(1, 2, 3, 0).reshape(a.shape[1], a.shape[0])


def _from_sub_fm(a, b, dil):
    if dil == 1:
        return a
    f, t = a.shape
    return a.reshape(f, b, dil, t // (b * dil)).transpose(0, 1, 3, 2).reshape(f, t)


def _ffn_kernel(x_ref, mix_ref, mo_ref, wo1_ref, wo2_ref, g_ref, wgu_ref, wd_ref, gf_ref,
                o_ref, act_scr, *, final_norm, tf):
    dff = wd_ref.shape[0]
    x2 = x_ref[...] + _dot_tn(mix_ref[...], wo1_ref[...]) + _dot_tn(mo_ref[...], wo2_ref[...])
    h = _rms(x2, g_ref[...]).astype(BF16)
    for f in range(dff // tf):
        gate = _dot(h, wgu_ref[:, f * tf:(f + 1) * tf])
        up = _dot(h, wgu_ref[:, dff + f * tf:dff + (f + 1) * tf])
        act_scr[:, f * tf:(f + 1) * tf] = (gate * jax.nn.sigmoid(gate) * up).astype(BF16)
    y = x2 + _dot(act_scr[...], wd_ref[...])
    if final_norm:
        y = _rms(y, gf_ref[...])
    o_ref[...] = y


def _out_ffn(x, mix, mo, w_out, g_ffn, w_gate_up, w_down, g_final, final_norm):
    t, d = x.shape
    cm, cmo = mix.shape[0], mo.shape[0]
    dff = w_down.shape[0]
    wo1 = w_out[:cm].astype(BF16)
    wo2 = w_out[cm:].astype(BF16)
    wgu = w_gate_up.astype(BF16)
    wd = w_down.astype(BF16)
    tm = min(512, t)
    tf = 256 if dff % 256 == 0 else dff
    row = lambda c: pl.BlockSpec((tm, c), lambda i: (i, 0))
    const = lambda r, c: pl.BlockSpec((r, c), lambda i: (0, 0), pipeline_mode=pl.Buffered(1))
    return pl.pallas_call(
        functools.partial(_ffn_kernel, final_norm=final_norm, tf=tf),
        grid=(t // tm,),
        in_specs=[row(d), pl.BlockSpec((cm, tm), lambda i: (0, i)), pl.BlockSpec((cmo, tm), lambda i: (0, i)),
                  const(cm, d), const(cmo, d), const(1, d),
                  const(d, 2 * dff), const(dff, d), const(1, d)],
        out_specs=row(d),
        out_shape=jax.ShapeDtypeStruct((t, d), F32),
        scratch_shapes=[pltpu.VMEM((tm, dff), BF16)],
        compiler_params=_params("parallel"),
        name="out_ffn",
    )(x, mix, mo, wo1, wo2, g_ffn.reshape(1, d), wgu, wd, g_final.reshape(1, d))


def kernel(x, mem, positions,
           l0_norm_mix, l0_norm_mem, l0_w_in, l0_w_mem_kv, l0_w_out, l0_norm_ffn, l0_w_gate_up, l0_w_down,
           l1_norm_mix, l1_norm_mem, l1_w_in, l1_w_mem_kv, l1_w_out, l1_norm_ffn, l1_w_gate_up, l1_w_down,
           final_norm):
    b, s, d = x.shape
    t = b * s
    xt = x.reshape(t, d)
    tabs = _trig_tables(positions.reshape(1, t))

    qt, k, vat, qit, ki, wit, qmt, kn = _inproj_a(xt, l0_norm_mix, l0_w_in, tabs)
    mix = _dsa_attention(qt, qit, wit, ki, k, vat, kn, b, s)
    mo = _mem_attention(qmt, *_mem_kv(mem, l0_norm_mem, l0_w_mem_kv), s)
    xt = _out_ffn(xt, mix, mo, l0_w_out, l0_norm_ffn, l0_w_gate_up, l0_w_down, final_norm, False)

    outs = _inproj_b(xt, l1_norm_mix, l1_w_in, tabs)
    os_, lses = [], []
    for g, (window, dil) in enumerate(DIL_PATTERNS):
        assert window // dil == BLK
        o, lse = _band_attention(*outs[3 * g:3 * g + 3], b, dil)
        os_.append(_from_sub_fm(o, b, dil))
        lses.append(_from_sub_fm(lse, b, dil))
    mix = _merge_groups(os_, lses)
    mo = _mem_attention(outs[-1], *_mem_kv(mem, l1_norm_mem, l1_w_mem_kv), s)
    xt = _out_ffn(xt, mix, mo, l1_w_out, l1_norm_ffn, l1_w_gate_up, l1_w_down, final_norm, True)
    return xt.reshape(b, s, d)
```

```python
import functools

import jax
import jax.numpy as jnp
from jax import lax
from jax.experimental import pallas as pl
from jax.experimental.pallas import tpu as pltpu

F32 = jnp.float32
BF16 = jnp.bfloat16
I32 = jnp.int32

HEAD_DIM = 64
N_HEADS_A = 12
N_KV_A = 4
IDX_HEADS = 8
IDX_DIM = 64
IDX_ROPE_DIM = 32
TOPK_MAX = 256
DIL_PATTERNS = ((128, 1), (512, 4), (2048, 16))
HEADS_PER_DIL = 4
N_MEM_HEADS = 4
BLK = 128
ROPE_THETA = 10000.0
EPS = 1e-6
NEG = -1e30
INT_MIN = -2147483648

LANES = 128
VMEM_LIMIT = 56 * 1024 * 1024

Q_SCALE = HEAD_DIM ** -0.5
WI_SCALE = IDX_HEADS ** -0.5 * IDX_DIM ** -0.5
LOG2E = 1.4426950408889634


def _dot(a, b):
    return jnp.dot(a, b, preferred_element_type=F32)


def _dot_nt(a, b):
    return lax.dot_general(a, b, (((1,), (1,)), ((), ())), preferred_element_type=F32)


def _dot_tn(a, b):
    return lax.dot_general(a, b, (((0,), (0,)), ((), ())), preferred_element_type=F32)


def _params(*sem):
    return pltpu.CompilerParams(dimension_semantics=sem, vmem_limit_bytes=VMEM_LIMIT)


def _rms(x, g):
    ms = jnp.mean(x * x, axis=-1, keepdims=True)
    return x * lax.rsqrt(ms + EPS) * g


H_HD = HEAD_DIM // 2
H_IX = IDX_ROPE_DIM // 2


def _trig_kernel(pos_ref, f_ref, chd_ref, shd_ref, cix_ref, six_ref):
    tm = pos_ref.shape[1]
    pos = pos_ref[...].astype(F32)
    f = jnp.concatenate([f_ref[...]] * (tm // LANES), axis=1)
    ang = f * pos
    c, s = jnp.cos(ang), jnp.sin(ang)
    chd_ref[...] = c[:H_HD]
    shd_ref[...] = s[:H_HD]
    cix_ref[...] = c[H_HD:]
    six_ref[...] = s[H_HD:]


def _trig_tables(pos_row):
    t = pos_row.shape[1]
    tm = min(2048, t)
    f_hd = ROPE_THETA ** (-jnp.arange(H_HD, dtype=F32) / H_HD)
    f_ix = ROPE_THETA ** (-jnp.arange(H_IX, dtype=F32) / H_IX)
    f = jnp.broadcast_to(jnp.concatenate([f_hd, f_ix])[:, None], (H_HD + H_IX, LANES))
    spec = lambda r: pl.BlockSpec((r, tm), lambda i: (0, i))
    rows = [H_HD, H_HD, H_IX, H_IX]
    return pl.pallas_call(
        _trig_kernel,
        grid=(t // tm,),
        in_specs=[spec(1), pl.BlockSpec((H_HD + H_IX, LANES), lambda i: (0, 0))],
        out_specs=[spec(r) for r in rows],
        out_shape=[jax.ShapeDtypeStruct((r, t), F32) for r in rows],
        compiler_params=_params("parallel"),
        name="rope_tables",
    )(pos_row, f)


def _norm_matmul_kernel(x_ref, g_ref, w_ref, o_ref):
    h = _rms(x_ref[...], g_ref[...]).astype(BF16)
    o_ref[...] = _dot(h, w_ref[...]).astype(o_ref.dtype)


def _norm_matmul(x, g, w, out_dtype):
    t, d = x.shape
    n = w.shape[1]
    tm = min(512, t)
    return pl.pallas_call(
        _norm_matmul_kernel,
        grid=(t // tm,),
        in_specs=[pl.BlockSpec((tm, d), lambda i: (i, 0)),
                  pl.BlockSpec((1, d), lambda i: (0, 0)),
                  pl.BlockSpec((d, n), lambda i: (0, 0))],
        out_specs=pl.BlockSpec((tm, n), lambda i: (i, 0)),
        out_shape=jax.ShapeDtypeStruct((t, n), out_dtype),
        compiler_params=_params("parallel"),
        name="norm_matmul",
    )(x, g.reshape(1, d), w)


def _mem_kv(mem, g, w_kv):
    b, m, d = mem.shape
    kv = _norm_matmul(mem.reshape(b * m, d), g, w_kv.astype(BF16), BF16).reshape(b, m, -1)
    nk = N_MEM_HEADS * HEAD_DIM
    return kv[:, :, :nk], kv[:, :, nk:].transpose(0, 2, 1)


VR = 80

A_Q, A_K, A_V, A_QI, A_KI, A_WI, A_QM, A_END = 0, 768, 1024, 1280, 1792, 1856, 1872, 2128


def _rope_heads(p, nheads, out_ref, half, c, s, scale):
    for hh in range(nheads):
        r0 = hh * HEAD_DIM
        x1, x2 = p[r0:r0 + half], p[r0 + half:r0 + 2 * half]
        out_ref[r0:r0 + half, :] = ((x1 * c - x2 * s) * scale).astype(out_ref.dtype)
        out_ref[r0 + half:r0 + 2 * half, :] = ((x2 * c + x1 * s) * scale).astype(out_ref.dtype)
        if 2 * half < HEAD_DIM:
            out_ref[r0 + 2 * half:r0 + HEAD_DIM, :] = (p[r0 + 2 * half:r0 + HEAD_DIM] * scale).astype(out_ref.dtype)


def _write_values(pv, nheads, va_ref):
    tm = pv.shape[1]
    ones_rows = jnp.where(lax.broadcasted_iota(I32, (VR - HEAD_DIM, tm), 0) == 0, 1.0, 0.0).astype(BF16)
    for g in range(nheads):
        va_ref[g * VR:g * VR + HEAD_DIM, :] = pv[g * HEAD_DIM:(g + 1) * HEAD_DIM].astype(BF16)
        va_ref[g * VR + HEAD_DIM:(g + 1) * VR, :] = ones_rows


def _inproj_a_kernel(x_ref, g_ref, wt_ref, chd_ref, shd_ref, cix_ref, six_ref,
                     q_ref, k_ref, va_ref, qi_ref, ki_ref, wi_ref, qm_ref, kn_ref, kt_scr):
    h = _rms(x_ref[...], g_ref[...]).astype(BF16)
    chd, shd = chd_ref[...], shd_ref[...]
    cix, six = cix_ref[...], six_ref[...]

    def proj(a, b):
        return _dot_nt(wt_ref[a:b, :], h)

    _rope_heads(proj(A_Q, A_K), N_HEADS_A, q_ref, H_HD, chd, shd, Q_SCALE * LOG2E)
    nk = N_KV_A * HEAD_DIM
    _rope_heads(proj(A_K, A_V), N_KV_A, kt_scr.at[0:nk], H_HD, chd, shd, 1.0)
    _write_values(proj(A_V, A_QI), N_KV_A, va_ref)
    _rope_heads(proj(A_QI, A_KI), IDX_HEADS, qi_ref, H_IX, cix, six, 1.0)
    pkw = proj(A_KI, A_QM)
    _rope_heads(pkw, 1, kt_scr.at[nk:nk + IDX_DIM], H_IX, cix, six, 1.0)
    k_ref[...] = kt_scr[0:nk, :].T.astype(BF16)
    ki_ref[...] = kt_scr[nk:nk + IDX_DIM, :].T.astype(BF16)
    for g in range(N_KV_A):
        kg = kt_scr[g * HEAD_DIM:(g + 1) * HEAD_DIM, :]
        kn_ref[g:g + 1, :] = jnp.sum(kg * kg, axis=0, keepdims=True)
    wi_ref[...] = pkw[IDX_DIM:IDX_DIM + IDX_HEADS] * WI_SCALE
    qm_ref[...] = (proj(A_QM, A_END) * Q_SCALE).astype(BF16)


def _inproj_a(x, g, w_in, tabs):
    t, d = x.shape
    wt = w_in.T
    pad = jnp.zeros((A_QM - A_WI - IDX_HEADS, d), w_in.dtype)
    split = A_WI + IDX_HEADS
    wt = jnp.concatenate([wt[:split], pad, wt[split:]], axis=0).astype(BF16)
    tm = min(512, t)
    col = lambda r: pl.BlockSpec((r, tm), lambda i: (0, i))
    outs = [(N_HEADS_A * HEAD_DIM, BF16), (N_KV_A * HEAD_DIM, BF16), (N_KV_A * VR, BF16),
            (IDX_HEADS * IDX_DIM, BF16), (IDX_DIM, BF16), (IDX_HEADS, F32), (N_MEM_HEADS * HEAD_DIM, BF16),
            (N_KV_A, F32)]
    return pl.pallas_call(
        _inproj_a_kernel,
        grid=(t // tm,),
        in_specs=[pl.BlockSpec((tm, d), lambda i: (i, 0)), pl.BlockSpec((1, d), lambda i: (0, 0)),
                  pl.BlockSpec((A_END, d), lambda i: (0, 0)),
                  col(H_HD), col(H_HD), col(H_IX), col(H_IX)],
        out_specs=[pl.BlockSpec((tm, r), lambda i: (i, 0)) if k in (1, 4) else col(r)
                   for k, (r, _) in enumerate(outs)],
        out_shape=[jax.ShapeDtypeStruct((t, r) if k in (1, 4) else (r, t), dt) for k, (r, dt) in enumerate(outs)],
        scratch_shapes=[pltpu.VMEM((N_KV_A * HEAD_DIM + IDX_DIM, tm), F32)],
        compiler_params=_params("parallel"),
        name="inproj_a",
    )(x, g.reshape(1, d), wt, *tabs)


def _inproj_b_kernel(x_ref, g_ref, wt_ref, chd_ref, shd_ref, *refs):
    ng = len(DIL_PATTERNS)
    out_refs, (rope_scr, tok_scr) = refs[:3 * ng + 1], refs[3 * ng + 1:]
    h = _rms(x_ref[...], g_ref[...]).astype(BF16)
    chd, shd = chd_ref[...], shd_ref[...]
    gw = HEADS_PER_DIL * HEAD_DIM
    tm = h.shape[0]

    def emit(out_ref, slot, dil, value_t):
        tok = value_t.T
        if dil == 1:
            out_ref[...] = tok.astype(BF16)
        else:
            for cb in range(gw // LANES):
                tok_scr[slot, cb] = tok[:, cb * LANES:(cb + 1) * LANES]
            for r in range(dil):
                for cb in range(gw // LANES):
                    out_ref[:, r * gw + cb * LANES:r * gw + (cb + 1) * LANES] = (
                        tok_scr[slot, cb, pl.ds(r, tm // dil, stride=dil), :].astype(BF16))

    for g, (_, dil) in enumerate(DIL_PATTERNS):
        q_ref, k_ref, v_ref = out_refs[3 * g:3 * g + 3]
        base = 3 * g * gw
        _rope_heads(_dot_nt(wt_ref[base:base + gw, :], h), HEADS_PER_DIL, rope_scr.at[0], H_HD, chd, shd,
                    Q_SCALE * LOG2E)
        emit(q_ref, 0, dil, rope_scr[0])
        _rope_heads(_dot_nt(wt_ref[base + gw:base + 2 * gw, :], h), HEADS_PER_DIL, rope_scr.at[1], H_HD, chd, shd, 1.0)
        emit(k_ref, 1, dil, rope_scr[1])
        emit(v_ref, 2, dil, _dot_nt(wt_ref[base + 2 * gw:base + 3 * gw, :], h))
    out_refs[3 * ng][...] = (_dot_nt(wt_ref[3 * ng * gw:3 * ng * gw + N_MEM_HEADS * HEAD_DIM, :], h)
                            * Q_SCALE).astype(BF16)


def _inproj_b(x, g, w_in, tabs):
    t, d = x.shape
    wt = w_in.T.astype(BF16)
    tm = min(512, t)
    col = lambda r: pl.BlockSpec((r, tm), lambda i: (0, i))
    gw = HEADS_PER_DIL * HEAD_DIM
    sub_specs, sub_shapes = [], []
    for _, dil in DIL_PATTERNS:
        assert tm % (16 * dil) == 0
        sub_specs += [pl.BlockSpec((tm // dil, dil * gw), lambda i: (i, 0))] * 3
        sub_shapes += [jax.ShapeDtypeStruct((t // dil, dil * gw), BF16)] * 3
    return pl.pallas_call(
        _inproj_b_kernel,
        grid=(t // tm,),
        in_specs=[pl.BlockSpec((tm, d), lambda i: (i, 0)), pl.BlockSpec((1, d), lambda i: (0, 0)),
                  pl.BlockSpec(wt.shape, lambda i: (0, 0)), col(H_HD), col(H_HD)],
        out_specs=sub_specs + [col(N_MEM_HEADS * HEAD_DIM)],
        out_shape=sub_shapes + [jax.ShapeDtypeStruct((N_MEM_HEADS * HEAD_DIM, t), BF16)],
        scratch_shapes=[pltpu.VMEM((2, gw, tm), F32), pltpu.VMEM((3, gw // LANES, tm, LANES), F32)],
        compiler_params=_params("parallel"),
        name="inproj_b",
    )(x, g.reshape(1, d), wt, tabs[0], tabs[1])


TQ = 256
CR = 32
COARSE_BITS = 16
FINE_BITS = 17
BOUND_SLACK = 1.01
UNDERFLOW_GUARD = 2.0 ** -100


def _key_to_f32(key):
    bits = jnp.where(key < 0, key ^ jnp.int32(0x7FFFFFFF), key)
    return pltpu.bitcast(bits, F32)


def _dsa_kernel(q_ref, qi_ref, wi_ref, ki_ref, k_ref, va_ref, kn_ref, o_ref,
                sc_scr, sc16_scr, tau_scr, need_scr, tie_scr, tri_scr, bound_scr, m_scr, alpha_scr, acc_scr, s0_scr, s1_scr, p0_scr, p1_scr,
                *, seq, n_sel):
    i = pl.program_id(1)
    nch = i + 1
    krow = lax.broadcasted_iota(I32, (TQ, TQ), 0)
    qcol = lax.broadcasted_iota(I32, (TQ, TQ), 1)

    def chunk_off(c):
        return pl.multiple_of(c * TQ, TQ)

    def score_chunk(c, diag):
        off = chunk_off(c)
        kic = ki_ref[pl.ds(off, TQ), :]
        sc = jnp.zeros((TQ, TQ), F32)
        for h in range(IDX_HEADS):
            lg = _dot(kic, qi_ref[h * IDX_DIM:(h + 1) * IDX_DIM, :])
            sc = sc + jnp.maximum(lg, 0.0) * wi_ref[h:h + 1, :]
        if diag:
            sc = jnp.where(krow > qcol, -jnp.inf, sc)
        sc_scr[pl.ds(off, TQ), :] = sc
        sc16_scr[pl.ds(off, TQ), :] = sc.astype(BF16)

    def score_body(c, carry):
        score_chunk(c, False)
        return carry

    lax.fori_loop(0, i, score_body, 0)
    score_chunk(i, True)

    def count(pred):
        def body(c, acc):
            off = chunk_off(c)
            ind = pred(sc_scr[pl.ds(off, TQ), :], off)
            return acc + jnp.sum(ind.reshape(TQ // CR, CR, TQ), axis=0)
        acc = lax.fori_loop(0, nch, body, jnp.zeros((CR, TQ), F32))
        return jnp.sum(acc, axis=0, keepdims=True)

    @pl.when(i * TQ < n_sel)
    def _():
        tau_scr[...] = jnp.full((1, TQ), -jnp.inf, F32)
        need_scr[...] = jnp.zeros((1, TQ), F32)

    @pl.when(i * TQ >= n_sel)
    def _():
        def body16(step, tau16):
            cand = tau16 + jnp.left_shift(jnp.int32(1), COARSE_BITS - 1 - step)
            bits = jnp.where(cand < 0, cand ^ jnp.int32(0x7FFF), cand)
            cand_f = pltpu.bitcast(jnp.left_shift(bits, COARSE_BITS), F32).astype(BF16)

            def chunk(c, acc):
                ind = jnp.where(sc16_scr[pl.ds(chunk_off(c), TQ), :] >= cand_f,
                                jnp.bfloat16(1.0), jnp.bfloat16(0.0))
                return functools.reduce(lambda a, k: a + ind[k * CR:(k + 1) * CR], range(TQ // CR), acc)
            acc = lax.fori_loop(0, nch, chunk, jnp.zeros((CR, TQ), BF16))
            ok = jnp.sum(acc.astype(F32), axis=0, keepdims=True) >= float(n_sel)
            return jnp.where(ok, cand, tau16)

        tau16 = lax.fori_loop(0, COARSE_BITS, body16, jnp.full((1, TQ), -(1 << (COARSE_BITS - 1)), I32))

        key16 = jnp.left_shift(tau16, COARSE_BITS) + jnp.where(tau16 < 0, (1 << COARSE_BITS) - 1, 0)
        lo = key16 - ((1 << (COARSE_BITS - 1)) + 1)

        def body(step, carry):
            delta, cge, crej = carry
            cand = delta + jnp.left_shift(jnp.int32(1), FINE_BITS - 1 - step)
            cand_f = _key_to_f32(lo + cand)
            cnt = count(lambda blk, _: jnp.where(blk >= cand_f, 1.0, 0.0))
            ok = cnt >= float(n_sel)
            return jnp.where(ok, cand, delta), jnp.where(ok, cnt, cge), jnp.where(ok, crej, cnt)

        init = (jnp.zeros((1, TQ), I32), jnp.full((1, TQ), 2.0 * n_sel, F32), jnp.zeros((1, TQ), F32))
        delta, cge, crej = lax.fori_loop(0, FINE_BITS, body, init)
        tau = lo + delta
        tau_scr[...] = _key_to_f32(tau)
        need_scr[...] = jnp.where(cge > float(n_sel), float(n_sel) - crej, 2.0 * seq)

    tri_scr[...] = jnp.where(krow >= qcol, 1.0, 0.0).astype(BF16)
    grp = N_HEADS_A // N_KV_A
    tau_f = tau_scr[...]
    need = need_scr[...]

    def chunk_select(c):
        blk = sc_scr[pl.ds(chunk_off(c), TQ), :]
        eq = blk == tau_f
        rank = _dot(tri_scr[...], jnp.where(eq, 1.0, 0.0).astype(BF16)) + tie_scr[...]
        tie_scr[...] = rank[TQ - 1:TQ, :]
        return jnp.where(eq, jnp.where(rank <= need, 1.0, 0.0), jnp.where(blk > tau_f, 1.0, 0.0))

    kmax2 = jnp.max(kn_ref[...], axis=1, keepdims=True)
    for h in range(N_HEADS_A):
        qf = q_ref[h * HEAD_DIM:(h + 1) * HEAD_DIM, :].astype(F32)
        qn2 = jnp.sum(qf * qf, axis=0, keepdims=True)
        bound_scr[h] = jnp.sqrt(qn2 * kmax2[h // grp:h // grp + 1]) * BOUND_SLACK
    acc_scr[...] = jnp.zeros(acc_scr.shape, F32)
    tie_scr[...] = jnp.zeros((1, TQ), F32)

    def pipeline(stage_a, stage_b, buf0, buf1):
        def step(c, src, dst):
            stage_b(c - 1, src)
            stage_a(c, dst)

        stage_a(0, buf0)

        def pair_body(t, carry):
            step(2 * t + 1, buf0, buf1)
            step(2 * t + 2, buf1, buf0)
            return carry

        lax.fori_loop(0, (nch - 1) // 2, pair_body, 0)

        @pl.when((nch - 1) % 2 == 1)
        def _():
            step(nch - 1, buf0, buf1)
            stage_b(nch - 1, buf1)

        @pl.when((nch - 1) % 2 == 0)
        def _():
            stage_b(nch - 1, buf0)

    def fast_a(c, p_dst):
        off = chunk_off(c)
        sel = chunk_select(c).astype(BF16)
        for g in range(N_KV_A):
            kc = k_ref[pl.ds(off, TQ), g * HEAD_DIM:(g + 1) * HEAD_DIM]
            for j in range(grp):
                h = g * grp + j
                s = _dot(kc, q_ref[h * HEAD_DIM:(h + 1) * HEAD_DIM, :])
                p_dst[h] = jnp.exp2(s - bound_scr[h]).astype(BF16) * sel

    def fast_b(c, p_src):
        off = chunk_off(c)
        for g in range(N_KV_A):
            vt = va_ref[g * VR:(g + 1) * VR, pl.ds(off, TQ)]
            for j in range(grp):
                h = g * grp + j
                acc_scr[h] += _dot(vt, p_src[h])

    pipeline(fast_a, fast_b, p0_scr, p1_scr)
    lmin = functools.reduce(jnp.minimum, [acc_scr[h, HEAD_DIM:HEAD_DIM + 1, :] for h in range(N_HEADS_A)])

    def stage_a(c, s_dst):
        off = chunk_off(c)
        bias = (chunk_select(c) - 1.0) * (-NEG)
        for g in range(N_KV_A):
            kc = k_ref[pl.ds(off, TQ), g * HEAD_DIM:(g + 1) * HEAD_DIM]
            for j in range(grp):
                h = g * grp + j
                s = _dot(kc, q_ref[h * HEAD_DIM:(h + 1) * HEAD_DIM, :]) + bias
                s_dst[h] = s
                m_old = m_scr[h]
                m_new = jnp.maximum(m_old, jnp.max(s, axis=0, keepdims=True))
                alpha_scr[h] = jnp.exp2(m_old - m_new)
                m_scr[h] = m_new

    def stage_b(c, s_src):
        off = chunk_off(c)
        for g in range(N_KV_A):
            vt = va_ref[g * VR:(g + 1) * VR, pl.ds(off, TQ)]
            for j in range(grp):
                h = g * grp + j
                p = jnp.exp2(s_src[h] - m_scr[h]).astype(BF16)
                acc_scr[h] = alpha_scr[h] * acc_scr[h] + _dot(vt, p)

    @pl.when(jnp.logical_not(jnp.min(lmin) > UNDERFLOW_GUARD))
    def _():
        m_scr[...] = jnp.full(m_scr.shape, NEG, F32)
        acc_scr[...] = jnp.zeros(acc_scr.shape, F32)
        tie_scr[...] = jnp.zeros((1, TQ), F32)
        pipeline(stage_a, stage_b, s0_scr, s1_scr)

    for h in range(N_HEADS_A):
        a = acc_scr[h]
        o_ref[h * HEAD_DIM:(h + 1) * HEAD_DIM, :] = (a[:HEAD_DIM] / a[HEAD_DIM:HEAD_DIM + 1]).astype(BF16)


def _dsa_attention(qt, qit, wit, ki, k, vat, kn, b, s):
    n_sel = min(TOPK_MAX, s // 4)
    assert s % TQ == 0 and n_sel % TQ == 0
    nq = s // TQ
    qblk = lambda r: pl.BlockSpec((r, TQ), lambda bi, i: (0, bi * nq + i))
    tok = lambda c: pl.BlockSpec((s, c), lambda bi, i: (bi, 0))
    return pl.pallas_call(
        functools.partial(_dsa_kernel, seq=s, n_sel=n_sel),
        grid=(b, nq),
        in_specs=[qblk(qt.shape[0]), qblk(qit.shape[0]), qblk(wit.shape[0]),
                  tok(ki.shape[1]), tok(k.shape[1]),
                  pl.BlockSpec((vat.shape[0], s), lambda bi, i: (0, bi)),
                  pl.BlockSpec((kn.shape[0], s), lambda bi, i: (0, bi))],
        out_specs=qblk(qt.shape[0]),
        out_shape=jax.ShapeDtypeStruct(qt.shape, BF16),
        scratch_shapes=[
            pltpu.VMEM((s, TQ), F32),
            pltpu.VMEM((s, TQ), BF16),
            pltpu.VMEM((1, TQ), F32),
            pltpu.VMEM((1, TQ), F32),
            pltpu.VMEM((1, TQ), F32),
            pltpu.VMEM((TQ, TQ), BF16),
            pltpu.VMEM((N_HEADS_A, 1, TQ), F32),
            pltpu.VMEM((N_HEADS_A, 1, TQ), F32),
            pltpu.VMEM((N_HEADS_A, 1, TQ), F32),
            pltpu.VMEM((N_HEADS_A, VR, TQ), F32),
            pltpu.VMEM((N_HEADS_A, TQ, TQ), F32),
            pltpu.VMEM((N_HEADS_A, TQ, TQ), F32),
            pltpu.VMEM((N_HEADS_A, TQ, TQ), BF16),
            pltpu.VMEM((N_HEADS_A, TQ, TQ), BF16),
        ],
        compiler_params=_params("parallel", "arbitrary"),
        name="dsa_attention",
    )(qt, qit, wit, ki, k, vat, kn)


def _mem_attn_kernel(q_ref, k_ref, v_ref, o_ref, s_scr, m_scr):
    nsub = q_ref.shape[1] // MEM_SUB

    def stage_a(sb):
        ts = slice(sb * MEM_SUB, (sb + 1) * MEM_SUB)
        for h in range(N_MEM_HEADS):
            sl = slice(h * HEAD_DIM, (h + 1) * HEAD_DIM)
            s = _dot(k_ref[:, sl], q_ref[sl, ts])
            s_scr[sb, h] = s
            m_scr[sb, h] = jnp.max(s, axis=0, keepdims=True)

    def stage_b(sb):
        ts = slice(sb * MEM_SUB, (sb + 1) * MEM_SUB)
        for h in range(N_MEM_HEADS):
            sl = slice(h * HEAD_DIM, (h + 1) * HEAD_DIM)
            p = jnp.exp(s_scr[sb, h] - m_scr[sb, h])
            l = jnp.sum(p, axis=0, keepdims=True)
            o_ref[sl, ts] = (_dot(v_ref[sl, :], p.astype(BF16)) / l).astype(BF16)

    stage_a(0)
    for sb in range(1, nsub):
        stage_b(sb - 1)
        stage_a(sb)
    stage_b(nsub - 1)


MEM_SUB = 512


def _mem_attention(qmt, km, vmt, s):
    c, t = qmt.shape
    m = km.shape[1]
    tm = min(1024, s)
    assert tm % MEM_SUB == 0
    nq = s // tm
    blk = pl.BlockSpec((c, tm), lambda bi, i: (0, bi * nq + i))
    return pl.pallas_call(
        _mem_attn_kernel,
        grid=(t // s, nq),
        in_specs=[blk,
                  pl.BlockSpec((None, m, km.shape[2]), lambda bi, i: (bi, 0, 0)),
                  pl.BlockSpec((None, vmt.shape[1], m), lambda bi, i: (bi, 0, 0))],
        out_specs=blk,
        out_shape=jax.ShapeDtypeStruct((c, t), BF16),
        scratch_shapes=[pltpu.VMEM((tm // MEM_SUB, N_MEM_HEADS, m, MEM_SUB), F32),
                        pltpu.VMEM((tm // MEM_SUB, N_MEM_HEADS, 1, MEM_SUB), F32)],
        compiler_params=_params("parallel", "parallel"),
        name="mem_attention",
    )(qmt, km, vmt)


QB = 512
LN2 = 0.6931471805599453


def _band_kernel(q_ref, kp_ref, kc_ref, vp_ref, vc_ref, o_ref, lse_ref, s_scr, m_scr, ot_scr, lt_scr, *, qb):
    j = pl.program_id(1)
    nsub = qb // BLK
    krow = lax.broadcasted_iota(I32, (BLK, BLK), 0)
    qcol = lax.broadcasted_iota(I32, (BLK, BLK), 1)
    bias_prev = jnp.where(krow >= qcol, 0.0, NEG)
    bias_cur = jnp.where(krow <= qcol, 0.0, NEG)
    no_prev = jnp.where(j > 0, 0.0, NEG)
    lt_scr[...] = jnp.zeros(lt_scr.shape, F32)

    def stage_a(sb):
        qs = slice(sb * BLK, (sb + 1) * BLK)
        for h in range(HEADS_PER_DIL):
            hs = slice(h * HEAD_DIM, (h + 1) * HEAD_DIM)
            qh = q_ref[qs, hs]
            if sb == 0:
                s_p = _dot_nt(kp_ref[:, hs], qh) + (bias_prev + no_prev)
            else:
                s_p = _dot_nt(kc_ref[(sb - 1) * BLK:sb * BLK, hs], qh) + bias_prev
            s_c = _dot_nt(kc_ref[qs, hs], qh) + bias_cur
            s_scr[sb, h, 0:BLK] = s_p
            s_scr[sb, h, BLK:2 * BLK] = s_c
            m_scr[sb, h] = jnp.maximum(jnp.max(s_p, axis=0, keepdims=True),
                                           jnp.max(s_c, axis=0, keepdims=True))

    def stage_b(sb):
        qs = slice(sb * BLK, (sb + 1) * BLK)
        for h in range(HEADS_PER_DIL):
            hs = slice(h * HEAD_DIM, (h + 1) * HEAD_DIM)
            m = m_scr[sb, h]
            p_p = jnp.exp2(s_scr[sb, h, 0:BLK] - m)
            p_c = jnp.exp2(s_scr[sb, h, BLK:2 * BLK] - m)
            l = jnp.sum(p_p, axis=0, keepdims=True) + jnp.sum(p_c, axis=0, keepdims=True)
            v_p = vp_ref[:, hs] if sb == 0 else vc_ref[(sb - 1) * BLK:sb * BLK, hs]
            acc = _dot_tn(v_p, p_p.astype(BF16)) + _dot_tn(vc_ref[qs, hs], p_c.astype(BF16))
            ot_scr[hs, qs] = acc / l
            lt_scr[h:h + 1, qs] = m * LN2 + jnp.log(l)

    groups = [range(g, min(g + 2, nsub)) for g in range(0, nsub, 2)]
    for sb in groups[0]:
        stage_a(sb)
    for prev, nxt in zip(groups[:-1], groups[1:]):
        for sb in prev:
            stage_b(sb)
        for sb in nxt:
            stage_a(sb)
    for sb in groups[-1]:
        stage_b(sb)
    o_ref[...] = ot_scr[...].T.astype(o_ref.dtype)
    lse_ref[...] = lt_scr[...].T


def _band_attention(q, k, v, b, dil):
    rows, width = q.shape
    c = width // dil
    t = rows * dil
    n = t // (b * dil)
    qb = min(QB, n)
    assert n % qb == 0 and qb % BLK == 0
    rr, nj = qb // BLK, n // qb
    cur = lambda w: pl.BlockSpec((qb, w), lambda si, j: ((si // dil) * nj + j, si % dil))
    prev = pl.BlockSpec((BLK, c), lambda si, j: ((si // dil) * (n // BLK) + jnp.maximum(j * rr - 1, 0), si % dil))
    return pl.pallas_call(
        functools.partial(_band_kernel, qb=qb),
        grid=(t // n, nj),
        in_specs=[cur(c), prev, cur(c), prev, cur(c)],
        out_specs=[cur(c), cur(LANES)],
        out_shape=[jax.ShapeDtypeStruct((rows, dil * c), BF16), jax.ShapeDtypeStruct((rows, dil * LANES), F32)],
        scratch_shapes=[pltpu.VMEM((rr, HEADS_PER_DIL, 2 * BLK, BLK), F32),
                        pltpu.VMEM((rr, HEADS_PER_DIL, 1, BLK), F32),
                        pltpu.VMEM((c, qb), F32),
                        pltpu.VMEM((LANES, qb), F32)],
        compiler_params=_params("parallel", "parallel"),
        name="band_attention",
    )(q, k, k, v, v)


def _merge_kernel(*refs):
    ng = len(DIL_PATTERNS)
    o_refs, l_refs, out_ref, o_scr, l_scr = refs[:ng], refs[ng:2 * ng], refs[2 * ng], refs[2 * ng + 1], refs[2 * ng + 2]
    tm, gw = out_ref.shape
    os_, lses = [], []
    for g, (_, dil) in enumerate(DIL_PATTERNS):
        if dil == 1:
            os_.append([o_refs[g][:, cb * LANES:(cb + 1) * LANES].astype(F32) for cb in range(gw // LANES)])
            lses.append(l_refs[g][...])
            continue
        for r in range(dil):
            rows = pl.ds(r, tm // dil, stride=dil)
            for cb in range(gw // LANES):
                o_scr[g, cb, rows, :] = o_refs[g][:, r * gw + cb * LANES:r * gw + (cb + 1) * LANES].astype(F32)
            l_scr[g, rows, :] = l_refs[g][:, r * LANES:(r + 1) * LANES]
        os_.append([o_scr[g, cb] for cb in range(gw // LANES)])
        lses.append(l_scr[g])
    m = functools.reduce(jnp.maximum, lses)
    es = [jnp.exp(l - m) for l in lses]
    den = sum(es)
    spread = jnp.where(lax.broadcasted_iota(I32, (LANES, gw), 1) // HEAD_DIM
                       == lax.broadcasted_iota(I32, (LANES, gw), 0), 1.0, 0.0).astype(BF16)
    ws = [_dot((e / den).astype(BF16), spread) for e in es]
    for cb in range(gw // LANES):
        cs = slice(cb * LANES, (cb + 1) * LANES)
        out_ref[:, cs] = sum(w[:, cs] * o[cb] for w, o in zip(ws, os_)).astype(out_ref.dtype)


def _merge_groups(os_, lses, t):
    gw = HEADS_PER_DIL * HEAD_DIM
    tm = min(1024, t)
    ng = len(DIL_PATTERNS)
    spec = lambda w: [pl.BlockSpec((tm // dil, dil * w), lambda i: (i, 0)) for _, dil in DIL_PATTERNS]
    return pl.pallas_call(
        _merge_kernel,
        grid=(t // tm,),
        in_specs=spec(gw) + spec(LANES),
        out_specs=pl.BlockSpec((tm, gw), lambda i: (i, 0)),
        out_shape=jax.ShapeDtypeStruct((t, gw), BF16),
        scratch_shapes=[pltpu.VMEM((ng, gw // LANES, tm, LANES), F32), pltpu.VMEM((ng, tm, LANES), F32)],
        compiler_params=_params("parallel"),
        name="merge_groups",
    )(*os_, *lses)


def _ffn_kernel(x_ref, mix_ref, mo_ref, wo1_ref, wo2_ref, g_ref, wgu_ref, wd_ref, gf_ref,
                o_ref, act_scr, *, final_norm, tf, mix_token_major):
    dff = wd_ref.shape[0]
    mixed = _dot(mix_ref[...], wo1_ref[...]) if mix_token_major else _dot_tn(mix_ref[...], wo1_ref[...])
    x2 = x_ref[...] + mixed + _dot_tn(mo_ref[...], wo2_ref[...])
    h = _rms(x2, g_ref[...]).astype(BF16)
    for f in range(dff // tf):
        gate = _dot(h, wgu_ref[:, f * tf:(f + 1) * tf])
        up = _dot(h, wgu_ref[:, dff + f * tf:dff + (f + 1) * tf])
        act_scr[:, f * tf:(f + 1) * tf] = (gate * jax.nn.sigmoid(gate) * up).astype(BF16)
    y = x2 + _dot(act_scr[...], wd_ref[...])
    if final_norm:
        y = _rms(y, gf_ref[...])
    o_ref[...] = y


def _out_ffn(x, mix, mo, w_out, g_ffn, w_gate_up, w_down, g_final, final_norm):
    t, d = x.shape
    mix_token_major = mix.shape[0] == t
    cm, cmo = mix.shape[1 if mix_token_major else 0], mo.shape[0]
    dff = w_down.shape[0]
    wo1 = w_out[:cm].astype(BF16)
    wo2 = w_out[cm:].astype(BF16)
    wgu = w_gate_up.astype(BF16)
    wd = w_down.astype(BF16)
    tm = min(512, t)
    tf = 256 if dff % 256 == 0 else dff
    row = lambda c: pl.BlockSpec((tm, c), lambda i: (i, 0))
    const = lambda r, c: pl.BlockSpec((r, c), lambda i: (0, 0), pipeline_mode=pl.Buffered(1))
    return pl.pallas_call(
        functools.partial(_ffn_kernel, final_norm=final_norm, tf=tf, mix_token_major=mix_token_major),
        grid=(t // tm,),
        in_specs=[row(d), row(cm) if mix_token_major else pl.BlockSpec((cm, tm), lambda i: (0, i)),
                  pl.BlockSpec((cmo, tm), lambda i: (0, i)),
                  const(cm, d), const(cmo, d), const(1, d),
                  const(d, 2 * dff), const(dff, d), const(1, d)],
        out_specs=row(d),
        out_shape=jax.ShapeDtypeStruct((t, d), F32),
        scratch_shapes=[pltpu.VMEM((tm, dff), BF16)],
        compiler_params=_params("parallel"),
        name="out_ffn",
    )(x, mix, mo, wo1, wo2, g_ffn.reshape(1, d), wgu, wd, g_final.reshape(1, d))


def kernel(x, mem, positions,
           l0_norm_mix, l0_norm_mem, l0_w_in, l0_w_mem_kv, l0_w_out, l0_norm_ffn, l0_w_gate_up, l0_w_down,
           l1_norm_mix, l1_norm_mem, l1_w_in, l1_w_mem_kv, l1_w_out, l1_norm_ffn, l1_w_gate_up, l1_w_down,
           final_norm):
    b, s, d = x.shape
    t = b * s
    xt = x.reshape(t, d)
    tabs = _trig_tables(positions.reshape(1, t))

    qt, k, vat, qit, ki, wit, qmt, kn = _inproj_a(xt, l0_norm_mix, l0_w_in, tabs)
    mix = _dsa_attention(qt, qit, wit, ki, k, vat, kn, b, s)
    mo = _mem_attention(qmt, *_mem_kv(mem, l0_norm_mem, l0_w_mem_kv), s)
    xt = _out_ffn(xt, mix, mo, l0_w_out, l0_norm_ffn, l0_w_gate_up, l0_w_down, final_norm, False)

    outs = _inproj_b(xt, l1_norm_mix, l1_w_in, tabs)
    os_, lses = [], []
    for g, (window, dil) in enumerate(DIL_PATTERNS):
        assert window // dil == BLK
        o, lse = _band_attention(*outs[3 * g:3 * g + 3], b, dil)
        os_.append(o)
        lses.append(lse)
    mix = _merge_groups(os_, lses, t)
    mo = _mem_attention(outs[-1], *_mem_kv(mem, l1_norm_mem, l1_w_mem_kv), s)
    xt = _out_ffn(xt, mix, mo, l1_w_out, l1_norm_ffn, l1_w_gate_up, l1_w_down, final_norm, True)
    return xt.reshape(b, s, d)
```

```python
import functools

import jax
import jax.numpy as jnp
from jax import lax
from jax.experimental import pallas as pl
from jax.experimental.pallas import tpu as pltpu

F32 = jnp.float32
BF16 = jnp.bfloat16
I32 = jnp.int32

HEAD_DIM = 64
N_HEADS_A = 12
N_KV_A = 4
IDX_HEADS = 8
IDX_DIM = 64
IDX_ROPE_DIM = 32
TOPK_MAX = 256
DIL_PATTERNS = ((128, 1), (512, 4), (2048, 16))
HEADS_PER_DIL = 4
N_MEM_HEADS = 4
BLK = 128
ROPE_THETA = 10000.0
EPS = 1e-6
NEG = -1e30
INT_MIN = -2147483648

LANES = 128
VMEM_LIMIT = 56 * 1024 * 1024

Q_SCALE = HEAD_DIM ** -0.5
WI_SCALE = IDX_HEADS ** -0.5 * IDX_DIM ** -0.5
LOG2E = 1.4426950408889634


def _dot(a, b):
    return jnp.dot(a, b, preferred_element_type=F32)


def _dot_nt(a, b):
    return lax.dot_general(a, b, (((1,), (1,)), ((), ())), preferred_element_type=F32)


def _dot_tn(a, b):
    return lax.dot_general(a, b, (((0,), (0,)), ((), ())), preferred_element_type=F32)


def _params(*sem):
    return pltpu.CompilerParams(dimension_semantics=sem, vmem_limit_bytes=VMEM_LIMIT)


def _rms(x, g):
    ms = jnp.mean(x * x, axis=-1, keepdims=True)
    return x * lax.rsqrt(ms + EPS) * g


H_HD = HEAD_DIM // 2
H_IX = IDX_ROPE_DIM // 2


def _trig_kernel(pos_ref, f_ref, chd_ref, shd_ref, cix_ref, six_ref):
    tm = pos_ref.shape[1]
    pos = pos_ref[...].astype(F32)
    f = jnp.concatenate([f_ref[...]] * (tm // LANES), axis=1)
    ang = f * pos
    c, s = jnp.cos(ang), jnp.sin(ang)
    chd_ref[...] = c[:H_HD]
    shd_ref[...] = s[:H_HD]
    cix_ref[...] = c[H_HD:]
    six_ref[...] = s[H_HD:]


def _trig_tables(pos_row):
    t = pos_row.shape[1]
    tm = min(2048, t)
    f_hd = ROPE_THETA ** (-jnp.arange(H_HD, dtype=F32) / H_HD)
    f_ix = ROPE_THETA ** (-jnp.arange(H_IX, dtype=F32) / H_IX)
    f = jnp.broadcast_to(jnp.concatenate([f_hd, f_ix])[:, None], (H_HD + H_IX, LANES))
    spec = lambda r: pl.BlockSpec((r, tm), lambda i: (0, i))
    rows = [H_HD, H_HD, H_IX, H_IX]
    return pl.pallas_call(
        _trig_kernel,
        grid=(t // tm,),
        in_specs=[spec(1), pl.BlockSpec((H_HD + H_IX, LANES), lambda i: (0, 0))],
        out_specs=[spec(r) for r in rows],
        out_shape=[jax.ShapeDtypeStruct((r, t), F32) for r in rows],
        compiler_params=_params("parallel"),
        name="rope_tables",
    )(pos_row, f)


def _norm_matmul_kernel(x_ref, g_ref, w_ref, o_ref):
    h = _rms(x_ref[...], g_ref[...]).astype(BF16)
    o_ref[...] = _dot(h, w_ref[...]).astype(o_ref.dtype)


def _norm_matmul(x, g, w, out_dtype):
    t, d = x.shape
    n = w.shape[1]
    tm = min(512, t)
    return pl.pallas_call(
        _norm_matmul_kernel,
        grid=(t // tm,),
        in_specs=[pl.BlockSpec((tm, d), lambda i: (i, 0)),
                  pl.BlockSpec((1, d), lambda i: (0, 0)),
                  pl.BlockSpec((d, n), lambda i: (0, 0))],
        out_specs=pl.BlockSpec((tm, n), lambda i: (i, 0)),
        out_shape=jax.ShapeDtypeStruct((t, n), out_dtype),
        compiler_params=_params("parallel"),
        name="norm_matmul",
    )(x, g.reshape(1, d), w)


def _mem_kv(mem, g, w_kv):
    b, m, d = mem.shape
    kv = _norm_matmul(mem.reshape(b * m, d), g, w_kv.astype(BF16), BF16).reshape(b, m, -1)
    nk = N_MEM_HEADS * HEAD_DIM
    return kv[:, :, :nk], kv[:, :, nk:].transpose(0, 2, 1)


VR = 80

A_Q, A_K, A_V, A_QI, A_KI, A_WI, A_QM, A_END = 0, 768, 1024, 1280, 1792, 1856, 1872, 2128


def _rope_heads(p, nheads, out_ref, half, c, s, scale):
    for hh in range(nheads):
        r0 = hh * HEAD_DIM
        x1, x2 = p[r0:r0 + half], p[r0 + half:r0 + 2 * half]
        out_ref[r0:r0 + half, :] = ((x1 * c - x2 * s) * scale).astype(out_ref.dtype)
        out_ref[r0 + half:r0 + 2 * half, :] = ((x2 * c + x1 * s) * scale).astype(out_ref.dtype)
        if 2 * half < HEAD_DIM:
            out_ref[r0 + 2 * half:r0 + HEAD_DIM, :] = (p[r0 + 2 * half:r0 + HEAD_DIM] * scale).astype(out_ref.dtype)


def _write_values(pv, nheads, va_ref):
    tm = pv.shape[1]
    ones_rows = jnp.where(lax.broadcasted_iota(I32, (VR - HEAD_DIM, tm), 0) == 0, 1.0, 0.0).astype(BF16)
    for g in range(nheads):
        va_ref[g * VR:g * VR + HEAD_DIM, :] = pv[g * HEAD_DIM:(g + 1) * HEAD_DIM].astype(BF16)
        va_ref[g * VR + HEAD_DIM:(g + 1) * VR, :] = ones_rows


def _inproj_a_kernel(x_ref, g_ref, wt_ref, chd_ref, shd_ref, cix_ref, six_ref,
                     q_ref, k_ref, va_ref, qi_ref, ki_ref, wi_ref, qm_ref, kn_ref, kt_scr):
    h = _rms(x_ref[...], g_ref[...]).astype(BF16)
    chd, shd = chd_ref[...], shd_ref[...]
    cix, six = cix_ref[...], six_ref[...]

    def proj(a, b):
        return _dot_nt(wt_ref[a:b, :], h)

    _rope_heads(proj(A_Q, A_K), N_HEADS_A, q_ref, H_HD, chd, shd, Q_SCALE * LOG2E)
    nk = N_KV_A * HEAD_DIM
    _rope_heads(proj(A_K, A_V), N_KV_A, kt_scr.at[0:nk], H_HD, chd, shd, 1.0)
    _write_values(proj(A_V, A_QI), N_KV_A, va_ref)
    _rope_heads(proj(A_QI, A_KI), IDX_HEADS, qi_ref, H_IX, cix, six, 1.0)
    pkw = proj(A_KI, A_QM)
    _rope_heads(pkw, 1, kt_scr.at[nk:nk + IDX_DIM], H_IX, cix, six, 1.0)
    k_ref[...] = kt_scr[0:nk, :].T.astype(BF16)
    ki_ref[...] = kt_scr[nk:nk + IDX_DIM, :].T.astype(BF16)
    for g in range(N_KV_A):
        kg = kt_scr[g * HEAD_DIM:(g + 1) * HEAD_DIM, :]
        kn_ref[g:g + 1, :] = jnp.sum(kg * kg, axis=0, keepdims=True)
    wi_ref[...] = pkw[IDX_DIM:IDX_DIM + IDX_HEADS] * WI_SCALE
    qm_ref[...] = (proj(A_QM, A_END) * Q_SCALE).astype(BF16)


def _inproj_a(x, g, w_in, tabs):
    t, d = x.shape
    wt = w_in.T
    pad = jnp.zeros((A_QM - A_WI - IDX_HEADS, d), w_in.dtype)
    split = A_WI + IDX_HEADS
    wt = jnp.concatenate([wt[:split], pad, wt[split:]], axis=0).astype(BF16)
    tm = min(512, t)
    col = lambda r: pl.BlockSpec((r, tm), lambda i: (0, i))
    outs = [(N_HEADS_A * HEAD_DIM, BF16), (N_KV_A * HEAD_DIM, BF16), (N_KV_A * VR, BF16),
            (IDX_HEADS * IDX_DIM, BF16), (IDX_DIM, BF16), (IDX_HEADS, F32), (N_MEM_HEADS * HEAD_DIM, BF16),
            (N_KV_A, F32)]
    return pl.pallas_call(
        _inproj_a_kernel,
        grid=(t // tm,),
        in_specs=[pl.BlockSpec((tm, d), lambda i: (i, 0)), pl.BlockSpec((1, d), lambda i: (0, 0)),
                  pl.BlockSpec((A_END, d), lambda i: (0, 0)),
                  col(H_HD), col(H_HD), col(H_IX), col(H_IX)],
        out_specs=[pl.BlockSpec((tm, r), lambda i: (i, 0)) if k in (1, 4) else col(r)
                   for k, (r, _) in enumerate(outs)],
        out_shape=[jax.ShapeDtypeStruct((t, r) if k in (1, 4) else (r, t), dt) for k, (r, dt) in enumerate(outs)],
        scratch_shapes=[pltpu.VMEM((N_KV_A * HEAD_DIM + IDX_DIM, tm), F32)],
        compiler_params=_params("parallel"),
        name="inproj_a",
    )(x, g.reshape(1, d), wt, *tabs)


def _inproj_b_kernel(x_ref, g_ref, wt_ref, chd_ref, shd_ref, *refs):
    ng = len(DIL_PATTERNS)
    out_refs, (rope_scr, tok_scr) = refs[:3 * ng + 1], refs[3 * ng + 1:]
    h = _rms(x_ref[...], g_ref[...]).astype(BF16)
    chd, shd = chd_ref[...], shd_ref[...]
    gw = HEADS_PER_DIL * HEAD_DIM
    tm = h.shape[0]

    def emit(out_ref, slot, dil, value_t):
        tok = value_t.T
        if dil == 1:
            out_ref[...] = tok.astype(BF16)
        else:
            for cb in range(gw // LANES):
                tok_scr[slot, cb] = tok[:, cb * LANES:(cb + 1) * LANES]
            for r in range(dil):
                for cb in range(gw // LANES):
                    out_ref[:, r * gw + cb * LANES:r * gw + (cb + 1) * LANES] = (
                        tok_scr[slot, cb, pl.ds(r, tm // dil, stride=dil), :].astype(BF16))

    for g, (_, dil) in enumerate(DIL_PATTERNS):
        q_ref, k_ref, v_ref = out_refs[3 * g:3 * g + 3]
        base = 3 * g * gw
        _rope_heads(_dot_nt(wt_ref[base:base + gw, :], h), HEADS_PER_DIL, rope_scr.at[0], H_HD, chd, shd,
                    Q_SCALE * LOG2E)
        emit(q_ref, 0, dil, rope_scr[0])
        _rope_heads(_dot_nt(wt_ref[base + gw:base + 2 * gw, :], h), HEADS_PER_DIL, rope_scr.at[1], H_HD, chd, shd, 1.0)
        emit(k_ref, 1, dil, rope_scr[1])
        emit(v_ref, 2, dil, _dot_nt(wt_ref[base + 2 * gw:base + 3 * gw, :], h))
    out_refs[3 * ng][...] = (_dot_nt(wt_ref[3 * ng * gw:3 * ng * gw + N_MEM_HEADS * HEAD_DIM, :], h)
                            * Q_SCALE).astype(BF16)


def _inproj_b(x, g, w_in, tabs):
    t, d = x.shape
    wt = w_in.T.astype(BF16)
    tm = min(512, t)
    col = lambda r: pl.BlockSpec((r, tm), lambda i: (0, i))
    gw = HEADS_PER_DIL * HEAD_DIM
    sub_specs, sub_shapes = [], []
    for _, dil in DIL_PATTERNS:
        assert tm % (16 * dil) == 0
        sub_specs += [pl.BlockSpec((tm // dil, dil * gw), lambda i: (i, 0))] * 3
        sub_shapes += [jax.ShapeDtypeStruct((t // dil, dil * gw), BF16)] * 3
    return pl.pallas_call(
        _inproj_b_kernel,
        grid=(t // tm,),
        in_specs=[pl.BlockSpec((tm, d), lambda i: (i, 0)), pl.BlockSpec((1, d), lambda i: (0, 0)),
                  pl.BlockSpec(wt.shape, lambda i: (0, 0)), col(H_HD), col(H_HD)],
        out_specs=sub_specs + [col(N_MEM_HEADS * HEAD_DIM)],
        out_shape=sub_shapes + [jax.ShapeDtypeStruct((N_MEM_HEADS * HEAD_DIM, t), BF16)],
        scratch_shapes=[pltpu.VMEM((2, gw, tm), F32), pltpu.VMEM((3, gw // LANES, tm, LANES), F32)],
        compiler_params=_params("parallel"),
        name="inproj_b",
    )(x, g.reshape(1, d), wt, tabs[0], tabs[1])


TQ = 256
CR = 32
COARSE_BITS = 16
FINE_BITS = 17
BOUND_SLACK = 1.01
UNDERFLOW_GUARD = 2.0 ** -100


def _key_to_f32(key):
    bits = jnp.where(key < 0, key ^ jnp.int32(0x7FFFFFFF), key)
    return pltpu.bitcast(bits, F32)


def _dsa_kernel(q_ref, qi_ref, wi_ref, ki_ref, k_ref, va_ref, kn_ref, o_ref,
                sc_scr, sc16_scr, tau_scr, need_scr, tie_scr, tri_scr, bound_scr, m_scr, alpha_scr, acc_scr, s0_scr, s1_scr, p0_scr, p1_scr,
                *, seq, n_sel):
    i = pl.program_id(1)
    nch = i + 1
    krow = lax.broadcasted_iota(I32, (TQ, TQ), 0)
    qcol = lax.broadcasted_iota(I32, (TQ, TQ), 1)

    def chunk_off(c):
        return pl.multiple_of(c * TQ, TQ)

    def score_chunk(c, diag):
        off = chunk_off(c)
        kic = ki_ref[pl.ds(off, TQ), :]
        sc = jnp.zeros((TQ, TQ), F32)
        for h in range(IDX_HEADS):
            lg = _dot(kic, qi_ref[h * IDX_DIM:(h + 1) * IDX_DIM, :])
            sc = sc + jnp.maximum(lg, 0.0) * wi_ref[h:h + 1, :]
        if diag:
            sc = jnp.where(krow > qcol, -jnp.inf, sc)
        sc_scr[pl.ds(off, TQ), :] = sc
        sc16_scr[pl.ds(off, TQ), :] = sc.astype(BF16)

    def score_pair(t, carry):
        score_chunk(2 * t, False)
        score_chunk(2 * t + 1, False)
        return carry

    lax.fori_loop(0, i // 2, score_pair, 0)

    @pl.when(i % 2 == 1)
    def _():
        score_chunk(i - 1, False)

    score_chunk(i, True)

    npair = (nch + 1) // 2

    @pl.when(nch % 2 == 1)
    def _():
        sc_scr[pl.ds(chunk_off(nch), TQ), :] = jnp.full((TQ, TQ), -jnp.inf, F32)
        sc16_scr[pl.ds(chunk_off(nch), TQ), :] = jnp.full((TQ, TQ), -jnp.inf, BF16)

    def count(pred):
        def body(t, acc):
            for c in (2 * t, 2 * t + 1):
                off = chunk_off(c)
                ind = pred(sc_scr[pl.ds(off, TQ), :], off)
                acc = acc + jnp.sum(ind.reshape(TQ // CR, CR, TQ), axis=0)
            return acc
        acc = lax.fori_loop(0, npair, body, jnp.zeros((CR, TQ), F32))
        return jnp.sum(acc, axis=0, keepdims=True)

    @pl.when(i * TQ < n_sel)
    def _():
        tau_scr[...] = jnp.full((1, TQ), -jnp.inf, F32)
        need_scr[...] = jnp.zeros((1, TQ), F32)

    @pl.when(i * TQ >= n_sel)
    def _():
        def body16(step, tau16):
            cand = tau16 + jnp.left_shift(jnp.int32(1), COARSE_BITS - 1 - step)
            bits = jnp.where(cand < 0, cand ^ jnp.int32(0x7FFF), cand)
            cand_f = pltpu.bitcast(jnp.left_shift(bits, COARSE_BITS), F32).astype(BF16)

            def chunk_pair(t, acc):
                for c in (2 * t, 2 * t + 1):
                    ind = jnp.where(sc16_scr[pl.ds(chunk_off(c), TQ), :] >= cand_f,
                                    jnp.bfloat16(1.0), jnp.bfloat16(0.0))
                    acc = functools.reduce(lambda a, k: a + ind[k * CR:(k + 1) * CR], range(TQ // CR), acc)
                return acc
            acc = lax.fori_loop(0, npair, chunk_pair, jnp.zeros((CR, TQ), BF16))
            ok = jnp.sum(acc.astype(F32), axis=0, keepdims=True) >= float(n_sel)
            return jnp.where(ok, cand, tau16)

        tau16 = lax.fori_loop(0, COARSE_BITS, body16, jnp.full((1, TQ), -(1 << (COARSE_BITS - 1)), I32))

        key16 = jnp.left_shift(tau16, COARSE_BITS) + jnp.where(tau16 < 0, (1 << COARSE_BITS) - 1, 0)
        lo = key16 - ((1 << (COARSE_BITS - 1)) + 1)

        def body(step, carry):
            delta, cge, crej = carry
            cand = delta + jnp.left_shift(jnp.int32(1), FINE_BITS - 1 - step)
            cand_f = _key_to_f32(lo + cand)
            cnt = count(lambda blk, _: jnp.where(blk >= cand_f, 1.0, 0.0))
            ok = cnt >= float(n_sel)
            return jnp.where(ok, cand, delta), jnp.where(ok, cnt, cge), jnp.where(ok, crej, cnt)

        init = (jnp.zeros((1, TQ), I32), jnp.full((1, TQ), 2.0 * n_sel, F32), jnp.zeros((1, TQ), F32))
        delta, cge, crej = lax.fori_loop(0, FINE_BITS, body, init)
        tau = lo + delta
        tau_scr[...] = _key_to_f32(tau)
        need_scr[...] = jnp.where(cge > float(n_sel), float(n_sel) - crej, 2.0 * seq)

    tri_scr[...] = jnp.where(krow >= qcol, 1.0, 0.0).astype(BF16)
    grp = N_HEADS_A // N_KV_A
    tau_f = tau_scr[...]
    need = need_scr[...]

    def chunk_select(c):
        blk = sc_scr[pl.ds(chunk_off(c), TQ), :]
        eq = blk == tau_f
        rank = _dot(tri_scr[...], jnp.where(eq, 1.0, 0.0).astype(BF16)) + tie_scr[...]
        tie_scr[...] = rank[TQ - 1:TQ, :]
        return jnp.where(eq, jnp.where(rank <= need, 1.0, 0.0), jnp.where(blk > tau_f, 1.0, 0.0))

    kmax2 = jnp.max(kn_ref[...], axis=1, keepdims=True)
    for h in range(N_HEADS_A):
        qf = q_ref[h * HEAD_DIM:(h + 1) * HEAD_DIM, :].astype(F32)
        qn2 = jnp.sum(qf * qf, axis=0, keepdims=True)
        bound_scr[h] = jnp.sqrt(qn2 * kmax2[h // grp:h // grp + 1]) * BOUND_SLACK
    acc_scr[...] = jnp.zeros(acc_scr.shape, F32)
    tie_scr[...] = jnp.zeros((1, TQ), F32)

    def pipeline(stage_a, stage_b, buf0, buf1):
        def step(c, src, dst):
            stage_b(c - 1, src)
            stage_a(c, dst)

        stage_a(0, buf0)

        def pair_body(t, carry):
            step(2 * t + 1, buf0, buf1)
            step(2 * t + 2, buf1, buf0)
            return carry

        lax.fori_loop(0, (nch - 1) // 2, pair_body, 0)

        @pl.when((nch - 1) % 2 == 1)
        def _():
            step(nch - 1, buf0, buf1)
            stage_b(nch - 1, buf1)

        @pl.when((nch - 1) % 2 == 0)
        def _():
            stage_b(nch - 1, buf0)

    def fast_a(c, p_dst):
        off = chunk_off(c)
        sel = chunk_select(c).astype(BF16)
        for g in range(N_KV_A):
            kc = k_ref[pl.ds(off, TQ), g * HEAD_DIM:(g + 1) * HEAD_DIM]
            for j in range(grp):
                h = g * grp + j
                s = _dot(kc, q_ref[h * HEAD_DIM:(h + 1) * HEAD_DIM, :])
                p_dst[h] = jnp.exp2(s - bound_scr[h]).astype(BF16) * sel

    def fast_b(c, p_src):
        off = chunk_off(c)
        for g in range(N_KV_A):
            vt = va_ref[g * VR:(g + 1) * VR, pl.ds(off, TQ)]
            for j in range(grp):
                h = g * grp + j
                acc_scr[h] += _dot(vt, p_src[h])

    pipeline(fast_a, fast_b, p0_scr, p1_scr)
    lmin = functools.reduce(jnp.minimum, [acc_scr[h, HEAD_DIM:HEAD_DIM + 1, :] for h in range(N_HEADS_A)])

    def stage_a(c, s_dst):
        off = chunk_off(c)
        bias = (chunk_select(c) - 1.0) * (-NEG)
        for g in range(N_KV_A):
            kc = k_ref[pl.ds(off, TQ), g * HEAD_DIM:(g + 1) * HEAD_DIM]
            for j in range(grp):
                h = g * grp + j
                s = _dot(kc, q_ref[h * HEAD_DIM:(h + 1) * HEAD_DIM, :]) + bias
                s_dst[h] = s
                m_old = m_scr[h]
                m_new = jnp.maximum(m_old, jnp.max(s, axis=0, keepdims=True))
                alpha_scr[h] = jnp.exp2(m_old - m_new)
                m_scr[h] = m_new

    def stage_b(c, s_src):
        off = chunk_off(c)
        for g in range(N_KV_A):
            vt = va_ref[g * VR:(g + 1) * VR, pl.ds(off, TQ)]
            for j in range(grp):
                h = g * grp + j
                p = jnp.exp2(s_src[h] - m_scr[h]).astype(BF16)
                acc_scr[h] = alpha_scr[h] * acc_scr[h] + _dot(vt, p)

    @pl.when(jnp.logical_not(jnp.min(lmin) > UNDERFLOW_GUARD))
    def _():
        m_scr[...] = jnp.full(m_scr.shape, NEG, F32)
        acc_scr[...] = jnp.zeros(acc_scr.shape, F32)
        tie_scr[...] = jnp.zeros((1, TQ), F32)
        pipeline(stage_a, stage_b, s0_scr, s1_scr)

    for h in range(N_HEADS_A):
        a = acc_scr[h]
        o_ref[h * HEAD_DIM:(h + 1) * HEAD_DIM, :] = (a[:HEAD_DIM] / a[HEAD_DIM:HEAD_DIM + 1]).astype(BF16)


def _dsa_attention(qt, qit, wit, ki, k, vat, kn, b, s):
    n_sel = min(TOPK_MAX, s // 4)
    assert s % TQ == 0 and n_sel % TQ == 0
    nq = s // TQ
    qblk = lambda r: pl.BlockSpec((r, TQ), lambda bi, i: (0, bi * nq + i))
    tok = lambda c: pl.BlockSpec((s, c), lambda bi, i: (bi, 0))
    return pl.pallas_call(
        functools.partial(_dsa_kernel, seq=s, n_sel=n_sel),
        grid=(b, nq),
        in_specs=[qblk(qt.shape[0]), qblk(qit.shape[0]), qblk(wit.shape[0]),
                  tok(ki.shape[1]), tok(k.shape[1]),
                  pl.BlockSpec((vat.shape[0], s), lambda bi, i: (0, bi)),
                  pl.BlockSpec((kn.shape[0], s), lambda bi, i: (0, bi))],
        out_specs=qblk(qt.shape[0]),
        out_shape=jax.ShapeDtypeStruct(qt.shape, BF16),
        scratch_shapes=[
            pltpu.VMEM((s + TQ, TQ), F32),
            pltpu.VMEM((s + TQ, TQ), BF16),
            pltpu.VMEM((1, TQ), F32),
            pltpu.VMEM((1, TQ), F32),
            pltpu.VMEM((1, TQ), F32),
            pltpu.VMEM((TQ, TQ), BF16),
            pltpu.VMEM((N_HEADS_A, 1, TQ), F32),
            pltpu.VMEM((N_HEADS_A, 1, TQ), F32),
            pltpu.VMEM((N_HEADS_A, 1, TQ), F32),
            pltpu.VMEM((N_HEADS_A, VR, TQ), F32),
            pltpu.VMEM((N_HEADS_A, TQ, TQ), F32),
            pltpu.VMEM((N_HEADS_A, TQ, TQ), F32),
            pltpu.VMEM((N_HEADS_A, TQ, TQ), BF16),
            pltpu.VMEM((N_HEADS_A, TQ, TQ), BF16),
        ],
        compiler_params=_params("parallel", "arbitrary"),
        name="dsa_attention",
    )(qt, qit, wit, ki, k, vat, kn)


def _mem_attn_kernel(q_ref, k_ref, v_ref, o_ref, s_scr, m_scr):
    nsub = q_ref.shape[1] // MEM_SUB

    def stage_a(sb):
        ts = slice(sb * MEM_SUB, (sb + 1) * MEM_SUB)
        for h in range(N_MEM_HEADS):
            sl = slice(h * HEAD_DIM, (h + 1) * HEAD_DIM)
            s = _dot(k_ref[:, sl], q_ref[sl, ts])
            s_scr[sb, h] = s
            m_scr[sb, h] = jnp.max(s, axis=0, keepdims=True)

    def stage_b(sb):
        ts = slice(sb * MEM_SUB, (sb + 1) * MEM_SUB)
        for h in range(N_MEM_HEADS):
            sl = slice(h * HEAD_DIM, (h + 1) * HEAD_DIM)
            p = jnp.exp(s_scr[sb, h] - m_scr[sb, h])
            l = jnp.sum(p, axis=0, keepdims=True)
            o_ref[sl, ts] = (_dot(v_ref[sl, :], p.astype(BF16)) / l).astype(BF16)

    stage_a(0)
    for sb in range(1, nsub):
        stage_b(sb - 1)
        stage_a(sb)
    stage_b(nsub - 1)


MEM_SUB = 512


def _mem_attention(qmt, km, vmt, s):
    c, t = qmt.shape
    m = km.shape[1]
    tm = min(1024, s)
    assert tm % MEM_SUB == 0
    nq = s // tm
    blk = pl.BlockSpec((c, tm), lambda bi, i: (0, bi * nq + i))
    return pl.pallas_call(
        _mem_attn_kernel,
        grid=(t // s, nq),
        in_specs=[blk,
                  pl.BlockSpec((None, m, km.shape[2]), lambda bi, i: (bi, 0, 0)),
                  pl.BlockSpec((None, vmt.shape[1], m), lambda bi, i: (bi, 0, 0))],
        out_specs=blk,
        out_shape=jax.ShapeDtypeStruct((c, t), BF16),
        scratch_shapes=[pltpu.VMEM((tm // MEM_SUB, N_MEM_HEADS, m, MEM_SUB), F32),
                        pltpu.VMEM((tm // MEM_SUB, N_MEM_HEADS, 1, MEM_SUB), F32)],
        compiler_params=_params("parallel", "parallel"),
        name="mem_attention",
    )(qmt, km, vmt)


QB = 512
BAND_QUERIES = 1024
LN2 = 0.6931471805599453


def _band_kernel(q_ref, kp_ref, kc_ref, vp_ref, vc_ref, o_ref, lse_ref, s_scr, m_scr, ot_scr, lt_scr, *, qb, ns):
    j = pl.program_id(1)
    nsub = qb // BLK
    krow = lax.broadcasted_iota(I32, (BLK, BLK), 0)
    qcol = lax.broadcasted_iota(I32, (BLK, BLK), 1)
    bias_prev = jnp.where(krow >= qcol, 0.0, NEG)
    bias_cur = jnp.where(krow <= qcol, 0.0, NEG)
    no_prev = jnp.where(j > 0, 0.0, NEG)
    lt_scr[...] = jnp.zeros(lt_scr.shape, F32)

    gw = HEADS_PER_DIL * HEAD_DIM

    def stage_a(u):
        sq, sb = divmod(u, nsub)
        qs = slice(sb * BLK, (sb + 1) * BLK)
        for h in range(HEADS_PER_DIL):
            hs = slice(sq * gw + h * HEAD_DIM, sq * gw + (h + 1) * HEAD_DIM)
            qh = q_ref[qs, hs]
            if sb == 0:
                s_p = _dot_nt(kp_ref[:, hs], qh) + (bias_prev + no_prev)
            else:
                s_p = _dot_nt(kc_ref[(sb - 1) * BLK:sb * BLK, hs], qh) + bias_prev
            s_c = _dot_nt(kc_ref[qs, hs], qh) + bias_cur
            s_scr[u, h, 0:BLK] = s_p
            s_scr[u, h, BLK:2 * BLK] = s_c
            m_scr[u, h] = jnp.maximum(jnp.max(s_p, axis=0, keepdims=True), jnp.max(s_c, axis=0, keepdims=True))

    def stage_b(u):
        sq, sb = divmod(u, nsub)
        qs = slice(sb * BLK, (sb + 1) * BLK)
        for h in range(HEADS_PER_DIL):
            hs = slice(sq * gw + h * HEAD_DIM, sq * gw + (h + 1) * HEAD_DIM)
            m = m_scr[u, h]
            p_p = jnp.exp2(s_scr[u, h, 0:BLK] - m)
            p_c = jnp.exp2(s_scr[u, h, BLK:2 * BLK] - m)
            l = jnp.sum(p_p, axis=0, keepdims=True) + jnp.sum(p_c, axis=0, keepdims=True)
            v_p = vp_ref[:, hs] if sb == 0 else vc_ref[(sb - 1) * BLK:sb * BLK, hs]
            acc = _dot_tn(v_p, p_p.astype(BF16)) + _dot_tn(vc_ref[qs, hs], p_c.astype(BF16))
            ot_scr[sq, h * HEAD_DIM:(h + 1) * HEAD_DIM, qs] = acc / l
            lt_scr[sq, h:h + 1, qs] = m * LN2 + jnp.log(l)

    units = ns * nsub
    groups = [range(g, min(g + 2, units)) for g in range(0, units, 2)]
    for u in groups[0]:
        stage_a(u)
    for prev, nxt in zip(groups[:-1], groups[1:]):
        for u in prev:
            stage_b(u)
        for u in nxt:
            stage_a(u)
    for u in groups[-1]:
        stage_b(u)
    for sq in range(ns):
        o_ref[:, sq * gw:(sq + 1) * gw] = ot_scr[sq].T.astype(o_ref.dtype)
        lse_ref[:, sq * LANES:(sq + 1) * LANES] = lt_scr[sq].T


def _band_attention(q, k, v, b, dil):
    rows, width = q.shape
    c = width // dil
    t = rows * dil
    n = t // (b * dil)
    qb = min(QB, n)
    assert n % qb == 0 and qb % BLK == 0
    rr, nj = qb // BLK, n // qb
    ns = min(dil, max(1, BAND_QUERIES // qb))
    assert dil % ns == 0
    gpb = dil // ns
    cur = lambda w: pl.BlockSpec((qb, ns * w), lambda sg, j: ((sg // gpb) * nj + j, sg % gpb))
    prev = pl.BlockSpec((BLK, ns * c),
                        lambda sg, j: ((sg // gpb) * (n // BLK) + jnp.maximum(j * rr - 1, 0), sg % gpb))
    return pl.pallas_call(
        functools.partial(_band_kernel, qb=qb, ns=ns),
        grid=(b * gpb, nj),
        in_specs=[cur(c), prev, cur(c), prev, cur(c)],
        out_specs=[cur(c), cur(LANES)],
        out_shape=[jax.ShapeDtypeStruct((rows, dil * c), BF16), jax.ShapeDtypeStruct((rows, dil * LANES), F32)],
        scratch_shapes=[pltpu.VMEM((ns * rr, HEADS_PER_DIL, 2 * BLK, BLK), F32),
                        pltpu.VMEM((ns * rr, HEADS_PER_DIL, 1, BLK), F32),
                        pltpu.VMEM((ns, c, qb), F32),
                        pltpu.VMEM((ns, LANES, qb), F32)],
        compiler_params=_params("parallel", "parallel"),
        name="band_attention",
    )(q, k, k, v, v)


def _merge_kernel(*refs):
    ng = len(DIL_PATTERNS)
    o_refs, l_refs, out_ref, o_scr, l_scr = refs[:ng], refs[ng:2 * ng], refs[2 * ng], refs[2 * ng + 1], refs[2 * ng + 2]
    tm, gw = out_ref.shape
    os_, lses = [], []
    for g, (_, dil) in enumerate(DIL_PATTERNS):
        if dil == 1:
            os_.append([o_refs[g][:, cb * LANES:(cb + 1) * LANES].astype(F32) for cb in range(gw // LANES)])
            lses.append(l_refs[g][...])
            continue
        for r in range(dil):
            rows = pl.ds(r, tm // dil, stride=dil)
            for cb in range(gw // LANES):
                o_scr[g, cb, rows, :] = o_refs[g][:, r * gw + cb * LANES:r * gw + (cb + 1) * LANES].astype(F32)
            l_scr[g, rows, :] = l_refs[g][:, r * LANES:(r + 1) * LANES]
        os_.append([o_scr[g, cb] for cb in range(gw // LANES)])
        lses.append(l_scr[g])
    m = functools.reduce(jnp.maximum, lses)
    es = [jnp.exp(l - m) for l in lses]
    den = sum(es)
    spread = jnp.where(lax.broadcasted_iota(I32, (LANES, gw), 1) // HEAD_DIM
                       == lax.broadcasted_iota(I32, (LANES, gw), 0), 1.0, 0.0).astype(BF16)
    ws = [_dot((e / den).astype(BF16), spread) for e in es]
    for cb in range(gw // LANES):
        cs = slice(cb * LANES, (cb + 1) * LANES)
        out_ref[:, cs] = sum(w[:, cs] * o[cb] for w, o in zip(ws, os_)).astype(out_ref.dtype)


def _merge_groups(os_, lses, t):
    gw = HEADS_PER_DIL * HEAD_DIM
    tm = min(1024, t)
    ng = len(DIL_PATTERNS)
    spec = lambda w: [pl.BlockSpec((tm // dil, dil * w), lambda i: (i, 0)) for _, dil in DIL_PATTERNS]
    return pl.pallas_call(
        _merge_kernel,
        grid=(t // tm,),
        in_specs=spec(gw) + spec(LANES),
        out_specs=pl.BlockSpec((tm, gw), lambda i: (i, 0)),
        out_shape=jax.ShapeDtypeStruct((t, gw), BF16),
        scratch_shapes=[pltpu.VMEM((ng, gw // LANES, tm, LANES), F32), pltpu.VMEM((ng, tm, LANES), F32)],
        compiler_params=_params("parallel"),
        name="merge_groups",
    )(*os_, *lses)


def _ffn_kernel(x_ref, mix_ref, mo_ref, wo1_ref, wo2_ref, g_ref, wgu_ref, wd_ref, gf_ref,
                o_ref, act_scr, *, final_norm, tf, mix_token_major):
    dff = wd_ref.shape[0]
    mixed = _dot(mix_ref[...], wo1_ref[...]) if mix_token_major else _dot_tn(mix_ref[...], wo1_ref[...])
    x2 = x_ref[...] + mixed + _dot_tn(mo_ref[...], wo2_ref[...])
    h = _rms(x2, g_ref[...]).astype(BF16)
    for f in range(dff // tf):
        gate = _dot(h, wgu_ref[:, f * tf:(f + 1) * tf])
        up = _dot(h, wgu_ref[:, dff + f * tf:dff + (f + 1) * tf])
        act_scr[:, f * tf:(f + 1) * tf] = (gate * jax.nn.sigmoid(gate) * up).astype(BF16)
    y = x2 + _dot(act_scr[...], wd_ref[...])
    if final_norm:
        y = _rms(y, gf_ref[...])
    o_ref[...] = y


def _out_ffn(x, mix, mo, w_out, g_ffn, w_gate_up, w_down, g_final, final_norm):
    t, d = x.shape
    mix_token_major = mix.shape[0] == t
    cm, cmo = mix.shape[1 if mix_token_major else 0], mo.shape[0]
    dff = w_down.shape[0]
    wo1 = w_out[:cm].astype(BF16)
    wo2 = w_out[cm:].astype(BF16)
    wgu = w_gate_up.astype(BF16)
    wd = w_down.astype(BF16)
    tm = min(512, t)
    tf = 256 if dff % 256 == 0 else dff
    row = lambda c: pl.BlockSpec((tm, c), lambda i: (i, 0))
    const = lambda r, c: pl.BlockSpec((r, c), lambda i: (0, 0), pipeline_mode=pl.Buffered(1))
    return pl.pallas_call(
        functools.partial(_ffn_kernel, final_norm=final_norm, tf=tf, mix_token_major=mix_token_major),
        grid=(t // tm,),
        in_specs=[row(d), row(cm) if mix_token_major else pl.BlockSpec((cm, tm), lambda i: (0, i)),
                  pl.BlockSpec((cmo, tm), lambda i: (0, i)),
                  const(cm, d), const(cmo, d), const(1, d),
                  const(d, 2 * dff), const(dff, d), const(1, d)],
        out_specs=row(d),
        out_shape=jax.ShapeDtypeStruct((t, d), F32),
        scratch_shapes=[pltpu.VMEM((tm, dff), BF16)],
        compiler_params=_params("parallel"),
        name="out_ffn",
    )(x, mix, mo, wo1, wo2, g_ffn.reshape(1, d), wgu, wd, g_final.reshape(1, d))


def kernel(x, mem, positions,
           l0_norm_mix, l0_norm_mem, l0_w_in, l0_w_mem_kv, l0_w_out, l0_norm_ffn, l0_w_gate_up, l0_w_down,
           l1_norm_mix, l1_norm_mem, l1_w_in, l1_w_mem_kv, l1_w_out, l1_norm_ffn, l1_w_gate_up, l1_w_down,
           final_norm):
    b, s, d = x.shape
    t = b * s
    xt = x.reshape(t, d)
    tabs = _trig_tables(positions.reshape(1, t))

    qt, k, vat, qit, ki, wit, qmt, kn = _inproj_a(xt, l0_norm_mix, l0_w_in, tabs)
    mix = _dsa_attention(qt, qit, wit, ki, k, vat, kn, b, s)
    mo = _mem_attention(qmt, *_mem_kv(mem, l0_norm_mem, l0_w_mem_kv), s)
    xt = _out_ffn(xt, mix, mo, l0_w_out, l0_norm_ffn, l0_w_gate_up, l0_w_down, final_norm, False)

    outs = _inproj_b(xt, l1_norm_mix, l1_w_in, tabs)
    os_, lses = [], []
    for g, (window, dil) in enumerate(DIL_PATTERNS):
        assert window // dil == BLK
        o, lse = _band_attention(*outs[3 * g:3 * g + 3], b, dil)
        os_.append(o)
        lses.append(lse)
    mix = _merge_groups(os_, lses, t)
    mo = _mem_attention(outs[-1], *_mem_kv(mem, l1_norm_mem, l1_w_mem_kv), s)
    xt = _out_ffn(xt, mix, mo, l1_w_out, l1_norm_ffn, l1_w_gate_up, l1_w_down, final_norm, True)
    return xt.reshape(b, s, d)
```

```python
import functools

import jax
import jax.numpy as jnp
from jax import lax
from jax.experimental import pallas as pl
from jax.experimental.pallas import tpu as pltpu

F32 = jnp.float32
BF16 = jnp.bfloat16
I32 = jnp.int32

HEAD_DIM = 64
N_HEADS_A = 12
N_KV_A = 4
IDX_HEADS = 8
IDX_DIM = 64
IDX_ROPE_DIM = 32
TOPK_MAX = 256
DIL_PATTERNS = ((128, 1), (512, 4), (2048, 16))
HEADS_PER_DIL = 4
N_MEM_HEADS = 4
BLK = 128
ROPE_THETA = 10000.0
EPS = 1e-6
NEG = -1e30

LANES = 128
VMEM_LIMIT = 56 * 1024 * 1024

Q_SCALE = HEAD_DIM ** -0.5
WI_SCALE = IDX_HEADS ** -0.5 * IDX_DIM ** -0.5
LOG2E = 1.4426950408889634


def _dot(a, b):
    return jnp.dot(a, b, preferred_element_type=F32)


def _dot_nt(a, b):
    return lax.dot_general(a, b, (((1,), (1,)), ((), ())), preferred_element_type=F32)


def _dot_tn(a, b):
    return lax.dot_general(a, b, (((0,), (0,)), ((), ())), preferred_element_type=F32)


def _params(*sem):
    return pltpu.CompilerParams(dimension_semantics=sem, vmem_limit_bytes=VMEM_LIMIT)


def _rms(x, g):
    ms = jnp.mean(x * x, axis=-1, keepdims=True)
    return x * lax.rsqrt(ms + EPS) * g


H_HD = HEAD_DIM // 2
H_IX = IDX_ROPE_DIM // 2


def _trig_kernel(pos_ref, f_ref, chd_ref, shd_ref, cix_ref, six_ref):
    tm = pos_ref.shape[1]
    pos = pos_ref[...].astype(F32)
    f = jnp.concatenate([f_ref[...]] * (tm // LANES), axis=1)
    ang = f * pos
    c, s = jnp.cos(ang), jnp.sin(ang)
    chd_ref[...] = c[:H_HD]
    shd_ref[...] = s[:H_HD]
    cix_ref[...] = c[H_HD:]
    six_ref[...] = s[H_HD:]


def _trig_tables(pos_row):
    t = pos_row.shape[1]
    tm = min(2048, t)
    f_hd = ROPE_THETA ** (-jnp.arange(H_HD, dtype=F32) / H_HD)
    f_ix = ROPE_THETA ** (-jnp.arange(H_IX, dtype=F32) / H_IX)
    f = jnp.broadcast_to(jnp.concatenate([f_hd, f_ix])[:, None], (H_HD + H_IX, LANES))
    spec = lambda r: pl.BlockSpec((r, tm), lambda i: (0, i))
    rows = [H_HD, H_HD, H_IX, H_IX]
    return pl.pallas_call(
        _trig_kernel,
        grid=(t // tm,),
        in_specs=[spec(1), pl.BlockSpec((H_HD + H_IX, LANES), lambda i: (0, 0))],
        out_specs=[spec(r) for r in rows],
        out_shape=[jax.ShapeDtypeStruct((r, t), F32) for r in rows],
        compiler_params=_params("parallel"),
        name="rope_tables",
    )(pos_row, f)


def _norm_matmul_kernel(x_ref, g_ref, w_ref, o_ref):
    h = _rms(x_ref[...], g_ref[...]).astype(BF16)
    o_ref[...] = _dot(h, w_ref[...]).astype(o_ref.dtype)


def _norm_matmul(x, g, w, out_dtype):
    t, d = x.shape
    n = w.shape[1]
    tm = min(512, t)
    return pl.pallas_call(
        _norm_matmul_kernel,
        grid=(t // tm,),
        in_specs=[pl.BlockSpec((tm, d), lambda i: (i, 0)),
                  pl.BlockSpec((1, d), lambda i: (0, 0)),
                  pl.BlockSpec((d, n), lambda i: (0, 0))],
        out_specs=pl.BlockSpec((tm, n), lambda i: (i, 0)),
        out_shape=jax.ShapeDtypeStruct((t, n), out_dtype),
        compiler_params=_params("parallel"),
        name="norm_matmul",
    )(x, g.reshape(1, d), w)


def _mem_kv(mem, g, w_kv):
    b, m, d = mem.shape
    kv = _norm_matmul(mem.reshape(b * m, d), g, w_kv.astype(BF16), BF16).reshape(b, m, -1)
    nk = N_MEM_HEADS * HEAD_DIM
    return kv[:, :, :nk], kv[:, :, nk:].transpose(0, 2, 1)


VR = 80

A_Q, A_K, A_V, A_QI, A_KI, A_WI, A_QM, A_END = 0, 768, 1024, 1280, 1792, 1856, 1872, 2128


def _rope_heads(p, nheads, out_ref, half, c, s, scale):
    for hh in range(nheads):
        r0 = hh * HEAD_DIM
        x1, x2 = p[r0:r0 + half], p[r0 + half:r0 + 2 * half]
        out_ref[r0:r0 + half, :] = ((x1 * c - x2 * s) * scale).astype(out_ref.dtype)
        out_ref[r0 + half:r0 + 2 * half, :] = ((x2 * c + x1 * s) * scale).astype(out_ref.dtype)
        if 2 * half < HEAD_DIM:
            out_ref[r0 + 2 * half:r0 + HEAD_DIM, :] = (p[r0 + 2 * half:r0 + HEAD_DIM] * scale).astype(out_ref.dtype)


def _write_values(pv, nheads, va_ref):
    tm = pv.shape[1]
    ones_rows = jnp.where(lax.broadcasted_iota(I32, (VR - HEAD_DIM, tm), 0) == 0, 1.0, 0.0).astype(BF16)
    for g in range(nheads):
        va_ref[g * VR:g * VR + HEAD_DIM, :] = pv[g * HEAD_DIM:(g + 1) * HEAD_DIM].astype(BF16)
        va_ref[g * VR + HEAD_DIM:(g + 1) * VR, :] = ones_rows


def _inproj_a_kernel(x_ref, g_ref, wt_ref, chd_ref, shd_ref, cix_ref, six_ref,
                     q_ref, k_ref, va_ref, qi_ref, ki_ref, wi_ref, qm_ref, kn_ref, kt_scr):
    h = _rms(x_ref[...], g_ref[...]).astype(BF16)
    chd, shd = chd_ref[...], shd_ref[...]
    cix, six = cix_ref[...], six_ref[...]

    def proj(a, b):
        return _dot_nt(wt_ref[a:b, :], h)

    _rope_heads(proj(A_Q, A_K), N_HEADS_A, q_ref, H_HD, chd, shd, Q_SCALE * LOG2E)
    nk = N_KV_A * HEAD_DIM
    _rope_heads(proj(A_K, A_V), N_KV_A, kt_scr.at[0:nk], H_HD, chd, shd, 1.0)
    _write_values(proj(A_V, A_QI), N_KV_A, va_ref)
    _rope_heads(proj(A_QI, A_KI), IDX_HEADS, qi_ref, H_IX, cix, six, 1.0)
    pkw = proj(A_KI, A_QM)
    _rope_heads(pkw, 1, kt_scr.at[nk:nk + IDX_DIM], H_IX, cix, six, 1.0)
    k_ref[...] = kt_scr[0:nk, :].T.astype(BF16)
    ki_ref[...] = kt_scr[nk:nk + IDX_DIM, :].T.astype(BF16)
    for g in range(N_KV_A):
        kg = kt_scr[g * HEAD_DIM:(g + 1) * HEAD_DIM, :]
        kn_ref[g:g + 1, :] = jnp.sum(kg * kg, axis=0, keepdims=True)
    wi_ref[...] = pkw[IDX_DIM:IDX_DIM + IDX_HEADS] * WI_SCALE
    qm_ref[...] = (proj(A_QM, A_END) * Q_SCALE).astype(BF16)


def _inproj_a(x, g, w_in, tabs):
    t, d = x.shape
    wt = w_in.T
    pad = jnp.zeros((A_QM - A_WI - IDX_HEADS, d), w_in.dtype)
    split = A_WI + IDX_HEADS
    wt = jnp.concatenate([wt[:split], pad, wt[split:]], axis=0).astype(BF16)
    tm = min(1024, t)
    col = lambda r: pl.BlockSpec((r, tm), lambda i: (0, i))
    outs = [(N_HEADS_A * HEAD_DIM, BF16), (N_KV_A * HEAD_DIM, BF16), (N_KV_A * VR, BF16),
            (IDX_HEADS * IDX_DIM, BF16), (IDX_DIM, BF16), (IDX_HEADS, F32), (N_MEM_HEADS * HEAD_DIM, BF16),
            (N_KV_A, F32)]
    return pl.pallas_call(
        _inproj_a_kernel,
        grid=(t // tm,),
        in_specs=[pl.BlockSpec((tm, d), lambda i: (i, 0)), pl.BlockSpec((1, d), lambda i: (0, 0)),
                  pl.BlockSpec((A_END, d), lambda i: (0, 0)),
                  col(H_HD), col(H_HD), col(H_IX), col(H_IX)],
        out_specs=[pl.BlockSpec((tm, r), lambda i: (i, 0)) if k in (1, 4) else col(r)
                   for k, (r, _) in enumerate(outs)],
        out_shape=[jax.ShapeDtypeStruct((t, r) if k in (1, 4) else (r, t), dt) for k, (r, dt) in enumerate(outs)],
        scratch_shapes=[pltpu.VMEM((N_KV_A * HEAD_DIM + IDX_DIM, tm), F32)],
        compiler_params=_params("parallel"),
        name="inproj_a",
    )(x, g.reshape(1, d), wt, *tabs)


def _inproj_b_kernel(x_ref, g_ref, wt_ref, chd_ref, shd_ref, *refs):
    ng = len(DIL_PATTERNS)
    out_refs, (rope_scr, tok_scr) = refs[:3 * ng + 1], refs[3 * ng + 1:]
    h = _rms(x_ref[...], g_ref[...]).astype(BF16)
    chd, shd = chd_ref[...], shd_ref[...]
    gw = HEADS_PER_DIL * HEAD_DIM
    tm = h.shape[0]

    def emit(out_ref, slot, dil, value_t):
        tok = value_t.T
        if dil == 1:
            out_ref[...] = tok.astype(BF16)
        else:
            for cb in range(gw // LANES):
                tok_scr[slot, cb] = tok[:, cb * LANES:(cb + 1) * LANES]
            for r in range(dil):
                for cb in range(gw // LANES):
                    out_ref[:, r * gw + cb * LANES:r * gw + (cb + 1) * LANES] = (
                        tok_scr[slot, cb, pl.ds(r, tm // dil, stride=dil), :].astype(BF16))

    for g, (_, dil) in enumerate(DIL_PATTERNS):
        q_ref, k_ref, v_ref = out_refs[3 * g:3 * g + 3]
        base = 3 * g * gw
        _rope_heads(_dot_nt(wt_ref[base:base + gw, :], h), HEADS_PER_DIL, rope_scr.at[0], H_HD, chd, shd,
                    Q_SCALE * LOG2E)
        emit(q_ref, 0, dil, rope_scr[0])
        _rope_heads(_dot_nt(wt_ref[base + gw:base + 2 * gw, :], h), HEADS_PER_DIL, rope_scr.at[1], H_HD, chd, shd, 1.0)
        emit(k_ref, 1, dil, rope_scr[1])
        emit(v_ref, 2, dil, _dot_nt(wt_ref[base + 2 * gw:base + 3 * gw, :], h))
    out_refs[3 * ng][...] = (_dot_nt(wt_ref[3 * ng * gw:3 * ng * gw + N_MEM_HEADS * HEAD_DIM, :], h)
                            * Q_SCALE).astype(BF16)


def _inproj_b(x, g, w_in, tabs):
    t, d = x.shape
    wt = w_in.T.astype(BF16)
    tm = min(1024, t)
    col = lambda r: pl.BlockSpec((r, tm), lambda i: (0, i))
    gw = HEADS_PER_DIL * HEAD_DIM
    sub_specs, sub_shapes = [], []
    for _, dil in DIL_PATTERNS:
        assert tm % (16 * dil) == 0
        sub_specs += [pl.BlockSpec((tm // dil, dil * gw), lambda i: (i, 0))] * 3
        sub_shapes += [jax.ShapeDtypeStruct((t // dil, dil * gw), BF16)] * 3
    return pl.pallas_call(
        _inproj_b_kernel,
        grid=(t // tm,),
        in_specs=[pl.BlockSpec((tm, d), lambda i: (i, 0)), pl.BlockSpec((1, d), lambda i: (0, 0)),
                  pl.BlockSpec(wt.shape, lambda i: (0, 0)), col(H_HD), col(H_HD)],
        out_specs=sub_specs + [col(N_MEM_HEADS * HEAD_DIM)],
        out_shape=sub_shapes + [jax.ShapeDtypeStruct((N_MEM_HEADS * HEAD_DIM, t), BF16)],
        scratch_shapes=[pltpu.VMEM((2, gw, tm), F32), pltpu.VMEM((3, gw // LANES, tm, LANES), F32)],
        compiler_params=_params("parallel"),
        name="inproj_b",
    )(x, g.reshape(1, d), wt, tabs[0], tabs[1])


TQ = 256
CR = 32
COARSE_BITS = 16
FINE_BITS = 17
BOUND_SLACK = 1.01
UNDERFLOW_GUARD = 2.0 ** -100


def _key_to_f32(key):
    bits = jnp.where(key < 0, key ^ jnp.int32(0x7FFFFFFF), key)
    return pltpu.bitcast(bits, F32)


def _dsa_kernel(q_ref, qi_ref, wi_ref, ki_ref, k_ref, va_ref, kn_ref, o_ref,
                sc_scr, sc16_scr, tau_scr, need_scr, tie_scr, tri_scr, bound_scr, m_scr, alpha_scr, acc_scr, s0_scr, s1_scr, p0_scr, p1_scr,
                *, seq, n_sel):
    i = pl.program_id(1)
    nch = i + 1
    krow = lax.broadcasted_iota(I32, (TQ, TQ), 0)
    qcol = lax.broadcasted_iota(I32, (TQ, TQ), 1)

    def chunk_off(c):
        return pl.multiple_of(c * TQ, TQ)

    def score_chunk(c, diag):
        off = chunk_off(c)
        kic = ki_ref[pl.ds(off, TQ), :]
        sc = jnp.zeros((TQ, TQ), F32)
        for h in range(IDX_HEADS):
            lg = _dot(kic, qi_ref[h * IDX_DIM:(h + 1) * IDX_DIM, :])
            sc = sc + jnp.maximum(lg, 0.0) * wi_ref[h:h + 1, :]
        if diag:
            sc = jnp.where(krow > qcol, -jnp.inf, sc)
        sc_scr[pl.ds(off, TQ), :] = sc
        sc16_scr[pl.ds(off, TQ), :] = sc.astype(BF16)

    def score_pair(t, carry):
        score_chunk(2 * t, False)
        score_chunk(2 * t + 1, False)
        return carry

    lax.fori_loop(0, i // 2, score_pair, 0)

    @pl.when(i % 2 == 1)
    def _():
        score_chunk(i - 1, False)

    score_chunk(i, True)

    def over_chunks(one, init):
        acc = lax.fori_loop(0, nch // 2, lambda t, a: one(2 * t + 1, one(2 * t, a)), init)
        return lax.cond(nch % 2 == 1, lambda a: one(nch - 1, a), lambda a: a, acc)

    def count(pred):
        def one(c, acc):
            off = chunk_off(c)
            ind = pred(sc_scr[pl.ds(off, TQ), :], off)
            return acc + jnp.sum(ind.reshape(TQ // CR, CR, TQ), axis=0)
        return jnp.sum(over_chunks(one, jnp.zeros((CR, TQ), F32)), axis=0, keepdims=True)

    @pl.when(i * TQ < n_sel)
    def _():
        tau_scr[...] = jnp.full((1, TQ), -jnp.inf, F32)
        need_scr[...] = jnp.zeros((1, TQ), F32)

    @pl.when(i * TQ >= n_sel)
    def _():
        def body16(step, tau16):
            cand = tau16 + jnp.left_shift(jnp.int32(1), COARSE_BITS - 1 - step)
            bits = jnp.where(cand < 0, cand ^ jnp.int32(0x7FFF), cand)
            cand_f = pltpu.bitcast(jnp.left_shift(bits, COARSE_BITS), F32).astype(BF16)

            def one(c, acc):
                ind = jnp.where(sc16_scr[pl.ds(chunk_off(c), TQ), :] >= cand_f,
                                jnp.bfloat16(1.0), jnp.bfloat16(0.0))
                return functools.reduce(lambda a, k: a + ind[k * CR:(k + 1) * CR], range(TQ // CR), acc)
            acc = over_chunks(one, jnp.zeros((CR, TQ), BF16))
            ok = jnp.sum(acc.astype(F32), axis=0, keepdims=True) >= float(n_sel)
            return jnp.where(ok, cand, tau16)

        tau16 = lax.fori_loop(0, COARSE_BITS, body16, jnp.full((1, TQ), -(1 << (COARSE_BITS - 1)), I32))

        key16 = jnp.left_shift(tau16, COARSE_BITS) + jnp.where(tau16 < 0, (1 << COARSE_BITS) - 1, 0)
        lo = key16 - ((1 << (COARSE_BITS - 1)) + 1)

        def body(step, carry):
            delta, cge, crej = carry
            cand = delta + jnp.left_shift(jnp.int32(1), FINE_BITS - 1 - step)
            cand_f = _key_to_f32(lo + cand)
            cnt = count(lambda blk, _: jnp.where(blk >= cand_f, 1.0, 0.0))
            ok = cnt >= float(n_sel)
            return jnp.where(ok, cand, delta), jnp.where(ok, cnt, cge), jnp.where(ok, crej, cnt)

        init = (jnp.zeros((1, TQ), I32), jnp.full((1, TQ), 2.0 * n_sel, F32), jnp.zeros((1, TQ), F32))
        delta, cge, crej = lax.fori_loop(0, FINE_BITS, body, init)
        tau = lo + delta
        tau_scr[...] = _key_to_f32(tau)
        need_scr[...] = jnp.where(cge > float(n_sel), float(n_sel) - crej, 2.0 * seq)

    tri_scr[...] = jnp.where(krow >= qcol, 1.0, 0.0).astype(BF16)
    grp = N_HEADS_A // N_KV_A
    tau_f = tau_scr[...]
    need = need_scr[...]

    def chunk_select(c):
        blk = sc_scr[pl.ds(chunk_off(c), TQ), :]
        eq = blk == tau_f
        rank = _dot(tri_scr[...], jnp.where(eq, 1.0, 0.0).astype(BF16)) + tie_scr[...]
        tie_scr[...] = rank[TQ - 1:TQ, :]
        return jnp.where(eq, jnp.where(rank <= need, 1.0, 0.0), jnp.where(blk > tau_f, 1.0, 0.0))

    kmax2 = jnp.max(kn_ref[...], axis=1, keepdims=True)
    for h in range(N_HEADS_A):
        qf = q_ref[h * HEAD_DIM:(h + 1) * HEAD_DIM, :].astype(F32)
        qn2 = jnp.sum(qf * qf, axis=0, keepdims=True)
        bound_scr[h] = jnp.sqrt(qn2 * kmax2[h // grp:h // grp + 1]) * BOUND_SLACK
    acc_scr[...] = jnp.zeros(acc_scr.shape, F32)
    tie_scr[...] = jnp.zeros((1, TQ), F32)

    def pipeline(stage_a, stage_b, buf0, buf1):
        def step(c, src, dst):
            stage_b(c - 1, src)
            stage_a(c, dst)

        stage_a(0, buf0)

        def pair_body(t, carry):
            step(2 * t + 1, buf0, buf1)
            step(2 * t + 2, buf1, buf0)
            return carry

        lax.fori_loop(0, (nch - 1) // 2, pair_body, 0)

        @pl.when((nch - 1) % 2 == 1)
        def _():
            step(nch - 1, buf0, buf1)
            stage_b(nch - 1, buf1)

        @pl.when((nch - 1) % 2 == 0)
        def _():
            stage_b(nch - 1, buf0)

    def fast_a(c, p_dst):
        off = chunk_off(c)
        sel = chunk_select(c).astype(BF16)
        for g in range(N_KV_A):
            kc = k_ref[pl.ds(off, TQ), g * HEAD_DIM:(g + 1) * HEAD_DIM]
            for j in range(grp):
                h = g * grp + j
                s = _dot(kc, q_ref[h * HEAD_DIM:(h + 1) * HEAD_DIM, :])
                p_dst[h] = jnp.exp2(s - bound_scr[h]).astype(BF16) * sel

    def fast_b(c, p_src):
        off = chunk_off(c)
        for g in range(N_KV_A):
            vt = va_ref[g * VR:(g + 1) * VR, pl.ds(off, TQ)]
            for j in range(grp):
                h = g * grp + j
                acc_scr[h] += _dot(vt, p_src[h])

    pipeline(fast_a, fast_b, p0_scr, p1_scr)
    lmin = functools.reduce(jnp.minimum, [acc_scr[h, HEAD_DIM:HEAD_DIM + 1, :] for h in range(N_HEADS_A)])

    def stage_a(c, s_dst):
        off = chunk_off(c)
        bias = (chunk_select(c) - 1.0) * (-NEG)
        for g in range(N_KV_A):
            kc = k_ref[pl.ds(off, TQ), g * HEAD_DIM:(g + 1) * HEAD_DIM]
            for j in range(grp):
                h = g * grp + j
                s = _dot(kc, q_ref[h * HEAD_DIM:(h + 1) * HEAD_DIM, :]) + bias
                s_dst[h] = s
                m_old = m_scr[h]
                m_new = jnp.maximum(m_old, jnp.max(s, axis=0, keepdims=True))
                alpha_scr[h] = jnp.exp2(m_old - m_new)
                m_scr[h] = m_new

    def stage_b(c, s_src):
        off = chunk_off(c)
        for g in range(N_KV_A):
            vt = va_ref[g * VR:(g + 1) * VR, pl.ds(off, TQ)]
            for j in range(grp):
                h = g * grp + j
                p = jnp.exp2(s_src[h] - m_scr[h]).astype(BF16)
                acc_scr[h] = alpha_scr[h] * acc_scr[h] + _dot(vt, p)

    @pl.when(jnp.logical_not(jnp.min(lmin) > UNDERFLOW_GUARD))
    def _():
        m_scr[...] = jnp.full(m_scr.shape, NEG, F32)
        acc_scr[...] = jnp.zeros(acc_scr.shape, F32)
        tie_scr[...] = jnp.zeros((1, TQ), F32)
        pipeline(stage_a, stage_b, s0_scr, s1_scr)

    for h in range(N_HEADS_A):
        a = acc_scr[h]
        o_ref[h * HEAD_DIM:(h + 1) * HEAD_DIM, :] = (a[:HEAD_DIM] / a[HEAD_DIM:HEAD_DIM + 1]).astype(BF16)


def _dsa_attention(qt, qit, wit, ki, k, vat, kn, b, s):
    n_sel = min(TOPK_MAX, s // 4)
    assert s % TQ == 0 and n_sel % TQ == 0
    nq = s // TQ
    qblk = lambda r: pl.BlockSpec((r, TQ), lambda bi, i: (0, bi * nq + i))
    tok = lambda c: pl.BlockSpec((s, c), lambda bi, i: (bi, 0))
    return pl.pallas_call(
        functools.partial(_dsa_kernel, seq=s, n_sel=n_sel),
        grid=(b, nq),
        in_specs=[qblk(qt.shape[0]), qblk(qit.shape[0]), qblk(wit.shape[0]),
                  tok(ki.shape[1]), tok(k.shape[1]),
                  pl.BlockSpec((vat.shape[0], s), lambda bi, i: (0, bi)),
                  pl.BlockSpec((kn.shape[0], s), lambda bi, i: (0, bi))],
        out_specs=qblk(qt.shape[0]),
        out_shape=jax.ShapeDtypeStruct(qt.shape, BF16),
        scratch_shapes=[
            pltpu.VMEM((s, TQ), F32),
            pltpu.VMEM((s, TQ), BF16),
            pltpu.VMEM((1, TQ), F32),
            pltpu.VMEM((1, TQ), F32),
            pltpu.VMEM((1, TQ), F32),
            pltpu.VMEM((TQ, TQ), BF16),
            pltpu.VMEM((N_HEADS_A, 1, TQ), F32),
            pltpu.VMEM((N_HEADS_A, 1, TQ), F32),
            pltpu.VMEM((N_HEADS_A, 1, TQ), F32),
            pltpu.VMEM((N_HEADS_A, VR, TQ), F32),
            pltpu.VMEM((N_HEADS_A, TQ, TQ), F32),
            pltpu.VMEM((N_HEADS_A, TQ, TQ), F32),
            pltpu.VMEM((N_HEADS_A, TQ, TQ), BF16),
            pltpu.VMEM((N_HEADS_A, TQ, TQ), BF16),
        ],
        compiler_params=_params("parallel", "arbitrary"),
        name="dsa_attention",
    )(qt, qit, wit, ki, k, vat, kn)


def _mem_attn_kernel(q_ref, k_ref, v_ref, o_ref, s_scr, m_scr):
    nsub = q_ref.shape[1] // MEM_SUB

    def stage_a(sb):
        ts = slice(sb * MEM_SUB, (sb + 1) * MEM_SUB)
        for h in range(N_MEM_HEADS):
            sl = slice(h * HEAD_DIM, (h + 1) * HEAD_DIM)
            s = _dot(k_ref[:, sl], q_ref[sl, ts])
            s_scr[sb, h] = s
            m_scr[sb, h] = jnp.max(s, axis=0, keepdims=True)

    def stage_b(sb):
        ts = slice(sb * MEM_SUB, (sb + 1) * MEM_SUB)
        for h in range(N_MEM_HEADS):
            sl = slice(h * HEAD_DIM, (h + 1) * HEAD_DIM)
            p = jnp.exp(s_scr[sb, h] - m_scr[sb, h])
            l = jnp.sum(p, axis=0, keepdims=True)
            o_ref[sl, ts] = (_dot(v_ref[sl, :], p.astype(BF16)) / l).astype(BF16)

    stage_a(0)
    for sb in range(1, nsub):
        stage_b(sb - 1)
        stage_a(sb)
    stage_b(nsub - 1)


MEM_SUB = 512


def _mem_attention(qmt, km, vmt, s):
    c, t = qmt.shape
    m = km.shape[1]
    tm = min(1024, s)
    assert tm % MEM_SUB == 0
    nq = s // tm
    blk = pl.BlockSpec((c, tm), lambda bi, i: (0, bi * nq + i))
    return pl.pallas_call(
        _mem_attn_kernel,
        grid=(t // s, nq),
        in_specs=[blk,
                  pl.BlockSpec((None, m, km.shape[2]), lambda bi, i: (bi, 0, 0)),
                  pl.BlockSpec((None, vmt.shape[1], m), lambda bi, i: (bi, 0, 0))],
        out_specs=blk,
        out_shape=jax.ShapeDtypeStruct((c, t), BF16),
        scratch_shapes=[pltpu.VMEM((tm // MEM_SUB, N_MEM_HEADS, m, MEM_SUB), F32),
                        pltpu.VMEM((tm // MEM_SUB, N_MEM_HEADS, 1, MEM_SUB), F32)],
        compiler_params=_params("parallel", "parallel"),
        name="mem_attention",
    )(qmt, km, vmt)


QB = 512
BAND_QUERIES = 1024
LN2 = 0.6931471805599453


def _band_kernel(q_ref, kp_ref, kc_ref, vp_ref, vc_ref, o_ref, lse_ref, s_scr, m_scr, ot_scr, lt_scr, *, qb, ns):
    j = pl.program_id(1)
    nsub = qb // BLK
    krow = lax.broadcasted_iota(I32, (BLK, BLK), 0)
    qcol = lax.broadcasted_iota(I32, (BLK, BLK), 1)
    bias_prev = jnp.where(krow >= qcol, 0.0, NEG)
    bias_cur = jnp.where(krow <= qcol, 0.0, NEG)
    no_prev = jnp.where(j > 0, 0.0, NEG)
    lt_scr[...] = jnp.zeros(lt_scr.shape, F32)

    gw = HEADS_PER_DIL * HEAD_DIM

    def stage_a(u):
        sq, sb = divmod(u, nsub)
        qs = slice(sb * BLK, (sb + 1) * BLK)
        for h in range(HEADS_PER_DIL):
            hs = slice(sq * gw + h * HEAD_DIM, sq * gw + (h + 1) * HEAD_DIM)
            qh = q_ref[qs, hs]
            if sb == 0:
                s_p = _dot_nt(kp_ref[:, hs], qh) + (bias_prev + no_prev)
            else:
                s_p = _dot_nt(kc_ref[(sb - 1) * BLK:sb * BLK, hs], qh) + bias_prev
            s_c = _dot_nt(kc_ref[qs, hs], qh) + bias_cur
            s_scr[u, h, 0:BLK] = s_p
            s_scr[u, h, BLK:2 * BLK] = s_c
            m_scr[u, h] = jnp.maximum(jnp.max(s_p, axis=0, keepdims=True), jnp.max(s_c, axis=0, keepdims=True))

    def stage_b(u):
        sq, sb = divmod(u, nsub)
        qs = slice(sb * BLK, (sb + 1) * BLK)
        for h in range(HEADS_PER_DIL):
            hs = slice(sq * gw + h * HEAD_DIM, sq * gw + (h + 1) * HEAD_DIM)
            m = m_scr[u, h]
            p_p = jnp.exp2(s_scr[u, h, 0:BLK] - m)
            p_c = jnp.exp2(s_scr[u, h, BLK:2 * BLK] - m)
            l = jnp.sum(p_p, axis=0, keepdims=True) + jnp.sum(p_c, axis=0, keepdims=True)
            v_p = vp_ref[:, hs] if sb == 0 else vc_ref[(sb - 1) * BLK:sb * BLK, hs]
            acc = _dot_tn(v_p, p_p.astype(BF16)) + _dot_tn(vc_ref[qs, hs], p_c.astype(BF16))
            ot_scr[sq, h * HEAD_DIM:(h + 1) * HEAD_DIM, qs] = acc / l
            lt_scr[sq, h:h + 1, qs] = m * LN2 + jnp.log(l)

    units = ns * nsub
    groups = [range(g, min(g + 2, units)) for g in range(0, units, 2)]
    for u in groups[0]:
        stage_a(u)
    for prev, nxt in zip(groups[:-1], groups[1:]):
        for u in prev:
            stage_b(u)
        for u in nxt:
            stage_a(u)
    for u in groups[-1]:
        stage_b(u)
    for sq in range(ns):
        o_ref[:, sq * gw:(sq + 1) * gw] = ot_scr[sq].T.astype(o_ref.dtype)
        lse_ref[:, sq * LANES:(sq + 1) * LANES] = lt_scr[sq].T


def _band_attention(q, k, v, b, dil):
    rows, width = q.shape
    c = width // dil
    t = rows * dil
    n = t // (b * dil)
    qb = min(QB, n)
    assert n % qb == 0 and qb % BLK == 0
    rr, nj = qb // BLK, n // qb
    ns = min(dil, max(1, BAND_QUERIES // qb))
    assert dil % ns == 0
    gpb = dil // ns
    cur = lambda w: pl.BlockSpec((qb, ns * w), lambda sg, j: ((sg // gpb) * nj + j, sg % gpb))
    prev = pl.BlockSpec((BLK, ns * c),
                        lambda sg, j: ((sg // gpb) * (n // BLK) + jnp.maximum(j * rr - 1, 0), sg % gpb))
    return pl.pallas_call(
        functools.partial(_band_kernel, qb=qb, ns=ns),
        grid=(b * gpb, nj),
        in_specs=[cur(c), prev, cur(c), prev, cur(c)],
        out_specs=[cur(c), cur(LANES)],
        out_shape=[jax.ShapeDtypeStruct((rows, dil * c), BF16), jax.ShapeDtypeStruct((rows, dil * LANES), F32)],
        scratch_shapes=[pltpu.VMEM((ns * rr, HEADS_PER_DIL, 2 * BLK, BLK), F32),
                        pltpu.VMEM((ns * rr, HEADS_PER_DIL, 1, BLK), F32),
                        pltpu.VMEM((ns, c, qb), F32),
                        pltpu.VMEM((ns, LANES, qb), F32)],
        compiler_params=_params("parallel", "parallel"),
        name="band_attention",
    )(q, k, k, v, v)


def _merge_kernel(*refs):
    ng = len(DIL_PATTERNS)
    o_refs, l_refs, out_ref, o_scr, l_scr = refs[:ng], refs[ng:2 * ng], refs[2 * ng], refs[2 * ng + 1], refs[2 * ng + 2]
    tm, gw = out_ref.shape
    os_, lses = [], []
    for g, (_, dil) in enumerate(DIL_PATTERNS):
        if dil == 1:
            os_.append([o_refs[g][:, cb * LANES:(cb + 1) * LANES].astype(F32) for cb in range(gw // LANES)])
            lses.append(l_refs[g][...])
            continue
        for r in range(dil):
            rows = pl.ds(r, tm // dil, stride=dil)
            for cb in range(gw // LANES):
                o_scr[g, cb, rows, :] = o_refs[g][:, r * gw + cb * LANES:r * gw + (cb + 1) * LANES].astype(F32)
            l_scr[g, rows, :] = l_refs[g][:, r * LANES:(r + 1) * LANES]
        os_.append([o_scr[g, cb] for cb in range(gw // LANES)])
        lses.append(l_scr[g])
    m = functools.reduce(jnp.maximum, lses)
    es = [jnp.exp(l - m) for l in lses]
    den = sum(es)
    spread = jnp.where(lax.broadcasted_iota(I32, (LANES, gw), 1) // HEAD_DIM
                       == lax.broadcasted_iota(I32, (LANES, gw), 0), 1.0, 0.0).astype(BF16)
    ws = [_dot((e / den).astype(BF16), spread) for e in es]
    for cb in range(gw // LANES):
        cs = slice(cb * LANES, (cb + 1) * LANES)
        out_ref[:, cs] = sum(w[:, cs] * o[cb] for w, o in zip(ws, os_)).astype(out_ref.dtype)


def _merge_groups(os_, lses, t):
    gw = HEADS_PER_DIL * HEAD_DIM
    tm = min(1024, t)
    ng = len(DIL_PATTERNS)
    spec = lambda w: [pl.BlockSpec((tm // dil, dil * w), lambda i: (i, 0)) for _, dil in DIL_PATTERNS]
    return pl.pallas_call(
        _merge_kernel,
        grid=(t // tm,),
        in_specs=spec(gw) + spec(LANES),
        out_specs=pl.BlockSpec((tm, gw), lambda i: (i, 0)),
        out_shape=jax.ShapeDtypeStruct((t, gw), BF16),
        scratch_shapes=[pltpu.VMEM((ng, gw // LANES, tm, LANES), F32), pltpu.VMEM((ng, tm, LANES), F32)],
        compiler_params=_params("parallel"),
        name="merge_groups",
    )(*os_, *lses)


def _ffn_kernel(x_ref, mix_ref, mo_ref, wo1_ref, wo2_ref, g_ref, wgu_ref, wd_ref, gf_ref,
                o_ref, act_scr, *, final_norm, tf, mix_token_major):
    dff = wd_ref.shape[0]
    mixed = _dot(mix_ref[...], wo1_ref[...]) if mix_token_major else _dot_tn(mix_ref[...], wo1_ref[...])
    x2 = x_ref[...] + mixed + _dot_tn(mo_ref[...], wo2_ref[...])
    h = _rms(x2, g_ref[...]).astype(BF16)
    for f in range(dff // tf):
        gate = _dot(h, wgu_ref[:, f * tf:(f + 1) * tf])
        up = _dot(h, wgu_ref[:, dff + f * tf:dff + (f + 1) * tf])
        act_scr[:, f * tf:(f + 1) * tf] = (gate * jax.nn.sigmoid(gate) * up).astype(BF16)
    y = x2 + _dot(act_scr[...], wd_ref[...])
    if final_norm:
        y = _rms(y, gf_ref[...])
    o_ref[...] = y


def _out_ffn(x, mix, mo, w_out, g_ffn, w_gate_up, w_down, g_final, final_norm):
    t, d = x.shape
    mix_token_major = mix.shape[0] == t
    cm, cmo = mix.shape[1 if mix_token_major else 0], mo.shape[0]
    dff = w_down.shape[0]
    wo1 = w_out[:cm].astype(BF16)
    wo2 = w_out[cm:].astype(BF16)
    wgu = w_gate_up.astype(BF16)
    wd = w_down.astype(BF16)
    tm = min(512, t)
    tf = 256 if dff % 256 == 0 else dff
    row = lambda c: pl.BlockSpec((tm, c), lambda i: (i, 0))
    const = lambda r, c: pl.BlockSpec((r, c), lambda i: (0, 0), pipeline_mode=pl.Buffered(1))
    return pl.pallas_call(
        functools.partial(_ffn_kernel, final_norm=final_norm, tf=tf, mix_token_major=mix_token_major),
        grid=(t // tm,),
        in_specs=[row(d), row(cm) if mix_token_major else pl.BlockSpec((cm, tm), lambda i: (0, i)),
                  pl.BlockSpec((cmo, tm), lambda i: (0, i)),
                  const(cm, d), const(cmo, d), const(1, d),
                  const(d, 2 * dff), const(dff, d), const(1, d)],
        out_specs=row(d),
        out_shape=jax.ShapeDtypeStruct((t, d), F32),
        scratch_shapes=[pltpu.VMEM((tm, dff), BF16)],
        compiler_params=_params("parallel"),
        name="out_ffn",
    )(x, mix, mo, wo1, wo2, g_ffn.reshape(1, d), wgu, wd, g_final.reshape(1, d))


def kernel(x, mem, positions,
           l0_norm_mix, l0_norm_mem, l0_w_in, l0_w_mem_kv, l0_w_out, l0_norm_ffn, l0_w_gate_up, l0_w_down,
           l1_norm_mix, l1_norm_mem, l1_w_in, l1_w_mem_kv, l1_w_out, l1_norm_ffn, l1_w_gate_up, l1_w_down,
           final_norm):
    b, s, d = x.shape
    t = b * s
    xt = x.reshape(t, d)
    tabs = _trig_tables(positions.reshape(1, t))

    qt, k, vat, qit, ki, wit, qmt, kn = _inproj_a(xt, l0_norm_mix, l0_w_in, tabs)
    mix = _dsa_attention(qt, qit, wit, ki, k, vat, kn, b, s)
    mo = _mem_attention(qmt, *_mem_kv(mem, l0_norm_mem, l0_w_mem_kv), s)
    xt = _out_ffn(xt, mix, mo, l0_w_out, l0_norm_ffn, l0_w_gate_up, l0_w_down, final_norm, False)

    outs = _inproj_b(xt, l1_norm_mix, l1_w_in, tabs)
    os_, lses = [], []
    for g, (window, dil) in enumerate(DIL_PATTERNS):
        assert window // dil == BLK
        o, lse = _band_attention(*outs[3 * g:3 * g + 3], b, dil)
        os_.append(o)
        lses.append(lse)
    mix = _merge_groups(os_, lses, t)
    mo = _mem_attention(outs[-1], *_mem_kv(mem, l1_norm_mem, l1_w_mem_kv), s)
    xt = _out_ffn(xt, mix, mo, l1_w_out, l1_norm_ffn, l1_w_gate_up, l1_w_down, final_norm, True)
    return xt.reshape(b, s, d)
```

```python
import functools

import jax
import jax.numpy as jnp
from jax import lax
from jax.experimental import pallas as pl
from jax.experimental.pallas import tpu as pltpu

F32 = jnp.float32
BF16 = jnp.bfloat16
I32 = jnp.int32

HEAD_DIM = 64
N_HEADS_A = 12
N_KV_A = 4
IDX_HEADS = 8
IDX_DIM = 64
IDX_ROPE_DIM = 32
TOPK_MAX = 256
DIL_PATTERNS = ((128, 1), (512, 4), (2048, 16))
HEADS_PER_DIL = 4
N_MEM_HEADS = 4
BLK = 128
ROPE_THETA = 10000.0
EPS = 1e-6
NEG = -1e30

LANES = 128
VMEM_LIMIT = 56 * 1024 * 1024

Q_SCALE = HEAD_DIM ** -0.5
WI_SCALE = IDX_HEADS ** -0.5 * IDX_DIM ** -0.5
LOG2E = 1.4426950408889634


def _dot(a, b):
    return jnp.dot(a, b, preferred_element_type=F32)


def _dot_nt(a, b):
    return lax.dot_general(a, b, (((1,), (1,)), ((), ())), preferred_element_type=F32)


def _dot_tn(a, b):
    return lax.dot_general(a, b, (((0,), (0,)), ((), ())), preferred_element_type=F32)


def _params(*sem):
    return pltpu.CompilerParams(dimension_semantics=sem, vmem_limit_bytes=VMEM_LIMIT)


def _rms(x, g):
    ms = jnp.mean(x * x, axis=-1, keepdims=True)
    return x * lax.rsqrt(ms + EPS) * g


H_HD = HEAD_DIM // 2
H_IX = IDX_ROPE_DIM // 2


def _trig_kernel(pos_ref, f_ref, chd_ref, shd_ref, cix_ref, six_ref):
    tm = pos_ref.shape[1]
    pos = pos_ref[...].astype(F32)
    f = jnp.concatenate([f_ref[...]] * (tm // LANES), axis=1)
    ang = f * pos
    c, s = jnp.cos(ang), jnp.sin(ang)
    chd_ref[...] = c[:H_HD]
    shd_ref[...] = s[:H_HD]
    cix_ref[...] = c[H_HD:]
    six_ref[...] = s[H_HD:]


def _trig_tables(pos_row):
    t = pos_row.shape[1]
    tm = min(2048, t)
    f_hd = ROPE_THETA ** (-jnp.arange(H_HD, dtype=F32) / H_HD)
    f_ix = ROPE_THETA ** (-jnp.arange(H_IX, dtype=F32) / H_IX)
    f = jnp.broadcast_to(jnp.concatenate([f_hd, f_ix])[:, None], (H_HD + H_IX, LANES))
    spec = lambda r: pl.BlockSpec((r, tm), lambda i: (0, i))
    rows = [H_HD, H_HD, H_IX, H_IX]
    return pl.pallas_call(
        _trig_kernel,
        grid=(t // tm,),
        in_specs=[spec(1), pl.BlockSpec((H_HD + H_IX, LANES), lambda i: (0, 0))],
        out_specs=[spec(r) for r in rows],
        out_shape=[jax.ShapeDtypeStruct((r, t), F32) for r in rows],
        compiler_params=_params("parallel"),
        name="rope_tables",
    )(pos_row, f)


def _norm_matmul_kernel(x_ref, g_ref, w_ref, o_ref):
    h = _rms(x_ref[...], g_ref[...]).astype(BF16)
    o_ref[...] = _dot(h, w_ref[...]).astype(o_ref.dtype)


def _norm_matmul(x, g, w, out_dtype):
    t, d = x.shape
    n = w.shape[1]
    tm = min(512, t)
    return pl.pallas_call(
        _norm_matmul_kernel,
        grid=(t // tm,),
        in_specs=[pl.BlockSpec((tm, d), lambda i: (i, 0)),
                  pl.BlockSpec((1, d), lambda i: (0, 0)),
                  pl.BlockSpec((d, n), lambda i: (0, 0))],
        out_specs=pl.BlockSpec((tm, n), lambda i: (i, 0)),
        out_shape=jax.ShapeDtypeStruct((t, n), out_dtype),
        compiler_params=_params("parallel"),
        name="norm_matmul",
    )(x, g.reshape(1, d), w)


def _mem_kv(mem, g, w_kv):
    b, m, d = mem.shape
    kv = _norm_matmul(mem.reshape(b * m, d), g, w_kv.astype(BF16), BF16).reshape(b, m, -1)
    nk = N_MEM_HEADS * HEAD_DIM
    return kv[:, :, :nk], kv[:, :, nk:].transpose(0, 2, 1)


VR = 80

A_Q, A_K, A_V, A_QI, A_KI, A_WI, A_QM, A_END = 0, 768, 1024, 1280, 1792, 1856, 1872, 2128


def _rope_heads(p, nheads, out_ref, half, c, s, scale):
    for hh in range(nheads):
        r0 = hh * HEAD_DIM
        x1, x2 = p[r0:r0 + half], p[r0 + half:r0 + 2 * half]
        out_ref[r0:r0 + half, :] = ((x1 * c - x2 * s) * scale).astype(out_ref.dtype)
        out_ref[r0 + half:r0 + 2 * half, :] = ((x2 * c + x1 * s) * scale).astype(out_ref.dtype)
        if 2 * half < HEAD_DIM:
            out_ref[r0 + 2 * half:r0 + HEAD_DIM, :] = (p[r0 + 2 * half:r0 + HEAD_DIM] * scale).astype(out_ref.dtype)


def _write_values(pv, nheads, va_ref):
    tm = pv.shape[1]
    ones_rows = jnp.where(lax.broadcasted_iota(I32, (VR - HEAD_DIM, tm), 0) == 0, 1.0, 0.0).astype(BF16)
    for g in range(nheads):
        va_ref[g * VR:g * VR + HEAD_DIM, :] = pv[g * HEAD_DIM:(g + 1) * HEAD_DIM].astype(BF16)
        va_ref[g * VR + HEAD_DIM:(g + 1) * VR, :] = ones_rows


def _inproj_a_kernel(x_ref, g_ref, wt_ref, chd_ref, shd_ref, cix_ref, six_ref,
                     q_ref, k_ref, va_ref, qi_ref, ki_ref, wi_ref, qm_ref, kn_ref, kt_scr):
    h = _rms(x_ref[...], g_ref[...]).astype(BF16)
    chd, shd = chd_ref[...], shd_ref[...]
    cix, six = cix_ref[...], six_ref[...]

    def proj(a, b):
        return _dot_nt(wt_ref[a:b, :], h)

    _rope_heads(proj(A_Q, A_K), N_HEADS_A, q_ref, H_HD, chd, shd, Q_SCALE * LOG2E)
    nk = N_KV_A * HEAD_DIM
    _rope_heads(proj(A_K, A_V), N_KV_A, kt_scr.at[0:nk], H_HD, chd, shd, 1.0)
    _write_values(proj(A_V, A_QI), N_KV_A, va_ref)
    _rope_heads(proj(A_QI, A_KI), IDX_HEADS, qi_ref, H_IX, cix, six, 1.0)
    pkw = proj(A_KI, A_QM)
    _rope_heads(pkw, 1, kt_scr.at[nk:nk + IDX_DIM], H_IX, cix, six, 1.0)
    k_ref[...] = kt_scr[0:nk, :].T.astype(BF16)
    ki_ref[...] = kt_scr[nk:nk + IDX_DIM, :].T.astype(BF16)
    for g in range(N_KV_A):
        kg = kt_scr[g * HEAD_DIM:(g + 1) * HEAD_DIM, :]
        kn_ref[g:g + 1, :] = jnp.sum(kg * kg, axis=0, keepdims=True)
    wi_ref[...] = pkw[IDX_DIM:IDX_DIM + IDX_HEADS] * WI_SCALE
    qm_ref[...] = (proj(A_QM, A_END) * Q_SCALE).astype(BF16)


def _inproj_a(x, g, w_in, tabs):
    t, d = x.shape
    wt = w_in.T
    pad = jnp.zeros((A_QM - A_WI - IDX_HEADS, d), w_in.dtype)
    split = A_WI + IDX_HEADS
    wt = jnp.concatenate([wt[:split], pad, wt[split:]], axis=0).astype(BF16)
    tm = min(1024, t)
    col = lambda r: pl.BlockSpec((r, tm), lambda i: (0, i))
    outs = [(N_HEADS_A * HEAD_DIM, BF16), (N_KV_A * HEAD_DIM, BF16), (N_KV_A * VR, BF16),
            (IDX_HEADS * IDX_DIM, BF16), (IDX_DIM, BF16), (IDX_HEADS, F32), (N_MEM_HEADS * HEAD_DIM, BF16),
            (N_KV_A, F32)]
    return pl.pallas_call(
        _inproj_a_kernel,
        grid=(t // tm,),
        in_specs=[pl.BlockSpec((tm, d), lambda i: (i, 0)), pl.BlockSpec((1, d), lambda i: (0, 0)),
                  pl.BlockSpec((A_END, d), lambda i: (0, 0)),
                  col(H_HD), col(H_HD), col(H_IX), col(H_IX)],
        out_specs=[pl.BlockSpec((tm, r), lambda i: (i, 0)) if k in (1, 4) else col(r)
                   for k, (r, _) in enumerate(outs)],
        out_shape=[jax.ShapeDtypeStruct((t, r) if k in (1, 4) else (r, t), dt) for k, (r, dt) in enumerate(outs)],
        scratch_shapes=[pltpu.VMEM((N_KV_A * HEAD_DIM + IDX_DIM, tm), F32)],
        compiler_params=_params("parallel"),
        name="inproj_a",
    )(x, g.reshape(1, d), wt, *tabs)


def _inproj_b_kernel(x_ref, g_ref, wt_ref, chd_ref, shd_ref, *refs):
    ng = len(DIL_PATTERNS)
    out_refs, (rope_scr, tok_scr) = refs[:3 * ng + 1], refs[3 * ng + 1:]
    h = _rms(x_ref[...], g_ref[...]).astype(BF16)
    chd, shd = chd_ref[...], shd_ref[...]
    gw = HEADS_PER_DIL * HEAD_DIM
    tm = h.shape[0]

    def emit(out_ref, slot, dil, value_t):
        tok = value_t.T
        if dil == 1:
            out_ref[...] = tok.astype(BF16)
        else:
            for cb in range(gw // LANES):
                tok_scr[slot, cb] = tok[:, cb * LANES:(cb + 1) * LANES]
            for r in range(dil):
                for cb in range(gw // LANES):
                    out_ref[:, r * gw + cb * LANES:r * gw + (cb + 1) * LANES] = (
                        tok_scr[slot, cb, pl.ds(r, tm // dil, stride=dil), :].astype(BF16))

    for g, (_, dil) in enumerate(DIL_PATTERNS):
        q_ref, k_ref, v_ref = out_refs[3 * g:3 * g + 3]
        base = 3 * g * gw
        _rope_heads(_dot_nt(wt_ref[base:base + gw, :], h), HEADS_PER_DIL, rope_scr.at[0], H_HD, chd, shd,
                    Q_SCALE * LOG2E)
        emit(q_ref, 0, dil, rope_scr[0])
        _rope_heads(_dot_nt(wt_ref[base + gw:base + 2 * gw, :], h), HEADS_PER_DIL, rope_scr.at[1], H_HD, chd, shd, 1.0)
        emit(k_ref, 1, dil, rope_scr[1])
        emit(v_ref, 2, dil, _dot_nt(wt_ref[base + 2 * gw:base + 3 * gw, :], h))
    out_refs[3 * ng][...] = (_dot_nt(wt_ref[3 * ng * gw:3 * ng * gw + N_MEM_HEADS * HEAD_DIM, :], h)
                            * Q_SCALE).astype(BF16)


def _inproj_b(x, g, w_in, tabs):
    t, d = x.shape
    wt = w_in.T.astype(BF16)
    tm = min(1024, t)
    col = lambda r: pl.BlockSpec((r, tm), lambda i: (0, i))
    gw = HEADS_PER_DIL * HEAD_DIM
    sub_specs, sub_shapes = [], []
    for _, dil in DIL_PATTERNS:
        assert tm % (16 * dil) == 0
        sub_specs += [pl.BlockSpec((tm // dil, dil * gw), lambda i: (i, 0))] * 3
        sub_shapes += [jax.ShapeDtypeStruct((t // dil, dil * gw), BF16)] * 3
    return pl.pallas_call(
        _inproj_b_kernel,
        grid=(t // tm,),
        in_specs=[pl.BlockSpec((tm, d), lambda i: (i, 0)), pl.BlockSpec((1, d), lambda i: (0, 0)),
                  pl.BlockSpec(wt.shape, lambda i: (0, 0)), col(H_HD), col(H_HD)],
        out_specs=sub_specs + [col(N_MEM_HEADS * HEAD_DIM)],
        out_shape=sub_shapes + [jax.ShapeDtypeStruct((N_MEM_HEADS * HEAD_DIM, t), BF16)],
        scratch_shapes=[pltpu.VMEM((2, gw, tm), F32), pltpu.VMEM((3, gw // LANES, tm, LANES), F32)],
        compiler_params=_params("parallel"),
        name="inproj_b",
    )(x, g.reshape(1, d), wt, tabs[0], tabs[1])


TQ = 256
CR = 32
COARSE_BITS = 16
FINE_BITS = 17
BOUND_SLACK = 1.01
UNDERFLOW_GUARD = 2.0 ** -100


def _key_to_f32(key):
    bits = jnp.where(key < 0, key ^ jnp.int32(0x7FFFFFFF), key)
    return pltpu.bitcast(bits, F32)


def _dsa_kernel(q_ref, qi_ref, wi_ref, ki_ref, k_ref, va_ref, kn_ref, o_ref,
                sc_scr, sc16_scr, tau_scr, need_scr, tie_scr, tri_scr, bound_scr, m_scr, alpha_scr, acc_scr, s0_scr, s1_scr, p0_scr, p1_scr,
                *, seq, n_sel):
    i = pl.program_id(1)
    nch = i + 1
    krow = lax.broadcasted_iota(I32, (TQ, TQ), 0)
    qcol = lax.broadcasted_iota(I32, (TQ, TQ), 1)

    def chunk_off(c):
        return pl.multiple_of(c * TQ, TQ)

    def score_chunk(c, diag):
        off = chunk_off(c)
        kic = ki_ref[pl.ds(off, TQ), :]
        sc = jnp.zeros((TQ, TQ), F32)
        for h in range(IDX_HEADS):
            lg = _dot(kic, qi_ref[h * IDX_DIM:(h + 1) * IDX_DIM, :])
            sc = sc + jnp.maximum(lg, 0.0) * wi_ref[h:h + 1, :]
        if diag:
            sc = jnp.where(krow > qcol, -jnp.inf, sc)
        sc_scr[pl.ds(off, TQ), :] = sc
        sc16_scr[pl.ds(off, TQ), :] = sc.astype(BF16)

    def score_pair(t, carry):
        score_chunk(2 * t, False)
        score_chunk(2 * t + 1, False)
        return carry

    lax.fori_loop(0, i // 2, score_pair, 0)

    @pl.when(i % 2 == 1)
    def _():
        score_chunk(i - 1, False)

    score_chunk(i, True)

    def over_chunks(one, init):
        acc = lax.fori_loop(0, nch // 2, lambda t, a: one(2 * t + 1, one(2 * t, a)), init)
        return lax.cond(nch % 2 == 1, lambda a: one(nch - 1, a), lambda a: a, acc)

    def count(pred):
        def one(c, acc):
            off = chunk_off(c)
            ind = pred(sc_scr[pl.ds(off, TQ), :], off)
            return acc + jnp.sum(ind.reshape(TQ // CR, CR, TQ), axis=0)
        return jnp.sum(over_chunks(one, jnp.zeros((CR, TQ), F32)), axis=0, keepdims=True)

    @pl.when(i * TQ < n_sel)
    def _():
        tau_scr[...] = jnp.full((1, TQ), -jnp.inf, F32)
        need_scr[...] = jnp.zeros((1, TQ), F32)

    @pl.when(i * TQ >= n_sel)
    def _():
        def body16(step, tau16):
            cand = tau16 + jnp.left_shift(jnp.int32(1), COARSE_BITS - 1 - step)
            bits = jnp.where(cand < 0, cand ^ jnp.int32(0x7FFF), cand)
            cand_f = pltpu.bitcast(jnp.left_shift(bits, COARSE_BITS), F32).astype(BF16)

            def one(c, acc):
                ind = jnp.where(sc16_scr[pl.ds(chunk_off(c), TQ), :] >= cand_f,
                                jnp.bfloat16(1.0), jnp.bfloat16(0.0))
                return functools.reduce(lambda a, k: a + ind[k * CR:(k + 1) * CR], range(TQ // CR), acc)
            acc = over_chunks(one, jnp.zeros((CR, TQ), BF16))
            ok = jnp.sum(acc.astype(F32), axis=0, keepdims=True) >= float(n_sel)
            return jnp.where(ok, cand, tau16)

        tau16 = lax.fori_loop(0, COARSE_BITS, body16, jnp.full((1, TQ), -(1 << (COARSE_BITS - 1)), I32))

        key16 = jnp.left_shift(tau16, COARSE_BITS) + jnp.where(tau16 < 0, (1 << COARSE_BITS) - 1, 0)
        lo = key16 - ((1 << (COARSE_BITS - 1)) + 1)

        def body(step, carry):
            delta, cge, crej = carry
            cand = delta + jnp.left_shift(jnp.int32(1), FINE_BITS - 1 - step)
            cand_f = _key_to_f32(lo + cand)
            cnt = count(lambda blk, _: jnp.where(blk >= cand_f, 1.0, 0.0))
            ok = cnt >= float(n_sel)
            return jnp.where(ok, cand, delta), jnp.where(ok, cnt, cge), jnp.where(ok, crej, cnt)

        init = (jnp.zeros((1, TQ), I32), jnp.full((1, TQ), 2.0 * n_sel, F32), jnp.zeros((1, TQ), F32))
        delta, cge, crej = lax.fori_loop(0, FINE_BITS, body, init)
        tau = lo + delta
        tau_scr[...] = _key_to_f32(tau)
        need_scr[...] = jnp.where(cge > float(n_sel), float(n_sel) - crej, 2.0 * seq)

    tri_scr[...] = jnp.where(krow >= qcol, 1.0, 0.0).astype(BF16)
    grp = N_HEADS_A // N_KV_A
    tau_f = tau_scr[...]
    need = need_scr[...]

    def chunk_select(c):
        blk = sc_scr[pl.ds(chunk_off(c), TQ), :]
        eq = blk == tau_f
        rank = _dot(tri_scr[...], jnp.where(eq, 1.0, 0.0).astype(BF16)) + tie_scr[...]
        tie_scr[...] = rank[TQ - 1:TQ, :]
        return jnp.where(eq, jnp.where(rank <= need, 1.0, 0.0), jnp.where(blk > tau_f, 1.0, 0.0))

    kmax2 = jnp.max(kn_ref[...], axis=1, keepdims=True)
    for h in range(N_HEADS_A):
        qf = q_ref[h * HEAD_DIM:(h + 1) * HEAD_DIM, :].astype(F32)
        qn2 = jnp.sum(qf * qf, axis=0, keepdims=True)
        bound_scr[h] = jnp.sqrt(qn2 * kmax2[h // grp:h // grp + 1]) * BOUND_SLACK
    acc_scr[...] = jnp.zeros(acc_scr.shape, F32)
    tie_scr[...] = jnp.zeros((1, TQ), F32)

    def pipeline(stage_a, stage_b, buf0, buf1):
        def step(c, src, dst):
            stage_b(c - 1, src)
            stage_a(c, dst)

        stage_a(0, buf0)

        def pair_body(t, carry):
            step(2 * t + 1, buf0, buf1)
            step(2 * t + 2, buf1, buf0)
            return carry

        lax.fori_loop(0, (nch - 1) // 2, pair_body, 0)

        @pl.when((nch - 1) % 2 == 1)
        def _():
            step(nch - 1, buf0, buf1)
            stage_b(nch - 1, buf1)

        @pl.when((nch - 1) % 2 == 0)
        def _():
            stage_b(nch - 1, buf0)

    def fast_a(c, p_dst):
        off = chunk_off(c)
        sel = chunk_select(c).astype(BF16)
        for g in range(N_KV_A):
            kc = k_ref[pl.ds(off, TQ), g * HEAD_DIM:(g + 1) * HEAD_DIM]
            for j in range(grp):
                h = g * grp + j
                s = _dot(kc, q_ref[h * HEAD_DIM:(h + 1) * HEAD_DIM, :])
                p_dst[h] = jnp.exp2(s - bound_scr[h]).astype(BF16) * sel

    def fast_b(c, p_src):
        off = chunk_off(c)
        for g in range(N_KV_A):
            vt = va_ref[g * VR:(g + 1) * VR, pl.ds(off, TQ)]
            for j in range(grp):
                h = g * grp + j
                acc_scr[h] += _dot(vt, p_src[h])

    pipeline(fast_a, fast_b, p0_scr, p1_scr)
    lmin = functools.reduce(jnp.minimum, [acc_scr[h, HEAD_DIM:HEAD_DIM + 1, :] for h in range(N_HEADS_A)])

    def stage_a(c, s_dst):
        off = chunk_off(c)
        bias = (chunk_select(c) - 1.0) * (-NEG)
        for g in range(N_KV_A):
            kc = k_ref[pl.ds(off, TQ), g * HEAD_DIM:(g + 1) * HEAD_DIM]
            for j in range(grp):
                h = g * grp + j
                s = _dot(kc, q_ref[h * HEAD_DIM:(h + 1) * HEAD_DIM, :]) + bias
                s_dst[h] = s
                m_old = m_scr[h]
                m_new = jnp.maximum(m_old, jnp.max(s, axis=0, keepdims=True))
                alpha_scr[h] = jnp.exp2(m_old - m_new)
                m_scr[h] = m_new

    def stage_b(c, s_src):
        off = chunk_off(c)
        for g in range(N_KV_A):
            vt = va_ref[g * VR:(g + 1) * VR, pl.ds(off, TQ)]
            for j in range(grp):
                h = g * grp + j
                p = jnp.exp2(s_src[h] - m_scr[h]).astype(BF16)
                acc_scr[h] = alpha_scr[h] * acc_scr[h] + _dot(vt, p)

    @pl.when(jnp.logical_not(jnp.min(lmin) > UNDERFLOW_GUARD))
    def _():
        m_scr[...] = jnp.full(m_scr.shape, NEG, F32)
        acc_scr[...] = jnp.zeros(acc_scr.shape, F32)
        tie_scr[...] = jnp.zeros((1, TQ), F32)
        pipeline(stage_a, stage_b, s0_scr, s1_scr)

    for h in range(N_HEADS_A):
        a = acc_scr[h]
        o_ref[h * HEAD_DIM:(h + 1) * HEAD_DIM, :] = (a[:HEAD_DIM] / a[HEAD_DIM:HEAD_DIM + 1]).astype(BF16)


def _dsa_attention(qt, qit, wit, ki, k, vat, kn, b, s):
    n_sel = min(TOPK_MAX, s // 4)
    assert s % TQ == 0 and n_sel % TQ == 0
    nq = s // TQ
    qblk = lambda r: pl.BlockSpec((r, TQ), lambda bi, i: (0, bi * nq + i))
    tok = lambda c: pl.BlockSpec((s, c), lambda bi, i: (bi, 0))
    return pl.pallas_call(
        functools.partial(_dsa_kernel, seq=s, n_sel=n_sel),
        grid=(b, nq),
        in_specs=[qblk(qt.shape[0]), qblk(qit.shape[0]), qblk(wit.shape[0]),
                  tok(ki.shape[1]), tok(k.shape[1]),
                  pl.BlockSpec((vat.shape[0], s), lambda bi, i: (0, bi)),
                  pl.BlockSpec((kn.shape[0], s), lambda bi, i: (0, bi))],
        out_specs=qblk(qt.shape[0]),
        out_shape=jax.ShapeDtypeStruct(qt.shape, BF16),
        scratch_shapes=[
            pltpu.VMEM((s, TQ), F32),
            pltpu.VMEM((s, TQ), BF16),
            pltpu.VMEM((1, TQ), F32),
            pltpu.VMEM((1, TQ), F32),
            pltpu.VMEM((1, TQ), F32),
            pltpu.VMEM((TQ, TQ), BF16),
            pltpu.VMEM((N_HEADS_A, 1, TQ), F32),
            pltpu.VMEM((N_HEADS_A, 1, TQ), F32),
            pltpu.VMEM((N_HEADS_A, 1, TQ), F32),
            pltpu.VMEM((N_HEADS_A, VR, TQ), F32),
            pltpu.VMEM((N_HEADS_A, TQ, TQ), F32),
            pltpu.VMEM((N_HEADS_A, TQ, TQ), F32),
            pltpu.VMEM((N_HEADS_A, TQ, TQ), BF16),
            pltpu.VMEM((N_HEADS_A, TQ, TQ), BF16),
        ],
        compiler_params=_params("parallel", "arbitrary"),
        name="dsa_attention",
    )(qt, qit, wit, ki, k, vat, kn)


def _mem_attn_kernel(q_ref, k_ref, v_ref, o_ref, s_scr, m_scr):
    nsub = q_ref.shape[1] // MEM_SUB

    def stage_a(sb):
        ts = slice(sb * MEM_SUB, (sb + 1) * MEM_SUB)
        for h in range(N_MEM_HEADS):
            sl = slice(h * HEAD_DIM, (h + 1) * HEAD_DIM)
            s = _dot(k_ref[:, sl], q_ref[sl, ts])
            s_scr[sb, h] = s
            m_scr[sb, h] = jnp.max(s, axis=0, keepdims=True)

    def stage_b(sb):
        ts = slice(sb * MEM_SUB, (sb + 1) * MEM_SUB)
        for h in range(N_MEM_HEADS):
            sl = slice(h * HEAD_DIM, (h + 1) * HEAD_DIM)
            p = jnp.exp(s_scr[sb, h] - m_scr[sb, h])
            l = jnp.sum(p, axis=0, keepdims=True)
            o_ref[sl, ts] = (_dot(v_ref[sl, :], p.astype(BF16)) / l).astype(BF16)

    stage_a(0)
    for sb in range(1, nsub):
        stage_b(sb - 1)
        stage_a(sb)
    stage_b(nsub - 1)


MEM_SUB = 512


def _mem_attention(qmt, km, vmt, s):
    c, t = qmt.shape
    m = km.shape[1]
    tm = min(2048, s)
    assert tm % MEM_SUB == 0
    nq = s // tm
    blk = pl.BlockSpec((c, tm), lambda bi, i: (0, bi * nq + i))
    return pl.pallas_call(
        _mem_attn_kernel,
        grid=(t // s, nq),
        in_specs=[blk,
                  pl.BlockSpec((None, m, km.shape[2]), lambda bi, i: (bi, 0, 0)),
                  pl.BlockSpec((None, vmt.shape[1], m), lambda bi, i: (bi, 0, 0))],
        out_specs=blk,
        out_shape=jax.ShapeDtypeStruct((c, t), BF16),
        scratch_shapes=[pltpu.VMEM((tm // MEM_SUB, N_MEM_HEADS, m, MEM_SUB), F32),
                        pltpu.VMEM((tm // MEM_SUB, N_MEM_HEADS, 1, MEM_SUB), F32)],
        compiler_params=_params("parallel", "parallel"),
        name="mem_attention",
    )(qmt, km, vmt)


QB = 1024
BAND_QUERIES = 2048
LN2 = 0.6931471805599453


def _band_kernel(q_ref, kp_ref, kc_ref, vp_ref, vc_ref, o_ref, lse_ref, s_scr, m_scr, ot_scr, lt_scr, *, qb, ns):
    j = pl.program_id(1)
    nsub = qb // BLK
    krow = lax.broadcasted_iota(I32, (BLK, BLK), 0)
    qcol = lax.broadcasted_iota(I32, (BLK, BLK), 1)
    bias_prev = jnp.where(krow >= qcol, 0.0, NEG)
    bias_cur = jnp.where(krow <= qcol, 0.0, NEG)
    no_prev = jnp.where(j > 0, 0.0, NEG)
    lt_scr[...] = jnp.zeros(lt_scr.shape, F32)

    gw = HEADS_PER_DIL * HEAD_DIM

    def stage_a(u):
        sq, sb = divmod(u, nsub)
        qs = slice(sb * BLK, (sb + 1) * BLK)
        for h in range(HEADS_PER_DIL):
            hs = slice(sq * gw + h * HEAD_DIM, sq * gw + (h + 1) * HEAD_DIM)
            qh = q_ref[qs, hs]
            if sb == 0:
                s_p = _dot_nt(kp_ref[:, hs], qh) + (bias_prev + no_prev)
            else:
                s_p = _dot_nt(kc_ref[(sb - 1) * BLK:sb * BLK, hs], qh) + bias_prev
            s_c = _dot_nt(kc_ref[qs, hs], qh) + bias_cur
            s_scr[u, h, 0:BLK] = s_p
            s_scr[u, h, BLK:2 * BLK] = s_c
            m_scr[u, h] = jnp.maximum(jnp.max(s_p, axis=0, keepdims=True), jnp.max(s_c, axis=0, keepdims=True))

    def stage_b(u):
        sq, sb = divmod(u, nsub)
        qs = slice(sb * BLK, (sb + 1) * BLK)
        for h in range(HEADS_PER_DIL):
            hs = slice(sq * gw + h * HEAD_DIM, sq * gw + (h + 1) * HEAD_DIM)
            m = m_scr[u, h]
            p_p = jnp.exp2(s_scr[u, h, 0:BLK] - m)
            p_c = jnp.exp2(s_scr[u, h, BLK:2 * BLK] - m)
            l = jnp.sum(p_p, axis=0, keepdims=True) + jnp.sum(p_c, axis=0, keepdims=True)
            v_p = vp_ref[:, hs] if sb == 0 else vc_ref[(sb - 1) * BLK:sb * BLK, hs]
            acc = _dot_tn(v_p, p_p.astype(BF16)) + _dot_tn(vc_ref[qs, hs], p_c.astype(BF16))
            ot_scr[sq, h * HEAD_DIM:(h + 1) * HEAD_DIM, qs] = acc / l
            lt_scr[sq, h:h + 1, qs] = m * LN2 + jnp.log(l)

    units = ns * nsub
    groups = [range(g, min(g + 2, units)) for g in range(0, units, 2)]
    for u in groups[0]:
        stage_a(u)
    for prev, nxt in zip(groups[:-1], groups[1:]):
        for u in prev:
            stage_b(u)
        for u in nxt:
            stage_a(u)
    for u in groups[-1]:
        stage_b(u)
    for sq in range(ns):
        o_ref[:, sq * gw:(sq + 1) * gw] = ot_scr[sq].T.astype(o_ref.dtype)
        lse_ref[:, sq * LANES:(sq + 1) * LANES] = lt_scr[sq].T


def _band_attention(q, k, v, b, dil):
    rows, width = q.shape
    c = width // dil
    t = rows * dil
    n = t // (b * dil)
    qb = min(QB, n)
    assert n % qb == 0 and qb % BLK == 0
    rr, nj = qb // BLK, n // qb
    ns = min(dil, max(1, BAND_QUERIES // qb))
    assert dil % ns == 0
    gpb = dil // ns
    cur = lambda w: pl.BlockSpec((qb, ns * w), lambda sg, j: ((sg // gpb) * nj + j, sg % gpb))
    prev = pl.BlockSpec((BLK, ns * c),
                        lambda sg, j: ((sg // gpb) * (n // BLK) + jnp.maximum(j * rr - 1, 0), sg % gpb))
    return pl.pallas_call(
        functools.partial(_band_kernel, qb=qb, ns=ns),
        grid=(b * gpb, nj),
        in_specs=[cur(c), prev, cur(c), prev, cur(c)],
        out_specs=[cur(c), cur(LANES)],
        out_shape=[jax.ShapeDtypeStruct((rows, dil * c), BF16), jax.ShapeDtypeStruct((rows, dil * LANES), F32)],
        scratch_shapes=[pltpu.VMEM((ns * rr, HEADS_PER_DIL, 2 * BLK, BLK), F32),
                        pltpu.VMEM((ns * rr, HEADS_PER_DIL, 1, BLK), F32),
                        pltpu.VMEM((ns, c, qb), F32),
                        pltpu.VMEM((ns, LANES, qb), F32)],
        compiler_params=_params("parallel", "parallel"),
        name="band_attention",
    )(q, k, k, v, v)


def _merge_kernel(*refs):
    ng = len(DIL_PATTERNS)
    o_refs, l_refs, out_ref, o_scr, l_scr = refs[:ng], refs[ng:2 * ng], refs[2 * ng], refs[2 * ng + 1], refs[2 * ng + 2]
    tm, gw = out_ref.shape
    os_, lses = [], []
    for g, (_, dil) in enumerate(DIL_PATTERNS):
        if dil == 1:
            os_.append([o_refs[g][:, cb * LANES:(cb + 1) * LANES].astype(F32) for cb in range(gw // LANES)])
            lses.append(l_refs[g][...])
            continue
        for r in range(dil):
            rows = pl.ds(r, tm // dil, stride=dil)
            for cb in range(gw // LANES):
                o_scr[g, cb, rows, :] = o_refs[g][:, r * gw + cb * LANES:r * gw + (cb + 1) * LANES].astype(F32)
            l_scr[g, rows, :] = l_refs[g][:, r * LANES:(r + 1) * LANES]
        os_.append([o_scr[g, cb] for cb in range(gw // LANES)])
        lses.append(l_scr[g])
    m = functools.reduce(jnp.maximum, lses)
    es = [jnp.exp(l - m) for l in lses]
    den = sum(es)
    spread = jnp.where(lax.broadcasted_iota(I32, (LANES, gw), 1) // HEAD_DIM
                       == lax.broadcasted_iota(I32, (LANES, gw), 0), 1.0, 0.0).astype(BF16)
    ws = [_dot((e / den).astype(BF16), spread) for e in es]
    for cb in range(gw // LANES):
        cs = slice(cb * LANES, (cb + 1) * LANES)
        out_ref[:, cs] = sum(w[:, cs] * o[cb] for w, o in zip(ws, os_)).astype(out_ref.dtype)


def _merge_groups(os_, lses, t):
    gw = HEADS_PER_DIL * HEAD_DIM
    tm = min(1024, t)
    ng = len(DIL_PATTERNS)
    spec = lambda w: [pl.BlockSpec((tm // dil, dil * w), lambda i: (i, 0)) for _, dil in DIL_PATTERNS]
    return pl.pallas_call(
        _merge_kernel,
        grid=(t // tm,),
        in_specs=spec(gw) + spec(LANES),
        out_specs=pl.BlockSpec((tm, gw), lambda i: (i, 0)),
        out_shape=jax.ShapeDtypeStruct((t, gw), BF16),
        scratch_shapes=[pltpu.VMEM((ng, gw // LANES, tm, LANES), F32), pltpu.VMEM((ng, tm, LANES), F32)],
        compiler_params=_params("parallel"),
        name="merge_groups",
    )(*os_, *lses)


def _ffn_kernel(x_ref, mix_ref, mo_ref, wo1_ref, wo2_ref, g_ref, wgu_ref, wd_ref, gf_ref,
                o_ref, act_scr, *, final_norm, tf, mix_token_major):
    dff = wd_ref.shape[0]
    mixed = _dot(mix_ref[...], wo1_ref[...]) if mix_token_major else _dot_tn(mix_ref[...], wo1_ref[...])
    x2 = x_ref[...] + mixed + _dot_tn(mo_ref[...], wo2_ref[...])
    h = _rms(x2, g_ref[...]).astype(BF16)
    for f in range(dff // tf):
        gate = _dot(h, wgu_ref[:, f * tf:(f + 1) * tf])
        up = _dot(h, wgu_ref[:, dff + f * tf:dff + (f + 1) * tf])
        act_scr[:, f * tf:(f + 1) * tf] = (gate * jax.nn.sigmoid(gate) * up).astype(BF16)
    y = x2 + _dot(act_scr[...], wd_ref[...])
    if final_norm:
        y = _rms(y, gf_ref[...])
    o_ref[...] = y


def _out_ffn(x, mix, mo, w_out, g_ffn, w_gate_up, w_down, g_final, final_norm):
    t, d = x.shape
    mix_token_major = mix.shape[0] == t
    cm, cmo = mix.shape[1 if mix_token_major else 0], mo.shape[0]
    dff = w_down.shape[0]
    wo1 = w_out[:cm].astype(BF16)
    wo2 = w_out[cm:].astype(BF16)
    wgu = w_gate_up.astype(BF16)
    wd = w_down.astype(BF16)
    tm = min(512, t)
    tf = 256 if dff % 256 == 0 else dff
    row = lambda c: pl.BlockSpec((tm, c), lambda i: (i, 0))
    const = lambda r, c: pl.BlockSpec((r, c), lambda i: (0, 0), pipeline_mode=pl.Buffered(1))
    return pl.pallas_call(
        functools.partial(_ffn_kernel, final_norm=final_norm, tf=tf, mix_token_major=mix_token_major),
        grid=(t // tm,),
        in_specs=[row(d), row(cm) if mix_token_major else pl.BlockSpec((cm, tm), lambda i: (0, i)),
                  pl.BlockSpec((cmo, tm), lambda i: (0, i)),
                  const(cm, d), const(cmo, d), const(1, d),
                  const(d, 2 * dff), const(dff, d), const(1, d)],
        out_specs=row(d),
        out_shape=jax.ShapeDtypeStruct((t, d), F32),
        scratch_shapes=[pltpu.VMEM((tm, dff), BF16)],
        compiler_params=_params("parallel"),
        name="out_ffn",
    )(x, mix, mo, wo1, wo2, g_ffn.reshape(1, d), wgu, wd, g_final.reshape(1, d))


def kernel(x, mem, positions,
           l0_norm_mix, l0_norm_mem, l0_w_in, l0_w_mem_kv, l0_w_out, l0_norm_ffn, l0_w_gate_up, l0_w_down,
           l1_norm_mix, l1_norm_mem, l1_w_in, l1_w_mem_kv, l1_w_out, l1_norm_ffn, l1_w_gate_up, l1_w_down,
           final_norm):
    b, s, d = x.shape
    t = b * s
    xt = x.reshape(t, d)
    tabs = _trig_tables(positions.reshape(1, t))

    qt, k, vat, qit, ki, wit, qmt, kn = _inproj_a(xt, l0_norm_mix, l0_w_in, tabs)
    mix = _dsa_attention(qt, qit, wit, ki, k, vat, kn, b, s)
    mo = _mem_attention(qmt, *_mem_kv(mem, l0_norm_mem, l0_w_mem_kv), s)
    xt = _out_ffn(xt, mix, mo, l0_w_out, l0_norm_ffn, l0_w_gate_up, l0_w_down, final_norm, False)

    outs = _inproj_b(xt, l1_norm_mix, l1_w_in, tabs)
    os_, lses = [], []
    for g, (window, dil) in enumerate(DIL_PATTERNS):
        assert window // dil == BLK
        o, lse = _band_attention(*outs[3 * g:3 * g + 3], b, dil)
        os_.append(o)
        lses.append(lse)
    mix = _merge_groups(os_, lses, t)
    mo = _mem_attention(outs[-1], *_mem_kv(mem, l1_norm_mem, l1_w_mem_kv), s)
    xt = _out_ffn(xt, mix, mo, l1_w_out, l1_norm_ffn, l1_w_gate_up, l1_w_down, final_norm, True)
    return xt.reshape(b, s, d)
```

```python
import functools

import jax
import jax.numpy as jnp
from jax import lax
from jax.experimental import pallas as pl
from jax.experimental.pallas import tpu as pltpu

F32 = jnp.float32
BF16 = jnp.bfloat16
I32 = jnp.int32

HEAD_DIM = 64
N_HEADS_A = 12
N_KV_A = 4
IDX_HEADS = 8
IDX_DIM = 64
IDX_ROPE_DIM = 32
TOPK_MAX = 256
DIL_PATTERNS = ((128, 1), (512, 4), (2048, 16))
HEADS_PER_DIL = 4
N_MEM_HEADS = 4
BLK = 128
ROPE_THETA = 10000.0
EPS = 1e-6
NEG = -1e30

LANES = 128
VMEM_LIMIT = 56 * 1024 * 1024

Q_SCALE = HEAD_DIM ** -0.5
WI_SCALE = IDX_HEADS ** -0.5 * IDX_DIM ** -0.5
LOG2E = 1.4426950408889634


def _dot(a, b):
    return jnp.dot(a, b, preferred_element_type=F32)


def _dot_nt(a, b):
    return lax.dot_general(a, b, (((1,), (1,)), ((), ())), preferred_element_type=F32)


def _dot_tn(a, b):
    return lax.dot_general(a, b, (((0,), (0,)), ((), ())), preferred_element_type=F32)


def _params(*sem):
    return pltpu.CompilerParams(dimension_semantics=sem, vmem_limit_bytes=VMEM_LIMIT)


def _rms(x, g):
    ms = jnp.mean(x * x, axis=-1, keepdims=True)
    return x * lax.rsqrt(ms + EPS) * g


H_HD = HEAD_DIM // 2
H_IX = IDX_ROPE_DIM // 2


def _trig_kernel(pos_ref, f_ref, chd_ref, shd_ref, cix_ref, six_ref):
    tm = pos_ref.shape[1]
    pos = pos_ref[...].astype(F32)
    f = jnp.concatenate([f_ref[...]] * (tm // LANES), axis=1)
    ang = f * pos
    c, s = jnp.cos(ang), jnp.sin(ang)
    chd_ref[...] = c[:H_HD]
    shd_ref[...] = s[:H_HD]
    cix_ref[...] = c[H_HD:]
    six_ref[...] = s[H_HD:]


def _trig_tables(pos_row):
    t = pos_row.shape[1]
    tm = min(2048, t)
    f_hd = ROPE_THETA ** (-jnp.arange(H_HD, dtype=F32) / H_HD)
    f_ix = ROPE_THETA ** (-jnp.arange(H_IX, dtype=F32) / H_IX)
    f = jnp.broadcast_to(jnp.concatenate([f_hd, f_ix])[:, None], (H_HD + H_IX, LANES))
    spec = lambda r: pl.BlockSpec((r, tm), lambda i: (0, i))
    rows = [H_HD, H_HD, H_IX, H_IX]
    return pl.pallas_call(
        _trig_kernel,
        grid=(t // tm,),
        in_specs=[spec(1), pl.BlockSpec((H_HD + H_IX, LANES), lambda i: (0, 0))],
        out_specs=[spec(r) for r in rows],
        out_shape=[jax.ShapeDtypeStruct((r, t), F32) for r in rows],
        compiler_params=_params("parallel"),
        name="rope_tables",
    )(pos_row, f)


def _norm_matmul_kernel(x_ref, g_ref, w_ref, o_ref):
    h = _rms(x_ref[...], g_ref[...]).astype(BF16)
    o_ref[...] = _dot(h, w_ref[...]).astype(o_ref.dtype)


def _norm_matmul(x, g, w, out_dtype):
    t, d = x.shape
    n = w.shape[1]
    tm = min(512, t)
    return pl.pallas_call(
        _norm_matmul_kernel,
        grid=(t // tm,),
        in_specs=[pl.BlockSpec((tm, d), lambda i: (i, 0)),
                  pl.BlockSpec((1, d), lambda i: (0, 0)),
                  pl.BlockSpec((d, n), lambda i: (0, 0))],
        out_specs=pl.BlockSpec((tm, n), lambda i: (i, 0)),
        out_shape=jax.ShapeDtypeStruct((t, n), out_dtype),
        compiler_params=_params("parallel"),
        name="norm_matmul",
    )(x, g.reshape(1, d), w)


def _mem_kv(mem, g, w_kv):
    b, m, d = mem.shape
    kv = _norm_matmul(mem.reshape(b * m, d), g, w_kv.astype(BF16), BF16).reshape(b, m, -1)
    nk = N_MEM_HEADS * HEAD_DIM
    return kv[:, :, :nk], kv[:, :, nk:].transpose(0, 2, 1)


BF16_ROWS = 16
VR = HEAD_DIM + BF16_ROWS

A_Q = 0
A_K = A_Q + N_HEADS_A * HEAD_DIM
A_V = A_K + N_KV_A * HEAD_DIM
A_QI = A_V + N_KV_A * HEAD_DIM
A_KI = A_QI + IDX_HEADS * IDX_DIM
A_WI = A_KI + IDX_DIM
A_QM = A_WI + -(-IDX_HEADS // BF16_ROWS) * BF16_ROWS
A_END = A_QM + N_MEM_HEADS * HEAD_DIM


def _rope_heads(p, nheads, out_ref, half, c, s, scale):
    for hh in range(nheads):
        r0 = hh * HEAD_DIM
        x1, x2 = p[r0:r0 + half], p[r0 + half:r0 + 2 * half]
        out_ref[r0:r0 + half, :] = ((x1 * c - x2 * s) * scale).astype(out_ref.dtype)
        out_ref[r0 + half:r0 + 2 * half, :] = ((x2 * c + x1 * s) * scale).astype(out_ref.dtype)
        if 2 * half < HEAD_DIM:
            out_ref[r0 + 2 * half:r0 + HEAD_DIM, :] = (p[r0 + 2 * half:r0 + HEAD_DIM] * scale).astype(out_ref.dtype)


def _write_values(pv, nheads, va_ref):
    tm = pv.shape[1]
    ones_rows = jnp.where(lax.broadcasted_iota(I32, (VR - HEAD_DIM, tm), 0) == 0, 1.0, 0.0).astype(BF16)
    for g in range(nheads):
        va_ref[g * VR:g * VR + HEAD_DIM, :] = pv[g * HEAD_DIM:(g + 1) * HEAD_DIM].astype(BF16)
        va_ref[g * VR + HEAD_DIM:(g + 1) * VR, :] = ones_rows


def _inproj_a_kernel(x_ref, g_ref, wt_ref, chd_ref, shd_ref, cix_ref, six_ref,
                     q_ref, k_ref, va_ref, qi_ref, ki_ref, wi_ref, qm_ref, kn_ref, kt_scr):
    h = _rms(x_ref[...], g_ref[...]).astype(BF16)
    chd, shd = chd_ref[...], shd_ref[...]
    cix, six = cix_ref[...], six_ref[...]

    def proj(a, b):
        return _dot_nt(wt_ref[a:b, :], h)

    _rope_heads(proj(A_Q, A_K), N_HEADS_A, q_ref, H_HD, chd, shd, Q_SCALE * LOG2E)
    nk = N_KV_A * HEAD_DIM
    _rope_heads(proj(A_K, A_V), N_KV_A, kt_scr.at[0:nk], H_HD, chd, shd, 1.0)
    _write_values(proj(A_V, A_QI), N_KV_A, va_ref)
    _rope_heads(proj(A_QI, A_KI), IDX_HEADS, qi_ref, H_IX, cix, six, 1.0)
    pkw = proj(A_KI, A_QM)
    _rope_heads(pkw, 1, kt_scr.at[nk:nk + IDX_DIM], H_IX, cix, six, 1.0)
    k_ref[...] = kt_scr[0:nk, :].T.astype(BF16)
    ki_ref[...] = kt_scr[nk:nk + IDX_DIM, :].T.astype(BF16)
    for g in range(N_KV_A):
        kg = kt_scr[g * HEAD_DIM:(g + 1) * HEAD_DIM, :]
        kn_ref[g:g + 1, :] = jnp.sum(kg * kg, axis=0, keepdims=True)
    wi_ref[...] = pkw[IDX_DIM:IDX_DIM + IDX_HEADS] * WI_SCALE
    qm_ref[...] = (proj(A_QM, A_END) * Q_SCALE).astype(BF16)


def _inproj_a(x, g, w_in, tabs):
    t, d = x.shape
    wt = w_in.T
    pad = jnp.zeros((A_QM - A_WI - IDX_HEADS, d), w_in.dtype)
    split = A_WI + IDX_HEADS
    wt = jnp.concatenate([wt[:split], pad, wt[split:]], axis=0).astype(BF16)
    tm = min(1024, t)
    col = lambda r: pl.BlockSpec((r, tm), lambda i: (0, i))
    outs = [(N_HEADS_A * HEAD_DIM, BF16), (N_KV_A * HEAD_DIM, BF16), (N_KV_A * VR, BF16),
            (IDX_HEADS * IDX_DIM, BF16), (IDX_DIM, BF16), (IDX_HEADS, F32), (N_MEM_HEADS * HEAD_DIM, BF16),
            (N_KV_A, F32)]
    return pl.pallas_call(
        _inproj_a_kernel,
        grid=(t // tm,),
        in_specs=[pl.BlockSpec((tm, d), lambda i: (i, 0)), pl.BlockSpec((1, d), lambda i: (0, 0)),
                  pl.BlockSpec((A_END, d), lambda i: (0, 0)),
                  col(H_HD), col(H_HD), col(H_IX), col(H_IX)],
        out_specs=[pl.BlockSpec((tm, r), lambda i: (i, 0)) if k in (1, 4) else col(r)
                   for k, (r, _) in enumerate(outs)],
        out_shape=[jax.ShapeDtypeStruct((t, r) if k in (1, 4) else (r, t), dt) for k, (r, dt) in enumerate(outs)],
        scratch_shapes=[pltpu.VMEM((N_KV_A * HEAD_DIM + IDX_DIM, tm), F32)],
        compiler_params=_params("parallel"),
        name="inproj_a",
    )(x, g.reshape(1, d), wt, *tabs)


def _inproj_b_kernel(x_ref, g_ref, wt_ref, chd_ref, shd_ref, *refs):
    ng = len(DIL_PATTERNS)
    out_refs, (rope_scr, tok_scr) = refs[:3 * ng + 1], refs[3 * ng + 1:]
    h = _rms(x_ref[...], g_ref[...]).astype(BF16)
    chd, shd = chd_ref[...], shd_ref[...]
    gw = HEADS_PER_DIL * HEAD_DIM
    tm = h.shape[0]

    def emit(out_ref, slot, dil, value_t):
        tok = value_t.T
        if dil == 1:
            out_ref[...] = tok.astype(BF16)
        else:
            for cb in range(gw // LANES):
                tok_scr[slot, cb] = tok[:, cb * LANES:(cb + 1) * LANES]
            for r in range(dil):
                for cb in range(gw // LANES):
                    out_ref[:, r * gw + cb * LANES:r * gw + (cb + 1) * LANES] = (
                        tok_scr[slot, cb, pl.ds(r, tm // dil, stride=dil), :].astype(BF16))

    for g, (_, dil) in enumerate(DIL_PATTERNS):
        q_ref, k_ref, v_ref = out_refs[3 * g:3 * g + 3]
        base = 3 * g * gw
        _rope_heads(_dot_nt(wt_ref[base:base + gw, :], h), HEADS_PER_DIL, rope_scr.at[2 * g], H_HD, chd, shd,
                    Q_SCALE * LOG2E)
        emit(q_ref, 3 * g, dil, rope_scr[2 * g])
        _rope_heads(_dot_nt(wt_ref[base + gw:base + 2 * gw, :], h), HEADS_PER_DIL, rope_scr.at[2 * g + 1], H_HD, chd, shd, 1.0)
        emit(k_ref, 3 * g + 1, dil, rope_scr[2 * g + 1])
        emit(v_ref, 3 * g + 2, dil, _dot_nt(wt_ref[base + 2 * gw:base + 3 * gw, :], h))
    out_refs[3 * ng][...] = (_dot_nt(wt_ref[3 * ng * gw:3 * ng * gw + N_MEM_HEADS * HEAD_DIM, :], h)
                            * Q_SCALE).astype(BF16)


def _inproj_b(x, g, w_in, tabs):
    t, d = x.shape
    wt = w_in.T.astype(BF16)
    tm = min(1024, t)
    col = lambda r: pl.BlockSpec((r, tm), lambda i: (0, i))
    gw = HEADS_PER_DIL * HEAD_DIM
    sub_specs, sub_shapes = [], []
    for _, dil in DIL_PATTERNS:
        assert tm % (16 * dil) == 0
        sub_specs += [pl.BlockSpec((tm // dil, dil * gw), lambda i: (i, 0))] * 3
        sub_shapes += [jax.ShapeDtypeStruct((t // dil, dil * gw), BF16)] * 3
    return pl.pallas_call(
        _inproj_b_kernel,
        grid=(t // tm,),
        in_specs=[pl.BlockSpec((tm, d), lambda i: (i, 0)), pl.BlockSpec((1, d), lambda i: (0, 0)),
                  pl.BlockSpec(wt.shape, lambda i: (0, 0)), col(H_HD), col(H_HD)],
        out_specs=sub_specs + [col(N_MEM_HEADS * HEAD_DIM)],
        out_shape=sub_shapes + [jax.ShapeDtypeStruct((N_MEM_HEADS * HEAD_DIM, t), BF16)],
        scratch_shapes=[pltpu.VMEM((2 * len(DIL_PATTERNS), gw, tm), F32),
                        pltpu.VMEM((3 * len(DIL_PATTERNS), gw // LANES, tm, LANES), F32)],
        compiler_params=_params("parallel"),
        name="inproj_b",
    )(x, g.reshape(1, d), wt, tabs[0], tabs[1])


TQ = 256
CR = 32
COARSE_BITS = 16
FINE_BITS = 17
BOUND_SLACK = 1.01
UNDERFLOW_GUARD = 2.0 ** -100


def _key_to_f32(key):
    bits = jnp.where(key < 0, key ^ jnp.int32(0x7FFFFFFF), key)
    return pltpu.bitcast(bits, F32)


def _dsa_kernel(q_ref, qi_ref, wi_ref, ki_ref, k_ref, va_ref, kn_ref, o_ref,
                sc_scr, sc16_scr, tau_scr, need_scr, tie_scr, tri_scr, bound_scr, m_scr, alpha_scr, acc_scr, s0_scr, s1_scr, p0_scr, p1_scr,
                *, seq, n_sel):
    i = pl.program_id(1)
    nch = i + 1
    krow = lax.broadcasted_iota(I32, (TQ, TQ), 0)
    qcol = lax.broadcasted_iota(I32, (TQ, TQ), 1)

    def chunk_off(c):
        return pl.multiple_of(c * TQ, TQ)

    def score_chunk(c, diag):
        off = chunk_off(c)
        kic = ki_ref[pl.ds(off, TQ), :]
        sc = jnp.zeros((TQ, TQ), F32)
        for h in range(IDX_HEADS):
            lg = _dot(kic, qi_ref[h * IDX_DIM:(h + 1) * IDX_DIM, :])
            sc = sc + jnp.maximum(lg, 0.0) * wi_ref[h:h + 1, :]
        if diag:
            sc = jnp.where(krow > qcol, -jnp.inf, sc)
        sc_scr[pl.ds(off, TQ), :] = sc
        sc16_scr[pl.ds(off, TQ), :] = sc.astype(BF16)

    def score_pair(t, carry):
        score_chunk(2 * t, False)
        score_chunk(2 * t + 1, False)
        return carry

    lax.fori_loop(0, i // 2, score_pair, 0)

    @pl.when(i % 2 == 1)
    def _():
        score_chunk(i - 1, False)

    score_chunk(i, True)

    def over_chunks(one, init):
        acc = lax.fori_loop(0, nch // 2, lambda t, a: one(2 * t + 1, one(2 * t, a)), init)
        return lax.cond(nch % 2 == 1, lambda a: one(nch - 1, a), lambda a: a, acc)

    def count(pred):
        def one(c, acc):
            off = chunk_off(c)
            ind = pred(sc_scr[pl.ds(off, TQ), :], off)
            return acc + jnp.sum(ind.reshape(TQ // CR, CR, TQ), axis=0)
        return jnp.sum(over_chunks(one, jnp.zeros((CR, TQ), F32)), axis=0, keepdims=True)

    @pl.when(i * TQ < n_sel)
    def _():
        tau_scr[...] = jnp.full((1, TQ), -jnp.inf, F32)
        need_scr[...] = jnp.zeros((1, TQ), F32)

    @pl.when(i * TQ >= n_sel)
    def _():
        def body16(step, tau16):
            cand = tau16 + jnp.left_shift(jnp.int32(1), COARSE_BITS - 1 - step)
            bits = jnp.where(cand < 0, cand ^ jnp.int32(0x7FFF), cand)
            cand_f = pltpu.bitcast(jnp.left_shift(bits, COARSE_BITS), F32).astype(BF16)

            def one(c, acc):
                ind = jnp.where(sc16_scr[pl.ds(chunk_off(c), TQ), :] >= cand_f,
                                jnp.bfloat16(1.0), jnp.bfloat16(0.0))
                return functools.reduce(lambda a, k: a + ind[k * CR:(k + 1) * CR], range(TQ // CR), acc)
            acc = over_chunks(one, jnp.zeros((CR, TQ), BF16))
            ok = jnp.sum(acc.astype(F32), axis=0, keepdims=True) >= float(n_sel)
            return jnp.where(ok, cand, tau16)

        tau16 = lax.fori_loop(0, COARSE_BITS, body16, jnp.full((1, TQ), -(1 << (COARSE_BITS - 1)), I32))

        key16 = jnp.left_shift(tau16, COARSE_BITS) + jnp.where(tau16 < 0, (1 << COARSE_BITS) - 1, 0)
        lo = key16 - ((1 << (COARSE_BITS - 1)) + 1)

        def body(step, carry):
            delta, cge, crej = carry
            cand = delta + jnp.left_shift(jnp.int32(1), FINE_BITS - 1 - step)
            cand_f = _key_to_f32(lo + cand)
            cnt = count(lambda blk, _: jnp.where(blk >= cand_f, 1.0, 0.0))
            ok = cnt >= float(n_sel)
            return jnp.where(ok, cand, delta), jnp.where(ok, cnt, cge), jnp.where(ok, crej, cnt)

        init = (jnp.zeros((1, TQ), I32), jnp.full((1, TQ), 2.0 * n_sel, F32), jnp.zeros((1, TQ), F32))
        delta, cge, crej = lax.fori_loop(0, FINE_BITS, body, init)
        tau = lo + delta
        tau_scr[...] = _key_to_f32(tau)
        need_scr[...] = jnp.where(cge > float(n_sel), float(n_sel) - crej, 2.0 * seq)

    tri_scr[...] = jnp.where(krow >= qcol, 1.0, 0.0).astype(BF16)
    grp = N_HEADS_A // N_KV_A
    tau_f = tau_scr[...]
    need = need_scr[...]

    def chunk_select(c):
        blk = sc_scr[pl.ds(chunk_off(c), TQ), :]
        eq = blk == tau_f
        rank = _dot(tri_scr[...], jnp.where(eq, 1.0, 0.0).astype(BF16)) + tie_scr[...]
        tie_scr[...] = rank[TQ - 1:TQ, :]
        return jnp.where(eq, jnp.where(rank <= need, 1.0, 0.0), jnp.where(blk > tau_f, 1.0, 0.0))

    kmax2 = jnp.max(kn_ref[...], axis=1, keepdims=True)
    for h in range(N_HEADS_A):
        qf = q_ref[h * HEAD_DIM:(h + 1) * HEAD_DIM, :].astype(F32)
        qn2 = jnp.sum(qf * qf, axis=0, keepdims=True)
        bound_scr[h] = jnp.sqrt(qn2 * kmax2[h // grp:h // grp + 1]) * BOUND_SLACK
    acc_scr[...] = jnp.zeros(acc_scr.shape, F32)
    tie_scr[...] = jnp.zeros((1, TQ), F32)

    def pipeline(stage_a, stage_b, buf0, buf1):
        def step(c, src, dst):
            stage_b(c - 1, src)
            stage_a(c, dst)

        stage_a(0, buf0)

        def pair_body(t, carry):
            step(2 * t + 1, buf0, buf1)
            step(2 * t + 2, buf1, buf0)
            return carry

        lax.fori_loop(0, (nch - 1) // 2, pair_body, 0)

        @pl.when((nch - 1) % 2 == 1)
        def _():
            step(nch - 1, buf0, buf1)
            stage_b(nch - 1, buf1)

        @pl.when((nch - 1) % 2 == 0)
        def _():
            stage_b(nch - 1, buf0)

    def fast_a(c, p_dst):
        off = chunk_off(c)
        sel = chunk_select(c).astype(BF16)
        for g in range(N_KV_A):
            kc = k_ref[pl.ds(off, TQ), g * HEAD_DIM:(g + 1) * HEAD_DIM]
            for j in range(grp):
                h = g * grp + j
                s = _dot(kc, q_ref[h * HEAD_DIM:(h + 1) * HEAD_DIM, :])
                p_dst[h] = jnp.exp2(s - bound_scr[h]).astype(BF16) * sel

    def fast_b(c, p_src):
        off = chunk_off(c)
        for g in range(N_KV_A):
            vt = va_ref[g * VR:(g + 1) * VR, pl.ds(off, TQ)]
            for j in range(grp):
                h = g * grp + j
                acc_scr[h] += _dot(vt, p_src[h])

    pipeline(fast_a, fast_b, p0_scr, p1_scr)
    lmin = functools.reduce(jnp.minimum, [acc_scr[h, HEAD_DIM:HEAD_DIM + 1, :] for h in range(N_HEADS_A)])

    def stage_a(c, s_dst):
        off = chunk_off(c)
        bias = (chunk_select(c) - 1.0) * (-NEG)
        for g in range(N_KV_A):
            kc = k_ref[pl.ds(off, TQ), g * HEAD_DIM:(g + 1) * HEAD_DIM]
            for j in range(grp):
                h = g * grp + j
                s = _dot(kc, q_ref[h * HEAD_DIM:(h + 1) * HEAD_DIM, :]) + bias
                s_dst[h] = s
                m_old = m_scr[h]
                m_new = jnp.maximum(m_old, jnp.max(s, axis=0, keepdims=True))
                alpha_scr[h] = jnp.exp2(m_old - m_new)
                m_scr[h] = m_new

    def stage_b(c, s_src):
        off = chunk_off(c)
        for g in range(N_KV_A):
            vt = va_ref[g * VR:(g + 1) * VR, pl.ds(off, TQ)]
            for j in range(grp):
                h = g * grp + j
                p = jnp.exp2(s_src[h] - m_scr[h]).astype(BF16)
                acc_scr[h] = alpha_scr[h] * acc_scr[h] + _dot(vt, p)

    @pl.when(jnp.logical_not(jnp.min(lmin) > UNDERFLOW_GUARD))
    def _():
        m_scr[...] = jnp.full(m_scr.shape, NEG, F32)
        acc_scr[...] = jnp.zeros(acc_scr.shape, F32)
        tie_scr[...] = jnp.zeros((1, TQ), F32)
        pipeline(stage_a, stage_b, s0_scr, s1_scr)

    for h in range(N_HEADS_A):
        a = acc_scr[h]
        o_ref[h * HEAD_DIM:(h + 1) * HEAD_DIM, :] = (a[:HEAD_DIM] / a[HEAD_DIM:HEAD_DIM + 1]).astype(BF16)


def _dsa_attention(qt, qit, wit, ki, k, vat, kn, b, s):
    n_sel = min(TOPK_MAX, s // 4)
    assert s % TQ == 0 and n_sel % TQ == 0
    nq = s // TQ
    qblk = lambda r: pl.BlockSpec((r, TQ), lambda bi, i: (0, bi * nq + i))
    tok = lambda c: pl.BlockSpec((s, c), lambda bi, i: (bi, 0))
    return pl.pallas_call(
        functools.partial(_dsa_kernel, seq=s, n_sel=n_sel),
        grid=(b, nq),
        in_specs=[qblk(qt.shape[0]), qblk(qit.shape[0]), qblk(wit.shape[0]),
                  tok(ki.shape[1]), tok(k.shape[1]),
                  pl.BlockSpec((vat.shape[0], s), lambda bi, i: (0, bi)),
                  pl.BlockSpec((kn.shape[0], s), lambda bi, i: (0, bi))],
        out_specs=qblk(qt.shape[0]),
        out_shape=jax.ShapeDtypeStruct(qt.shape, BF16),
        scratch_shapes=[
            pltpu.VMEM((s, TQ), F32),
            pltpu.VMEM((s, TQ), BF16),
            pltpu.VMEM((1, TQ), F32),
            pltpu.VMEM((1, TQ), F32),
            pltpu.VMEM((1, TQ), F32),
            pltpu.VMEM((TQ, TQ), BF16),
            pltpu.VMEM((N_HEADS_A, 1, TQ), F32),
            pltpu.VMEM((N_HEADS_A, 1, TQ), F32),
            pltpu.VMEM((N_HEADS_A, 1, TQ), F32),
            pltpu.VMEM((N_HEADS_A, VR, TQ), F32),
            pltpu.VMEM((N_HEADS_A, TQ, TQ), F32),
            pltpu.VMEM((N_HEADS_A, TQ, TQ), F32),
            pltpu.VMEM((N_HEADS_A, TQ, TQ), BF16),
            pltpu.VMEM((N_HEADS_A, TQ, TQ), BF16),
        ],
        compiler_params=_params("parallel", "arbitrary"),
        name="dsa_attention",
    )(qt, qit, wit, ki, k, vat, kn)


def _mem_attn_kernel(q_ref, k_ref, v_ref, o_ref, s_scr, m_scr):
    nsub = q_ref.shape[1] // MEM_SUB

    def stage_a(sb):
        ts = slice(sb * MEM_SUB, (sb + 1) * MEM_SUB)
        for h in range(N_MEM_HEADS):
            sl = slice(h * HEAD_DIM, (h + 1) * HEAD_DIM)
            s = _dot(k_ref[:, sl], q_ref[sl, ts])
            s_scr[sb, h] = s
            m_scr[sb, h] = jnp.max(s, axis=0, keepdims=True)

    def stage_b(sb):
        ts = slice(sb * MEM_SUB, (sb + 1) * MEM_SUB)
        for h in range(N_MEM_HEADS):
            sl = slice(h * HEAD_DIM, (h + 1) * HEAD_DIM)
            p = jnp.exp(s_scr[sb, h] - m_scr[sb, h])
            l = jnp.sum(p, axis=0, keepdims=True)
            o_ref[sl, ts] = (_dot(v_ref[sl, :], p.astype(BF16)) / l).astype(BF16)

    stage_a(0)
    for sb in range(1, nsub):
        stage_b(sb - 1)
        stage_a(sb)
    stage_b(nsub - 1)


MEM_SUB = 512


def _mem_attention(qmt, km, vmt, s):
    c, t = qmt.shape
    m = km.shape[1]
    tm = min(2048, s)
    assert tm % MEM_SUB == 0
    nq = s // tm
    blk = pl.BlockSpec((c, tm), lambda bi, i: (0, bi * nq + i))
    return pl.pallas_call(
        _mem_attn_kernel,
        grid=(t // s, nq),
        in_specs=[blk,
                  pl.BlockSpec((None, m, km.shape[2]), lambda bi, i: (bi, 0, 0)),
                  pl.BlockSpec((None, vmt.shape[1], m), lambda bi, i: (bi, 0, 0))],
        out_specs=blk,
        out_shape=jax.ShapeDtypeStruct((c, t), BF16),
        scratch_shapes=[pltpu.VMEM((tm // MEM_SUB, N_MEM_HEADS, m, MEM_SUB), F32),
                        pltpu.VMEM((tm // MEM_SUB, N_MEM_HEADS, 1, MEM_SUB), F32)],
        compiler_params=_params("parallel", "parallel"),
        name="mem_attention",
    )(qmt, km, vmt)


QB = 1024
BAND_QUERIES = 2048
LN2 = 0.6931471805599453


def _band_kernel(q_ref, kp_ref, kc_ref, vp_ref, vc_ref, o_ref, lse_ref, s_scr, m_scr, ot_scr, lt_scr, *, qb, ns):
    j = pl.program_id(1)
    nsub = qb // BLK
    krow = lax.broadcasted_iota(I32, (BLK, BLK), 0)
    qcol = lax.broadcasted_iota(I32, (BLK, BLK), 1)
    bias_prev = jnp.where(krow >= qcol, 0.0, NEG)
    bias_cur = jnp.where(krow <= qcol, 0.0, NEG)
    no_prev = jnp.where(j > 0, 0.0, NEG)
    lt_scr[...] = jnp.zeros(lt_scr.shape, F32)

    gw = HEADS_PER_DIL * HEAD_DIM

    def stage_a(u):
        sq, sb = divmod(u, nsub)
        qs = slice(sb * BLK, (sb + 1) * BLK)
        for h in range(HEADS_PER_DIL):
            hs = slice(sq * gw + h * HEAD_DIM, sq * gw + (h + 1) * HEAD_DIM)
            qh = q_ref[qs, hs]
            if sb == 0:
                s_p = _dot_nt(kp_ref[:, hs], qh) + (bias_prev + no_prev)
            else:
                s_p = _dot_nt(kc_ref[(sb - 1) * BLK:sb * BLK, hs], qh) + bias_prev
            s_c = _dot_nt(kc_ref[qs, hs], qh) + bias_cur
            s_scr[u, h, 0:BLK] = s_p
            s_scr[u, h, BLK:2 * BLK] = s_c
            m_scr[u, h] = jnp.maximum(jnp.max(s_p, axis=0, keepdims=True), jnp.max(s_c, axis=0, keepdims=True))

    def stage_b(u):
        sq, sb = divmod(u, nsub)
        qs = slice(sb * BLK, (sb + 1) * BLK)
        for h in range(HEADS_PER_DIL):
            hs = slice(sq * gw + h * HEAD_DIM, sq * gw + (h + 1) * HEAD_DIM)
            m = m_scr[u, h]
            p_p = jnp.exp2(s_scr[u, h, 0:BLK] - m)
            p_c = jnp.exp2(s_scr[u, h, BLK:2 * BLK] - m)
            l = jnp.sum(p_p, axis=0, keepdims=True) + jnp.sum(p_c, axis=0, keepdims=True)
            v_p = vp_ref[:, hs] if sb == 0 else vc_ref[(sb - 1) * BLK:sb * BLK, hs]
            acc = _dot_tn(v_p, p_p.astype(BF16)) + _dot_tn(vc_ref[qs, hs], p_c.astype(BF16))
            ot_scr[sq, h * HEAD_DIM:(h + 1) * HEAD_DIM, qs] = acc / l
            lt_scr[sq, h:h + 1, qs] = m * LN2 + jnp.log(l)

    units = ns * nsub
    groups = [range(g, min(g + 2, units)) for g in range(0, units, 2)]
    for u in groups[0]:
        stage_a(u)
    for prev, nxt in zip(groups[:-1], groups[1:]):
        for u in prev:
            stage_b(u)
        for u in nxt:
            stage_a(u)
    for u in groups[-1]:
        stage_b(u)
    for sq in range(ns):
        o_ref[:, sq * gw:(sq + 1) * gw] = ot_scr[sq].T.astype(o_ref.dtype)
        lse_ref[:, sq * LANES:(sq + 1) * LANES] = lt_scr[sq].T


def _band_attention(q, k, v, b, dil):
    rows, width = q.shape
    c = width // dil
    t = rows * dil
    n = t // (b * dil)
    qb = min(QB, n)
    assert n % qb == 0 and qb % BLK == 0
    rr, nj = qb // BLK, n // qb
    ns = min(dil, max(1, BAND_QUERIES // qb))
    assert dil % ns == 0
    gpb = dil // ns
    cur = lambda w: pl.BlockSpec((qb, ns * w), lambda sg, j: ((sg // gpb) * nj + j, sg % gpb))
    prev = pl.BlockSpec((BLK, ns * c),
                        lambda sg, j: ((sg // gpb) * (n // BLK) + jnp.maximum(j * rr - 1, 0), sg % gpb))
    return pl.pallas_call(
        functools.partial(_band_kernel, qb=qb, ns=ns),
        grid=(b * gpb, nj),
        in_specs=[cur(c), prev, cur(c), prev, cur(c)],
        out_specs=[cur(c), cur(LANES)],
        out_shape=[jax.ShapeDtypeStruct((rows, dil * c), BF16), jax.ShapeDtypeStruct((rows, dil * LANES), F32)],
        scratch_shapes=[pltpu.VMEM((ns * rr, HEADS_PER_DIL, 2 * BLK, BLK), F32),
                        pltpu.VMEM((ns * rr, HEADS_PER_DIL, 1, BLK), F32),
                        pltpu.VMEM((ns, c, qb), F32),
                        pltpu.VMEM((ns, LANES, qb), F32)],
        compiler_params=_params("parallel", "parallel"),
        name="band_attention",
    )(q, k, k, v, v)


def _merge_kernel(*refs):
    ng = len(DIL_PATTERNS)
    o_refs, l_refs, out_ref, o_scr, l_scr = refs[:ng], refs[ng:2 * ng], refs[2 * ng], refs[2 * ng + 1], refs[2 * ng + 2]
    tm, gw = out_ref.shape
    os_, lses = [], []
    for g, (_, dil) in enumerate(DIL_PATTERNS):
        if dil == 1:
            os_.append([o_refs[g][:, cb * LANES:(cb + 1) * LANES].astype(F32) for cb in range(gw // LANES)])
            lses.append(l_refs[g][...])
            continue
        for r in range(dil):
            rows = pl.ds(r, tm // dil, stride=dil)
            for cb in range(gw // LANES):
                o_scr[g, cb, rows, :] = o_refs[g][:, r * gw + cb * LANES:r * gw + (cb + 1) * LANES].astype(F32)
            l_scr[g, rows, :] = l_refs[g][:, r * LANES:(r + 1) * LANES]
        os_.append([o_scr[g, cb] for cb in range(gw // LANES)])
        lses.append(l_scr[g])
    m = functools.reduce(jnp.maximum, lses)
    es = [jnp.exp(l - m) for l in lses]
    den = sum(es)
    spread = jnp.where(lax.broadcasted_iota(I32, (LANES, gw), 1) // HEAD_DIM
                       == lax.broadcasted_iota(I32, (LANES, gw), 0), 1.0, 0.0).astype(BF16)
    ws = [_dot((e / den).astype(BF16), spread) for e in es]
    for cb in range(gw // LANES):
        cs = slice(cb * LANES, (cb + 1) * LANES)
        out_ref[:, cs] = sum(w[:, cs] * o[cb] for w, o in zip(ws, os_)).astype(out_ref.dtype)


def _merge_groups(os_, lses, t):
    gw = HEADS_PER_DIL * HEAD_DIM
    tm = min(1024, t)
    ng = len(DIL_PATTERNS)
    spec = lambda w: [pl.BlockSpec((tm // dil, dil * w), lambda i: (i, 0)) for _, dil in DIL_PATTERNS]
    return pl.pallas_call(
        _merge_kernel,
        grid=(t // tm,),
        in_specs=spec(gw) + spec(LANES),
        out_specs=pl.BlockSpec((tm, gw), lambda i: (i, 0)),
        out_shape=jax.ShapeDtypeStruct((t, gw), BF16),
        scratch_shapes=[pltpu.VMEM((ng, gw // LANES, tm, LANES), F32), pltpu.VMEM((ng, tm, LANES), F32)],
        compiler_params=_params("parallel"),
        name="merge_groups",
    )(*os_, *lses)


def _ffn_kernel(x_ref, mix_ref, mo_ref, wo1_ref, wo2_ref, g_ref, wgu_ref, wd_ref, gf_ref,
                o_ref, act_scr, *, final_norm, tf, mix_token_major):
    dff = wd_ref.shape[0]
    mixed = _dot(mix_ref[...], wo1_ref[...]) if mix_token_major else _dot_tn(mix_ref[...], wo1_ref[...])
    x2 = x_ref[...] + mixed + _dot_tn(mo_ref[...], wo2_ref[...])
    h = _rms(x2, g_ref[...]).astype(BF16)
    for f in range(dff // tf):
        gate = _dot(h, wgu_ref[:, f * tf:(f + 1) * tf])
        up = _dot(h, wgu_ref[:, dff + f * tf:dff + (f + 1) * tf])
        act_scr[:, f * tf:(f + 1) * tf] = (gate * jax.nn.sigmoid(gate) * up).astype(BF16)
    y = x2 + _dot(act_scr[...], wd_ref[...])
    if final_norm:
        y = _rms(y, gf_ref[...])
    o_ref[...] = y


def _out_ffn(x, mix, mo, w_out, g_ffn, w_gate_up, w_down, g_final, final_norm):
    t, d = x.shape
    mix_token_major = mix.shape[0] == t
    cm, cmo = mix.shape[1 if mix_token_major else 0], mo.shape[0]
    dff = w_down.shape[0]
    wo1 = w_out[:cm].astype(BF16)
    wo2 = w_out[cm:].astype(BF16)
    wgu = w_gate_up.astype(BF16)
    wd = w_down.astype(BF16)
    tm = min(512, t)
    tf = 256 if dff % 256 == 0 else dff
    row = lambda c: pl.BlockSpec((tm, c), lambda i: (i, 0))
    const = lambda r, c: pl.BlockSpec((r, c), lambda i: (0, 0), pipeline_mode=pl.Buffered(1))
    return pl.pallas_call(
        functools.partial(_ffn_kernel, final_norm=final_norm, tf=tf, mix_token_major=mix_token_major),
        grid=(t // tm,),
        in_specs=[row(d), row(cm) if mix_token_major else pl.BlockSpec((cm, tm), lambda i: (0, i)),
                  pl.BlockSpec((cmo, tm), lambda i: (0, i)),
                  const(cm, d), const(cmo, d), const(1, d),
                  const(d, 2 * dff), const(dff, d), const(1, d)],
        out_specs=row(d),
        out_shape=jax.ShapeDtypeStruct((t, d), F32),
        scratch_shapes=[pltpu.VMEM((tm, dff), BF16)],
        compiler_params=_params("parallel"),
        name="out_ffn",
    )(x, mix, mo, wo1, wo2, g_ffn.reshape(1, d), wgu, wd, g_final.reshape(1, d))


def kernel(x, mem, positions,
           l0_norm_mix, l0_norm_mem, l0_w_in, l0_w_mem_kv, l0_w_out, l0_norm_ffn, l0_w_gate_up, l0_w_down,
           l1_norm_mix, l1_norm_mem, l1_w_in, l1_w_mem_kv, l1_w_out, l1_norm_ffn, l1_w_gate_up, l1_w_down,
           final_norm):
    b, s, d = x.shape
    t = b * s
    xt = x.reshape(t, d)
    tabs = _trig_tables(positions.reshape(1, t))

    qt, k, vat, qit, ki, wit, qmt, kn = _inproj_a(xt, l0_norm_mix, l0_w_in, tabs)
    mix = _dsa_attention(qt, qit, wit, ki, k, vat, kn, b, s)
    mo = _mem_attention(qmt, *_mem_kv(mem, l0_norm_mem, l0_w_mem_kv), s)
    xt = _out_ffn(xt, mix, mo, l0_w_out, l0_norm_ffn, l0_w_gate_up, l0_w_down, final_norm, False)

    outs = _inproj_b(xt, l1_norm_mix, l1_w_in, tabs)
    os_, lses = [], []
    for g, (window, dil) in enumerate(DIL_PATTERNS):
        assert window // dil == BLK
        o, lse = _band_attention(*outs[3 * g:3 * g + 3], b, dil)
        os_.append(o)
        lses.append(lse)
    mix = _merge_groups(os_, lses, t)
    mo = _mem_attention(outs[-1], *_mem_kv(mem, l1_norm_mem, l1_w_mem_kv), s)
    xt = _out_ffn(xt, mix, mo, l1_w_out, l1_norm_ffn, l1_w_gate_up, l1_w_down, final_norm, True)
    return xt.reshape(b, s, d)
```

```python
import functools

import jax
import jax.numpy as jnp
from jax import lax
from jax.experimental import pallas as pl
from jax.experimental.pallas import tpu as pltpu

F32 = jnp.float32
BF16 = jnp.bfloat16
I32 = jnp.int32

HEAD_DIM = 64
N_HEADS_A = 12
N_KV_A = 4
IDX_HEADS = 8
IDX_DIM = 64
IDX_ROPE_DIM = 32
TOPK_MAX = 256
DIL_PATTERNS = ((128, 1), (512, 4), (2048, 16))
HEADS_PER_DIL = 4
N_MEM_HEADS = 4
BLK = 128
ROPE_THETA = 10000.0
EPS = 1e-6
NEG = -1e30

LANES = 128
VMEM_LIMIT = 56 * 1024 * 1024

Q_SCALE = HEAD_DIM ** -0.5
WI_SCALE = IDX_HEADS ** -0.5 * IDX_DIM ** -0.5
LOG2E = 1.4426950408889634


def _dot(a, b):
    return jnp.dot(a, b, preferred_element_type=F32)


def _dot_nt(a, b):
    return lax.dot_general(a, b, (((1,), (1,)), ((), ())), preferred_element_type=F32)


def _dot_tn(a, b):
    return lax.dot_general(a, b, (((0,), (0,)), ((), ())), preferred_element_type=F32)


def _params(*sem):
    return pltpu.CompilerParams(dimension_semantics=sem, vmem_limit_bytes=VMEM_LIMIT)


def _rms(x, g):
    ms = jnp.mean(x * x, axis=-1, keepdims=True)
    return x * lax.rsqrt(ms + EPS) * g


H_HD = HEAD_DIM // 2
H_IX = IDX_ROPE_DIM // 2


def _trig_kernel(pos_ref, f_ref, chd_ref, shd_ref, cix_ref, six_ref):
    tm = pos_ref.shape[1]
    pos = pos_ref[...].astype(F32)
    f = jnp.concatenate([f_ref[...]] * (tm // LANES), axis=1)
    ang = f * pos
    c, s = jnp.cos(ang), jnp.sin(ang)
    chd_ref[...] = c[:H_HD]
    shd_ref[...] = s[:H_HD]
    cix_ref[...] = c[H_HD:]
    six_ref[...] = s[H_HD:]


def _trig_tables(pos_row):
    t = pos_row.shape[1]
    tm = min(2048, t)
    f_hd = ROPE_THETA ** (-jnp.arange(H_HD, dtype=F32) / H_HD)
    f_ix = ROPE_THETA ** (-jnp.arange(H_IX, dtype=F32) / H_IX)
    f = jnp.broadcast_to(jnp.concatenate([f_hd, f_ix])[:, None], (H_HD + H_IX, LANES))
    spec = lambda r: pl.BlockSpec((r, tm), lambda i: (0, i))
    rows = [H_HD, H_HD, H_IX, H_IX]
    return pl.pallas_call(
        _trig_kernel,
        grid=(t // tm,),
        in_specs=[spec(1), pl.BlockSpec((H_HD + H_IX, LANES), lambda i: (0, 0))],
        out_specs=[spec(r) for r in rows],
        out_shape=[jax.ShapeDtypeStruct((r, t), F32) for r in rows],
        compiler_params=_params("parallel"),
        name="rope_tables",
    )(pos_row, f)


def _norm_matmul_kernel(x_ref, g_ref, w_ref, o_ref):
    h = _rms(x_ref[...], g_ref[...]).astype(BF16)
    o_ref[...] = _dot(h, w_ref[...]).astype(o_ref.dtype)


def _norm_matmul(x, g, w, out_dtype):
    t, d = x.shape
    n = w.shape[1]
    tm = min(512, t)
    return pl.pallas_call(
        _norm_matmul_kernel,
        grid=(t // tm,),
        in_specs=[pl.BlockSpec((tm, d), lambda i: (i, 0)),
                  pl.BlockSpec((1, d), lambda i: (0, 0)),
                  pl.BlockSpec((d, n), lambda i: (0, 0))],
        out_specs=pl.BlockSpec((tm, n), lambda i: (i, 0)),
        out_shape=jax.ShapeDtypeStruct((t, n), out_dtype),
        compiler_params=_params("parallel"),
        name="norm_matmul",
    )(x, g.reshape(1, d), w)


def _mem_kv(mem, g, w_kv):
    b, m, d = mem.shape
    kv = _norm_matmul(mem.reshape(b * m, d), g, w_kv.astype(BF16), BF16).reshape(b, m, -1)
    nk = N_MEM_HEADS * HEAD_DIM
    return kv[:, :, :nk], kv[:, :, nk:].transpose(0, 2, 1)


VR = 80

A_Q, A_K, A_V, A_QI, A_KI, A_WI, A_QM, A_END = 0, 768, 1024, 1280, 1792, 1856, 1872, 2128


def _rope_heads(p, nheads, out_ref, half, c, s, scale):
    for hh in range(nheads):
        r0 = hh * HEAD_DIM
        x1, x2 = p[r0:r0 + half], p[r0 + half:r0 + 2 * half]
        out_ref[r0:r0 + half, :] = ((x1 * c - x2 * s) * scale).astype(out_ref.dtype)
        out_ref[r0 + half:r0 + 2 * half, :] = ((x2 * c + x1 * s) * scale).astype(out_ref.dtype)
        if 2 * half < HEAD_DIM:
            out_ref[r0 + 2 * half:r0 + HEAD_DIM, :] = (p[r0 + 2 * half:r0 + HEAD_DIM] * scale).astype(out_ref.dtype)


def _write_values(pv, nheads, va_ref):
    tm = pv.shape[1]
    ones_rows = jnp.where(lax.broadcasted_iota(I32, (VR - HEAD_DIM, tm), 0) == 0, 1.0, 0.0).astype(BF16)
    for g in range(nheads):
        va_ref[g * VR:g * VR + HEAD_DIM, :] = pv[g * HEAD_DIM:(g + 1) * HEAD_DIM].astype(BF16)
        va_ref[g * VR + HEAD_DIM:(g + 1) * VR, :] = ones_rows


def _inproj_a_kernel(x_ref, g_ref, wt_ref, chd_ref, shd_ref, cix_ref, six_ref, km_ref, vm_ref,
                     q_ref, k_ref, va_ref, qi_ref, ki_ref, wi_ref, mo_ref, kn_ref, kt_scr, ms_scr, mm_scr):
    h = _rms(x_ref[...], g_ref[...]).astype(BF16)
    chd, shd = chd_ref[...], shd_ref[...]
    cix, six = cix_ref[...], six_ref[...]

    def proj(a, b):
        return _dot_nt(wt_ref[a:b, :], h)

    _rope_heads(proj(A_Q, A_K), N_HEADS_A, q_ref, H_HD, chd, shd, Q_SCALE * LOG2E)
    nk = N_KV_A * HEAD_DIM
    _rope_heads(proj(A_K, A_V), N_KV_A, kt_scr.at[0:nk], H_HD, chd, shd, 1.0)
    _write_values(proj(A_V, A_QI), N_KV_A, va_ref)
    _rope_heads(proj(A_QI, A_KI), IDX_HEADS, qi_ref, H_IX, cix, six, 1.0)
    pkw = proj(A_KI, A_QM)
    _rope_heads(pkw, 1, kt_scr.at[nk:nk + IDX_DIM], H_IX, cix, six, 1.0)
    k_ref[...] = kt_scr[0:nk, :].T.astype(BF16)
    ki_ref[...] = kt_scr[nk:nk + IDX_DIM, :].T.astype(BF16)
    for g in range(N_KV_A):
        kg = kt_scr[g * HEAD_DIM:(g + 1) * HEAD_DIM, :]
        kn_ref[g:g + 1, :] = jnp.sum(kg * kg, axis=0, keepdims=True)
    wi_ref[...] = pkw[IDX_DIM:IDX_DIM + IDX_HEADS] * WI_SCALE
    _mem_attend((proj(A_QM, A_END) * Q_SCALE).astype(BF16), km_ref, vm_ref, mo_ref, ms_scr, mm_scr)


def _mem_specs(km, vmt, tm, s):
    assert s % tm == 0 and tm % MEM_SUB == 0
    m = km.shape[1]
    specs = [pl.BlockSpec((None, m, km.shape[2]), lambda i: (i * tm // s, 0, 0)),
             pl.BlockSpec((None, vmt.shape[1], m), lambda i: (i * tm // s, 0, 0))]
    scratch = [pltpu.VMEM((tm // MEM_SUB, N_MEM_HEADS, m, MEM_SUB), F32),
               pltpu.VMEM((tm // MEM_SUB, N_MEM_HEADS, 1, MEM_SUB), F32)]
    return specs, scratch


def _inproj_a(x, g, w_in, tabs, km, vmt, s):
    t, d = x.shape
    wt = w_in.T
    pad = jnp.zeros((A_QM - A_WI - IDX_HEADS, d), w_in.dtype)
    split = A_WI + IDX_HEADS
    wt = jnp.concatenate([wt[:split], pad, wt[split:]], axis=0).astype(BF16)
    tm = min(1024, s)
    mem_specs, mem_scratch = _mem_specs(km, vmt, tm, s)
    col = lambda r: pl.BlockSpec((r, tm), lambda i: (0, i))
    outs = [(N_HEADS_A * HEAD_DIM, BF16), (N_KV_A * HEAD_DIM, BF16), (N_KV_A * VR, BF16),
            (IDX_HEADS * IDX_DIM, BF16), (IDX_DIM, BF16), (IDX_HEADS, F32), (N_MEM_HEADS * HEAD_DIM, BF16),
            (N_KV_A, F32)]
    return pl.pallas_call(
        _inproj_a_kernel,
        grid=(t // tm,),
        in_specs=[pl.BlockSpec((tm, d), lambda i: (i, 0)), pl.BlockSpec((1, d), lambda i: (0, 0)),
                  pl.BlockSpec((A_END, d), lambda i: (0, 0)),
                  col(H_HD), col(H_HD), col(H_IX), col(H_IX)] + mem_specs,
        out_specs=[pl.BlockSpec((tm, r), lambda i: (i, 0)) if k in (1, 4) else col(r)
                   for k, (r, _) in enumerate(outs)],
        out_shape=[jax.ShapeDtypeStruct((t, r) if k in (1, 4) else (r, t), dt) for k, (r, dt) in enumerate(outs)],
        scratch_shapes=[pltpu.VMEM((N_KV_A * HEAD_DIM + IDX_DIM, tm), F32)] + mem_scratch,
        compiler_params=_params("parallel"),
        name="inproj_a",
    )(x, g.reshape(1, d), wt, *tabs, km, vmt)


def _inproj_b_kernel(x_ref, g_ref, wt_ref, chd_ref, shd_ref, *refs):
    ng = len(DIL_PATTERNS)
    (km_ref, vm_ref), out_refs, (rope_scr, tok_scr, ms_scr, mm_scr) = refs[:2], refs[2:3 * ng + 3], refs[3 * ng + 3:]
    h = _rms(x_ref[...], g_ref[...]).astype(BF16)
    chd, shd = chd_ref[...], shd_ref[...]
    gw = HEADS_PER_DIL * HEAD_DIM
    tm = h.shape[0]

    def emit(out_ref, slot, dil, value_t):
        tok = value_t.T
        if dil == 1:
            out_ref[...] = tok.astype(BF16)
        else:
            for cb in range(gw // LANES):
                tok_scr[slot, cb] = tok[:, cb * LANES:(cb + 1) * LANES]
            for r in range(dil):
                for cb in range(gw // LANES):
                    out_ref[:, r * gw + cb * LANES:r * gw + (cb + 1) * LANES] = (
                        tok_scr[slot, cb, pl.ds(r, tm // dil, stride=dil), :].astype(BF16))

    for g, (_, dil) in enumerate(DIL_PATTERNS):
        q_ref, k_ref, v_ref = out_refs[3 * g:3 * g + 3]
        base = 3 * g * gw
        _rope_heads(_dot_nt(wt_ref[base:base + gw, :], h), HEADS_PER_DIL, rope_scr.at[0], H_HD, chd, shd,
                    Q_SCALE * LOG2E)
        emit(q_ref, 0, dil, rope_scr[0])
        _rope_heads(_dot_nt(wt_ref[base + gw:base + 2 * gw, :], h), HEADS_PER_DIL, rope_scr.at[1], H_HD, chd, shd, 1.0)
        emit(k_ref, 1, dil, rope_scr[1])
        emit(v_ref, 2, dil, _dot_nt(wt_ref[base + 2 * gw:base + 3 * gw, :], h))
    qm = (_dot_nt(wt_ref[3 * ng * gw:3 * ng * gw + N_MEM_HEADS * HEAD_DIM, :], h) * Q_SCALE).astype(BF16)
    _mem_attend(qm, km_ref, vm_ref, out_refs[3 * ng], ms_scr, mm_scr)


def _inproj_b(x, g, w_in, tabs, km, vmt, s):
    t, d = x.shape
    wt = w_in.T.astype(BF16)
    tm = min(1024, s)
    mem_specs, mem_scratch = _mem_specs(km, vmt, tm, s)
    col = lambda r: pl.BlockSpec((r, tm), lambda i: (0, i))
    gw = HEADS_PER_DIL * HEAD_DIM
    sub_specs, sub_shapes = [], []
    for _, dil in DIL_PATTERNS:
        assert tm % (16 * dil) == 0
        sub_specs += [pl.BlockSpec((tm // dil, dil * gw), lambda i: (i, 0))] * 3
        sub_shapes += [jax.ShapeDtypeStruct((t // dil, dil * gw), BF16)] * 3
    return pl.pallas_call(
        _inproj_b_kernel,
        grid=(t // tm,),
        in_specs=[pl.BlockSpec((tm, d), lambda i: (i, 0)), pl.BlockSpec((1, d), lambda i: (0, 0)),
                  pl.BlockSpec(wt.shape, lambda i: (0, 0)), col(H_HD), col(H_HD)] + mem_specs,
        out_specs=sub_specs + [col(N_MEM_HEADS * HEAD_DIM)],
        out_shape=sub_shapes + [jax.ShapeDtypeStruct((N_MEM_HEADS * HEAD_DIM, t), BF16)],
        scratch_shapes=[pltpu.VMEM((2, gw, tm), F32), pltpu.VMEM((3, gw // LANES, tm, LANES), F32)] + mem_scratch,
        compiler_params=_params("parallel"),
        name="inproj_b",
    )(x, g.reshape(1, d), wt, tabs[0], tabs[1], km, vmt)


TQ = 256
CR = 32
COARSE_BITS = 16
FINE_BITS = 17
BOUND_SLACK = 1.01
UNDERFLOW_GUARD = 2.0 ** -100


def _key_to_f32(key):
    bits = jnp.where(key < 0, key ^ jnp.int32(0x7FFFFFFF), key)
    return pltpu.bitcast(bits, F32)


def _dsa_kernel(q_ref, qi_ref, wi_ref, ki_ref, k_ref, va_ref, kn_ref, o_ref,
                sc_scr, sc16_scr, tau_scr, need_scr, tie_scr, tri_scr, bound_scr, m_scr, alpha_scr, acc_scr, s0_scr, s1_scr, p0_scr, p1_scr,
                *, seq, n_sel):
    i = pl.program_id(1)
    nch = i + 1
    krow = lax.broadcasted_iota(I32, (TQ, TQ), 0)
    qcol = lax.broadcasted_iota(I32, (TQ, TQ), 1)

    def chunk_off(c):
        return pl.multiple_of(c * TQ, TQ)

    def score_chunk(c, diag):
        off = chunk_off(c)
        kic = ki_ref[pl.ds(off, TQ), :]
        sc = jnp.zeros((TQ, TQ), F32)
        for h in range(IDX_HEADS):
            lg = _dot(kic, qi_ref[h * IDX_DIM:(h + 1) * IDX_DIM, :])
            sc = sc + jnp.maximum(lg, 0.0) * wi_ref[h:h + 1, :]
        if diag:
            sc = jnp.where(krow > qcol, -jnp.inf, sc)
        sc_scr[pl.ds(off, TQ), :] = sc
        sc16_scr[pl.ds(off, TQ), :] = sc.astype(BF16)

    def score_pair(t, carry):
        score_chunk(2 * t, False)
        score_chunk(2 * t + 1, False)
        return carry

    lax.fori_loop(0, i // 2, score_pair, 0)

    @pl.when(i % 2 == 1)
    def _():
        score_chunk(i - 1, False)

    score_chunk(i, True)

    def over_chunks(one, init):
        acc = lax.fori_loop(0, nch // 2, lambda t, a: one(2 * t + 1, one(2 * t, a)), init)
        return lax.cond(nch % 2 == 1, lambda a: one(nch - 1, a), lambda a: a, acc)

    def count(pred):
        def one(c, acc):
            off = chunk_off(c)
            ind = pred(sc_scr[pl.ds(off, TQ), :], off)
            return acc + jnp.sum(ind.reshape(TQ // CR, CR, TQ), axis=0)
        return jnp.sum(over_chunks(one, jnp.zeros((CR, TQ), F32)), axis=0, keepdims=True)

    @pl.when(i * TQ < n_sel)
    def _():
        tau_scr[...] = jnp.full((1, TQ), -jnp.inf, F32)
        need_scr[...] = jnp.zeros((1, TQ), F32)

    @pl.when(i * TQ >= n_sel)
    def _():
        def body16(step, tau16):
            cand = tau16 + jnp.left_shift(jnp.int32(1), COARSE_BITS - 1 - step)
            bits = jnp.where(cand < 0, cand ^ jnp.int32(0x7FFF), cand)
            cand_f = pltpu.bitcast(jnp.left_shift(bits, COARSE_BITS), F32).astype(BF16)

            def one(c, acc):
                ind = jnp.where(sc16_scr[pl.ds(chunk_off(c), TQ), :] >= cand_f,
                                jnp.bfloat16(1.0), jnp.bfloat16(0.0))
                return functools.reduce(lambda a, k: a + ind[k * CR:(k + 1) * CR], range(TQ // CR), acc)
            acc = over_chunks(one, jnp.zeros((CR, TQ), BF16))
            ok = jnp.sum(acc.astype(F32), axis=0, keepdims=True) >= float(n_sel)
            return jnp.where(ok, cand, tau16)

        tau16 = lax.fori_loop(0, COARSE_BITS, body16, jnp.full((1, TQ), -(1 << (COARSE_BITS - 1)), I32))

        key16 = jnp.left_shift(tau16, COARSE_BITS) + jnp.where(tau16 < 0, (1 << COARSE_BITS) - 1, 0)
        lo = key16 - ((1 << (COARSE_BITS - 1)) + 1)

        def body(step, carry):
            delta, cge, crej = carry
            cand = delta + jnp.left_shift(jnp.int32(1), FINE_BITS - 1 - step)
            cand_f = _key_to_f32(lo + cand)
            cnt = count(lambda blk, _: jnp.where(blk >= cand_f, 1.0, 0.0))
            ok = cnt >= float(n_sel)
            return jnp.where(ok, cand, delta), jnp.where(ok, cnt, cge), jnp.where(ok, crej, cnt)

        init = (jnp.zeros((1, TQ), I32), jnp.full((1, TQ), 2.0 * n_sel, F32), jnp.zeros((1, TQ), F32))
        delta, cge, crej = lax.fori_loop(0, FINE_BITS, body, init)
        tau = lo + delta
        tau_scr[...] = _key_to_f32(tau)
        need_scr[...] = jnp.where(cge > float(n_sel), float(n_sel) - crej, 2.0 * seq)

    tri_scr[...] = jnp.where(krow >= qcol, 1.0, 0.0).astype(BF16)
    grp = N_HEADS_A // N_KV_A
    tau_f = tau_scr[...]
    need = need_scr[...]

    def chunk_select(c):
        blk = sc_scr[pl.ds(chunk_off(c), TQ), :]
        eq = blk == tau_f
        rank = _dot(tri_scr[...], jnp.where(eq, 1.0, 0.0).astype(BF16)) + tie_scr[...]
        tie_scr[...] = rank[TQ - 1:TQ, :]
        return jnp.where(eq, jnp.where(rank <= need, 1.0, 0.0), jnp.where(blk > tau_f, 1.0, 0.0))

    kmax2 = jnp.max(kn_ref[...], axis=1, keepdims=True)
    for h in range(N_HEADS_A):
        qf = q_ref[h * HEAD_DIM:(h + 1) * HEAD_DIM, :].astype(F32)
        qn2 = jnp.sum(qf * qf, axis=0, keepdims=True)
        bound_scr[h] = jnp.sqrt(qn2 * kmax2[h // grp:h // grp + 1]) * BOUND_SLACK
    acc_scr[...] = jnp.zeros(acc_scr.shape, F32)
    tie_scr[...] = jnp.zeros((1, TQ), F32)

    def pipeline(stage_a, stage_b, buf0, buf1):
        def step(c, src, dst):
            stage_b(c - 1, src)
            stage_a(c, dst)

        stage_a(0, buf0)

        def pair_body(t, carry):
            step(2 * t + 1, buf0, buf1)
            step(2 * t + 2, buf1, buf0)
            return carry

        lax.fori_loop(0, (nch - 1) // 2, pair_body, 0)

        @pl.when((nch - 1) % 2 == 1)
        def _():
            step(nch - 1, buf0, buf1)
            stage_b(nch - 1, buf1)

        @pl.when((nch - 1) % 2 == 0)
        def _():
            stage_b(nch - 1, buf0)

    def fast_a(c, p_dst):
        off = chunk_off(c)
        sel = chunk_select(c).astype(BF16)
        for g in range(N_KV_A):
            kc = k_ref[pl.ds(off, TQ), g * HEAD_DIM:(g + 1) * HEAD_DIM]
            for j in range(grp):
                h = g * grp + j
                s = _dot(kc, q_ref[h * HEAD_DIM:(h + 1) * HEAD_DIM, :])
                p_dst[h] = jnp.exp2(s - bound_scr[h]).astype(BF16) * sel

    def fast_b(c, p_src):
        off = chunk_off(c)
        for g in range(N_KV_A):
            vt = va_ref[g * VR:(g + 1) * VR, pl.ds(off, TQ)]
            for j in range(grp):
                h = g * grp + j
                acc_scr[h] += _dot(vt, p_src[h])

    pipeline(fast_a, fast_b, p0_scr, p1_scr)
    lmin = functools.reduce(jnp.minimum, [acc_scr[h, HEAD_DIM:HEAD_DIM + 1, :] for h in range(N_HEADS_A)])

    def stage_a(c, s_dst):
        off = chunk_off(c)
        bias = (chunk_select(c) - 1.0) * (-NEG)
        for g in range(N_KV_A):
            kc = k_ref[pl.ds(off, TQ), g * HEAD_DIM:(g + 1) * HEAD_DIM]
            for j in range(grp):
                h = g * grp + j
                s = _dot(kc, q_ref[h * HEAD_DIM:(h + 1) * HEAD_DIM, :]) + bias
                s_dst[h] = s
                m_old = m_scr[h]
                m_new = jnp.maximum(m_old, jnp.max(s, axis=0, keepdims=True))
                alpha_scr[h] = jnp.exp2(m_old - m_new)
                m_scr[h] = m_new

    def stage_b(c, s_src):
        off = chunk_off(c)
        for g in range(N_KV_A):
            vt = va_ref[g * VR:(g + 1) * VR, pl.ds(off, TQ)]
            for j in range(grp):
                h = g * grp + j
                p = jnp.exp2(s_src[h] - m_scr[h]).astype(BF16)
                acc_scr[h] = alpha_scr[h] * acc_scr[h] + _dot(vt, p)

    @pl.when(jnp.logical_not(jnp.min(lmin) > UNDERFLOW_GUARD))
    def _():
        m_scr[...] = jnp.full(m_scr.shape, NEG, F32)
        acc_scr[...] = jnp.zeros(acc_scr.shape, F32)
        tie_scr[...] = jnp.zeros((1, TQ), F32)
        pipeline(stage_a, stage_b, s0_scr, s1_scr)

    for h in range(N_HEADS_A):
        a = acc_scr[h]
        o_ref[h * HEAD_DIM:(h + 1) * HEAD_DIM, :] = (a[:HEAD_DIM] / a[HEAD_DIM:HEAD_DIM + 1]).astype(BF16)


def _dsa_attention(qt, qit, wit, ki, k, vat, kn, b, s):
    n_sel = min(TOPK_MAX, s // 4)
    assert s % TQ == 0 and n_sel % TQ == 0
    nq = s // TQ
    qblk = lambda r: pl.BlockSpec((r, TQ), lambda bi, i: (0, bi * nq + i))
    tok = lambda c: pl.BlockSpec((s, c), lambda bi, i: (bi, 0))
    return pl.pallas_call(
        functools.partial(_dsa_kernel, seq=s, n_sel=n_sel),
        grid=(b, nq),
        in_specs=[qblk(qt.shape[0]), qblk(qit.shape[0]), qblk(wit.shape[0]),
                  tok(ki.shape[1]), tok(k.shape[1]),
                  pl.BlockSpec((vat.shape[0], s), lambda bi, i: (0, bi)),
                  pl.BlockSpec((kn.shape[0], s), lambda bi, i: (0, bi))],
        out_specs=qblk(qt.shape[0]),
        out_shape=jax.ShapeDtypeStruct(qt.shape, BF16),
        scratch_shapes=[
            pltpu.VMEM((s, TQ), F32),
            pltpu.VMEM((s, TQ), BF16),
            pltpu.VMEM((1, TQ), F32),
            pltpu.VMEM((1, TQ), F32),
            pltpu.VMEM((1, TQ), F32),
            pltpu.VMEM((TQ, TQ), BF16),
            pltpu.VMEM((N_HEADS_A, 1, TQ), F32),
            pltpu.VMEM((N_HEADS_A, 1, TQ), F32),
            pltpu.VMEM((N_HEADS_A, 1, TQ), F32),
            pltpu.VMEM((N_HEADS_A, VR, TQ), F32),
            pltpu.VMEM((N_HEADS_A, TQ, TQ), F32),
            pltpu.VMEM((N_HEADS_A, TQ, TQ), F32),
            pltpu.VMEM((N_HEADS_A, TQ, TQ), BF16),
            pltpu.VMEM((N_HEADS_A, TQ, TQ), BF16),
        ],
        compiler_params=_params("parallel", "arbitrary"),
        name="dsa_attention",
    )(qt, qit, wit, ki, k, vat, kn)


def _mem_attend(q_ref, k_ref, v_ref, o_ref, s_scr, m_scr):
    nsub = q_ref.shape[1] // MEM_SUB

    def stage_a(sb):
        ts = slice(sb * MEM_SUB, (sb + 1) * MEM_SUB)
        for h in range(N_MEM_HEADS):
            sl = slice(h * HEAD_DIM, (h + 1) * HEAD_DIM)
            s = _dot(k_ref[:, sl], q_ref[sl, ts])
            s_scr[sb, h] = s
            m_scr[sb, h] = jnp.max(s, axis=0, keepdims=True)

    def stage_b(sb):
        ts = slice(sb * MEM_SUB, (sb + 1) * MEM_SUB)
        for h in range(N_MEM_HEADS):
            sl = slice(h * HEAD_DIM, (h + 1) * HEAD_DIM)
            p = jnp.exp(s_scr[sb, h] - m_scr[sb, h])
            l = jnp.sum(p, axis=0, keepdims=True)
            o_ref[sl, ts] = (_dot(v_ref[sl, :], p.astype(BF16)) / l).astype(BF16)

    stage_a(0)
    for sb in range(1, nsub):
        stage_b(sb - 1)
        stage_a(sb)
    stage_b(nsub - 1)


MEM_SUB = 512


QB = 1024
BAND_QUERIES = 2048
LN2 = 0.6931471805599453


def _band_kernel(q_ref, kp_ref, kc_ref, vp_ref, vc_ref, o_ref, lse_ref, s_scr, m_scr, ot_scr, lt_scr, *, qb, ns):
    j = pl.program_id(1)
    nsub = qb // BLK
    krow = lax.broadcasted_iota(I32, (BLK, BLK), 0)
    qcol = lax.broadcasted_iota(I32, (BLK, BLK), 1)
    bias_prev = jnp.where(krow >= qcol, 0.0, NEG)
    bias_cur = jnp.where(krow <= qcol, 0.0, NEG)
    no_prev = jnp.where(j > 0, 0.0, NEG)
    lt_scr[...] = jnp.zeros(lt_scr.shape, F32)

    gw = HEADS_PER_DIL * HEAD_DIM

    def stage_a(u):
        sq, sb = divmod(u, nsub)
        qs = slice(sb * BLK, (sb + 1) * BLK)
        for h in range(HEADS_PER_DIL):
            hs = slice(sq * gw + h * HEAD_DIM, sq * gw + (h + 1) * HEAD_DIM)
            qh = q_ref[qs, hs]
            if sb == 0:
                s_p = _dot_nt(kp_ref[:, hs], qh) + (bias_prev + no_prev)
            else:
                s_p = _dot_nt(kc_ref[(sb - 1) * BLK:sb * BLK, hs], qh) + bias_prev
            s_c = _dot_nt(kc_ref[qs, hs], qh) + bias_cur
            s_scr[u, h, 0:BLK] = s_p
            s_scr[u, h, BLK:2 * BLK] = s_c
            m_scr[u, h] = jnp.maximum(jnp.max(s_p, axis=0, keepdims=True), jnp.max(s_c, axis=0, keepdims=True))

    def stage_b(u):
        sq, sb = divmod(u, nsub)
        qs = slice(sb * BLK, (sb + 1) * BLK)
        for h in range(HEADS_PER_DIL):
            hs = slice(sq * gw + h * HEAD_DIM, sq * gw + (h + 1) * HEAD_DIM)
            m = m_scr[u, h]
            p_p = jnp.exp2(s_scr[u, h, 0:BLK] - m)
            p_c = jnp.exp2(s_scr[u, h, BLK:2 * BLK] - m)
            l = jnp.sum(p_p, axis=0, keepdims=True) + jnp.sum(p_c, axis=0, keepdims=True)
            v_p = vp_ref[:, hs] if sb == 0 else vc_ref[(sb - 1) * BLK:sb * BLK, hs]
            acc = _dot_tn(v_p, p_p.astype(BF16)) + _dot_tn(vc_ref[qs, hs], p_c.astype(BF16))
            ot_scr[sq, h * HEAD_DIM:(h + 1) * HEAD_DIM, qs] = acc / l
            lt_scr[sq, h:h + 1, qs] = m * LN2 + jnp.log(l)

    units = ns * nsub
    groups = [range(g, min(g + 2, units)) for g in range(0, units, 2)]
    for u in groups[0]:
        stage_a(u)
    for prev, nxt in zip(groups[:-1], groups[1:]):
        for u in prev:
            stage_b(u)
        for u in nxt:
            stage_a(u)
    for u in groups[-1]:
        stage_b(u)
    for sq in range(ns):
        o_ref[:, sq * gw:(sq + 1) * gw] = ot_scr[sq].T.astype(o_ref.dtype)
        lse_ref[:, sq * LANES:(sq + 1) * LANES] = lt_scr[sq].T


def _band_attention(q, k, v, b, dil):
    rows, width = q.shape
    c = width // dil
    t = rows * dil
    n = t // (b * dil)
    qb = min(QB, n)
    assert n % qb == 0 and qb % BLK == 0
    rr, nj = qb // BLK, n // qb
    ns = min(dil, max(1, BAND_QUERIES // qb))
    assert dil % ns == 0
    gpb = dil // ns
    cur = lambda w: pl.BlockSpec((qb, ns * w), lambda sg, j: ((sg // gpb) * nj + j, sg % gpb))
    prev = pl.BlockSpec((BLK, ns * c),
                        lambda sg, j: ((sg // gpb) * (n // BLK) + jnp.maximum(j * rr - 1, 0), sg % gpb))
    return pl.pallas_call(
        functools.partial(_band_kernel, qb=qb, ns=ns),
        grid=(b * gpb, nj),
        in_specs=[cur(c), prev, cur(c), prev, cur(c)],
        out_specs=[cur(c), cur(LANES)],
        out_shape=[jax.ShapeDtypeStruct((rows, dil * c), BF16), jax.ShapeDtypeStruct((rows, dil * LANES), F32)],
        scratch_shapes=[pltpu.VMEM((ns * rr, HEADS_PER_DIL, 2 * BLK, BLK), F32),
                        pltpu.VMEM((ns * rr, HEADS_PER_DIL, 1, BLK), F32),
                        pltpu.VMEM((ns, c, qb), F32),
                        pltpu.VMEM((ns, LANES, qb), F32)],
        compiler_params=_params("parallel", "parallel"),
        name="band_attention",
    )(q, k, k, v, v)


def _merge_kernel(*refs):
    ng = len(DIL_PATTERNS)
    o_refs, l_refs, out_ref, o_scr, l_scr = refs[:ng], refs[ng:2 * ng], refs[2 * ng], refs[2 * ng + 1], refs[2 * ng + 2]
    tm, gw = out_ref.shape
    os_, lses = [], []
    for g, (_, dil) in enumerate(DIL_PATTERNS):
        if dil == 1:
            os_.append([o_refs[g][:, cb * LANES:(cb + 1) * LANES].astype(F32) for cb in range(gw // LANES)])
            lses.append(l_refs[g][...])
            continue
        for r in range(dil):
            rows = pl.ds(r, tm // dil, stride=dil)
            for cb in range(gw // LANES):
                o_scr[g, cb, rows, :] = o_refs[g][:, r * gw + cb * LANES:r * gw + (cb + 1) * LANES].astype(F32)
            l_scr[g, rows, :] = l_refs[g][:, r * LANES:(r + 1) * LANES]
        os_.append([o_scr[g, cb] for cb in range(gw // LANES)])
        lses.append(l_scr[g])
    m = functools.reduce(jnp.maximum, lses)
    es = [jnp.exp(l - m) for l in lses]
    den = sum(es)
    spread = jnp.where(lax.broadcasted_iota(I32, (LANES, gw), 1) // HEAD_DIM
                       == lax.broadcasted_iota(I32, (LANES, gw), 0), 1.0, 0.0).astype(BF16)
    ws = [_dot((e / den).astype(BF16), spread) for e in es]
    for cb in range(gw // LANES):
        cs = slice(cb * LANES, (cb + 1) * LANES)
        out_ref[:, cs] = sum(w[:, cs] * o[cb] for w, o in zip(ws, os_)).astype(out_ref.dtype)


def _merge_groups(os_, lses, t):
    gw = HEADS_PER_DIL * HEAD_DIM
    tm = min(1024, t)
    ng = len(DIL_PATTERNS)
    spec = lambda w: [pl.BlockSpec((tm // dil, dil * w), lambda i: (i, 0)) for _, dil in DIL_PATTERNS]
    return pl.pallas_call(
        _merge_kernel,
        grid=(t // tm,),
        in_specs=spec(gw) + spec(LANES),
        out_specs=pl.BlockSpec((tm, gw), lambda i: (i, 0)),
        out_shape=jax.ShapeDtypeStruct((t, gw), BF16),
        scratch_shapes=[pltpu.VMEM((ng, gw // LANES, tm, LANES), F32), pltpu.VMEM((ng, tm, LANES), F32)],
        compiler_params=_params("parallel"),
        name="merge_groups",
    )(*os_, *lses)


def _ffn_kernel(x_ref, mix_ref, mo_ref, wo1_ref, wo2_ref, g_ref, wgu_ref, wd_ref, gf_ref,
                o_ref, act_scr, *, final_norm, tf, mix_token_major):
    dff = wd_ref.shape[0]
    mixed = _dot(mix_ref[...], wo1_ref[...]) if mix_token_major else _dot_tn(mix_ref[...], wo1_ref[...])
    x2 = x_ref[...] + mixed + _dot_tn(mo_ref[...], wo2_ref[...])
    h = _rms(x2, g_ref[...]).astype(BF16)
    for f in range(dff // tf):
        gate = _dot(h, wgu_ref[:, f * tf:(f + 1) * tf])
        up = _dot(h, wgu_ref[:, dff + f * tf:dff + (f + 1) * tf])
        act_scr[:, f * tf:(f + 1) * tf] = (gate * jax.nn.sigmoid(gate) * up).astype(BF16)
    y = x2 + _dot(act_scr[...], wd_ref[...])
    if final_norm:
        y = _rms(y, gf_ref[...])
    o_ref[...] = y


def _out_ffn(x, mix, mo, w_out, g_ffn, w_gate_up, w_down, g_final, final_norm):
    t, d = x.shape
    mix_token_major = mix.shape[0] == t
    cm, cmo = mix.shape[1 if mix_token_major else 0], mo.shape[0]
    dff = w_down.shape[0]
    wo1 = w_out[:cm].astype(BF16)
    wo2 = w_out[cm:].astype(BF16)
    wgu = w_gate_up.astype(BF16)
    wd = w_down.astype(BF16)
    tm = min(512, t)
    tf = 256 if dff % 256 == 0 else dff
    row = lambda c: pl.BlockSpec((tm, c), lambda i: (i, 0))
    const = lambda r, c: pl.BlockSpec((r, c), lambda i: (0, 0), pipeline_mode=pl.Buffered(1))
    return pl.pallas_call(
        functools.partial(_ffn_kernel, final_norm=final_norm, tf=tf, mix_token_major=mix_token_major),
        grid=(t // tm,),
        in_specs=[row(d), row(cm) if mix_token_major else pl.BlockSpec((cm, tm), lambda i: (0, i)),
                  pl.BlockSpec((cmo, tm), lambda i: (0, i)),
                  const(cm, d), const(cmo, d), const(1, d),
                  const(d, 2 * dff), const(dff, d), const(1, d)],
        out_specs=row(d),
        out_shape=jax.ShapeDtypeStruct((t, d), F32),
        scratch_shapes=[pltpu.VMEM((tm, dff), BF16)],
        compiler_params=_params("parallel"),
        name="out_ffn",
    )(x, mix, mo, wo1, wo2, g_ffn.reshape(1, d), wgu, wd, g_final.reshape(1, d))


def kernel(x, mem, positions,
           l0_norm_mix, l0_norm_mem, l0_w_in, l0_w_mem_kv, l0_w_out, l0_norm_ffn, l0_w_gate_up, l0_w_down,
           l1_norm_mix, l1_norm_mem, l1_w_in, l1_w_mem_kv, l1_w_out, l1_norm_ffn, l1_w_gate_up, l1_w_down,
           final_norm):
    b, s, d = x.shape
    t = b * s
    xt = x.reshape(t, d)
    tabs = _trig_tables(positions.reshape(1, t))

    qt, k, vat, qit, ki, wit, mo, kn = _inproj_a(xt, l0_norm_mix, l0_w_in, tabs,
                                                 *_mem_kv(mem, l0_norm_mem, l0_w_mem_kv), s)
    mix = _dsa_attention(qt, qit, wit, ki, k, vat, kn, b, s)
    xt = _out_ffn(xt, mix, mo, l0_w_out, l0_norm_ffn, l0_w_gate_up, l0_w_down, final_norm, False)

    outs = _inproj_b(xt, l1_norm_mix, l1_w_in, tabs, *_mem_kv(mem, l1_norm_mem, l1_w_mem_kv), s)
    os_, lses = [], []
    for g, (window, dil) in enumerate(DIL_PATTERNS):
        assert window // dil == BLK
        o, lse = _band_attention(*outs[3 * g:3 * g + 3], b, dil)
        os_.append(o)
        lses.append(lse)
    mix = _merge_groups(os_, lses, t)
    xt = _out_ffn(xt, mix, outs[-1], l1_w_out, l1_norm_ffn, l1_w_gate_up, l1_w_down, final_norm, True)
    return xt.reshape(b, s, d)
```

```python
import functools

import jax
import jax.numpy as jnp
from jax import lax
from jax.experimental import pallas as pl
from jax.experimental.pallas import tpu as pltpu

F32 = jnp.float32
BF16 = jnp.bfloat16
I32 = jnp.int32

HEAD_DIM = 64
N_HEADS_A = 12
N_KV_A = 4
IDX_HEADS = 8
IDX_DIM = 64
IDX_ROPE_DIM = 32
TOPK_MAX = 256
DIL_PATTERNS = ((128, 1), (512, 4), (2048, 16))
HEADS_PER_DIL = 4
N_MEM_HEADS = 4
BLK = 128
ROPE_THETA = 10000.0
EPS = 1e-6
NEG = -1e30

LANES = 128
VMEM_LIMIT = 56 * 1024 * 1024

Q_SCALE = HEAD_DIM ** -0.5
WI_SCALE = IDX_HEADS ** -0.5 * IDX_DIM ** -0.5
LOG2E = 1.4426950408889634


def _dot(a, b):
    return jnp.dot(a, b, preferred_element_type=F32)


def _dot_nt(a, b):
    return lax.dot_general(a, b, (((1,), (1,)), ((), ())), preferred_element_type=F32)


def _dot_tn(a, b):
    return lax.dot_general(a, b, (((0,), (0,)), ((), ())), preferred_element_type=F32)


def _params(*sem):
    return pltpu.CompilerParams(dimension_semantics=sem, vmem_limit_bytes=VMEM_LIMIT)


def _rms(x, g):
    ms = jnp.mean(x * x, axis=-1, keepdims=True)
    return x * lax.rsqrt(ms + EPS) * g


H_HD = HEAD_DIM // 2
H_IX = IDX_ROPE_DIM // 2


def _trig_kernel(pos_ref, f_ref, chd_ref, shd_ref, cix_ref, six_ref):
    tm = pos_ref.shape[1]
    pos = pos_ref[...].astype(F32)
    f = jnp.concatenate([f_ref[...]] * (tm // LANES), axis=1)
    ang = f * pos
    c, s = jnp.cos(ang), jnp.sin(ang)
    chd_ref[...] = c[:H_HD]
    shd_ref[...] = s[:H_HD]
    cix_ref[...] = c[H_HD:]
    six_ref[...] = s[H_HD:]


def _trig_tables(pos_row):
    t = pos_row.shape[1]
    tm = min(2048, t)
    f_hd = ROPE_THETA ** (-jnp.arange(H_HD, dtype=F32) / H_HD)
    f_ix = ROPE_THETA ** (-jnp.arange(H_IX, dtype=F32) / H_IX)
    f = jnp.broadcast_to(jnp.concatenate([f_hd, f_ix])[:, None], (H_HD + H_IX, LANES))
    spec = lambda r: pl.BlockSpec((r, tm), lambda i: (0, i))
    rows = [H_HD, H_HD, H_IX, H_IX]
    return pl.pallas_call(
        _trig_kernel,
        grid=(t // tm,),
        in_specs=[spec(1), pl.BlockSpec((H_HD + H_IX, LANES), lambda i: (0, 0))],
        out_specs=[spec(r) for r in rows],
        out_shape=[jax.ShapeDtypeStruct((r, t), F32) for r in rows],
        compiler_params=_params("parallel"),
        name="rope_tables",
    )(pos_row, f)


def _norm_matmul_kernel(x_ref, g_ref, w_ref, o_ref):
    h = _rms(x_ref[...], g_ref[...]).astype(BF16)
    o_ref[...] = _dot(h, w_ref[...]).astype(o_ref.dtype)


def _norm_matmul(x, g, w, out_dtype):
    t, d = x.shape
    n = w.shape[1]
    tm = min(512, t)
    return pl.pallas_call(
        _norm_matmul_kernel,
        grid=(t // tm,),
        in_specs=[pl.BlockSpec((tm, d), lambda i: (i, 0)),
                  pl.BlockSpec((1, d), lambda i: (0, 0)),
                  pl.BlockSpec((d, n), lambda i: (0, 0))],
        out_specs=pl.BlockSpec((tm, n), lambda i: (i, 0)),
        out_shape=jax.ShapeDtypeStruct((t, n), out_dtype),
        compiler_params=_params("parallel"),
        name="norm_matmul",
    )(x, g.reshape(1, d), w)


def _mem_kv(mem, g, w_kv):
    b, m, d = mem.shape
    kv = _norm_matmul(mem.reshape(b * m, d), g, w_kv.astype(BF16), BF16).reshape(b, m, -1)
    nk = N_MEM_HEADS * HEAD_DIM
    return kv[:, :, :nk], kv[:, :, nk:].transpose(0, 2, 1)


VR = 80

A_Q, A_K, A_V, A_QI, A_KI, A_WI, A_QM, A_END = 0, 768, 1024, 1280, 1792, 1856, 1872, 2128


def _rope_heads(p, nheads, out_ref, half, c, s, scale):
    for hh in range(nheads):
        r0 = hh * HEAD_DIM
        x1, x2 = p[r0:r0 + half], p[r0 + half:r0 + 2 * half]
        out_ref[r0:r0 + half, :] = ((x1 * c - x2 * s) * scale).astype(out_ref.dtype)
        out_ref[r0 + half:r0 + 2 * half, :] = ((x2 * c + x1 * s) * scale).astype(out_ref.dtype)
        if 2 * half < HEAD_DIM:
            out_ref[r0 + 2 * half:r0 + HEAD_DIM, :] = (p[r0 + 2 * half:r0 + HEAD_DIM] * scale).astype(out_ref.dtype)


def _write_values(pv, nheads, va_ref):
    tm = pv.shape[1]
    ones_rows = jnp.where(lax.broadcasted_iota(I32, (VR - HEAD_DIM, tm), 0) == 0, 1.0, 0.0).astype(BF16)
    for g in range(nheads):
        va_ref[g * VR:g * VR + HEAD_DIM, :] = pv[g * HEAD_DIM:(g + 1) * HEAD_DIM].astype(BF16)
        va_ref[g * VR + HEAD_DIM:(g + 1) * VR, :] = ones_rows


def _inproj_a_kernel(x_ref, g_ref, wt_ref, pos_ref, f_ref,
                     q_ref, k_ref, va_ref, qi_ref, ki_ref, wi_ref, qm_ref, kn_ref,
                     chd_ref, shd_ref, cix_ref, six_ref, kt_scr):
    _trig_kernel(pos_ref, f_ref, chd_ref, shd_ref, cix_ref, six_ref)
    h = _rms(x_ref[...], g_ref[...]).astype(BF16)
    chd, shd = chd_ref[...], shd_ref[...]
    cix, six = cix_ref[...], six_ref[...]

    def proj(a, b):
        return _dot_nt(wt_ref[a:b, :], h)

    _rope_heads(proj(A_Q, A_K), N_HEADS_A, q_ref, H_HD, chd, shd, Q_SCALE * LOG2E)
    nk = N_KV_A * HEAD_DIM
    _rope_heads(proj(A_K, A_V), N_KV_A, kt_scr.at[0:nk], H_HD, chd, shd, 1.0)
    _write_values(proj(A_V, A_QI), N_KV_A, va_ref)
    _rope_heads(proj(A_QI, A_KI), IDX_HEADS, qi_ref, H_IX, cix, six, 1.0)
    pkw = proj(A_KI, A_QM)
    _rope_heads(pkw, 1, kt_scr.at[nk:nk + IDX_DIM], H_IX, cix, six, 1.0)
    k_ref[...] = kt_scr[0:nk, :].T.astype(BF16)
    ki_ref[...] = kt_scr[nk:nk + IDX_DIM, :].T.astype(BF16)
    for g in range(N_KV_A):
        kg = kt_scr[g * HEAD_DIM:(g + 1) * HEAD_DIM, :]
        kn_ref[g:g + 1, :] = jnp.sum(kg * kg, axis=0, keepdims=True)
    wi_ref[...] = pkw[IDX_DIM:IDX_DIM + IDX_HEADS] * WI_SCALE
    qm_ref[...] = (proj(A_QM, A_END) * Q_SCALE).astype(BF16)


def _inproj_a(x, g, w_in, pos_row):
    t, d = x.shape
    f_hd = ROPE_THETA ** (-jnp.arange(H_HD, dtype=F32) / H_HD)
    f_ix = ROPE_THETA ** (-jnp.arange(H_IX, dtype=F32) / H_IX)
    freqs = jnp.broadcast_to(jnp.concatenate([f_hd, f_ix])[:, None], (H_HD + H_IX, LANES))
    wt = w_in.T
    pad = jnp.zeros((A_QM - A_WI - IDX_HEADS, d), w_in.dtype)
    split = A_WI + IDX_HEADS
    wt = jnp.concatenate([wt[:split], pad, wt[split:]], axis=0).astype(BF16)
    tm = min(1024, t)
    col = lambda r: pl.BlockSpec((r, tm), lambda i: (0, i))
    outs = [(N_HEADS_A * HEAD_DIM, BF16), (N_KV_A * HEAD_DIM, BF16), (N_KV_A * VR, BF16),
            (IDX_HEADS * IDX_DIM, BF16), (IDX_DIM, BF16), (IDX_HEADS, F32), (N_MEM_HEADS * HEAD_DIM, BF16),
            (N_KV_A, F32), (H_HD, F32), (H_HD, F32), (H_IX, F32), (H_IX, F32)]
    return pl.pallas_call(
        _inproj_a_kernel,
        grid=(t // tm,),
        in_specs=[pl.BlockSpec((tm, d), lambda i: (i, 0)), pl.BlockSpec((1, d), lambda i: (0, 0)),
                  pl.BlockSpec((A_END, d), lambda i: (0, 0)),
                  col(1), pl.BlockSpec((H_HD + H_IX, LANES), lambda i: (0, 0))],
        out_specs=[pl.BlockSpec((tm, r), lambda i: (i, 0)) if k in (1, 4) else col(r)
                   for k, (r, _) in enumerate(outs)],
        out_shape=[jax.ShapeDtypeStruct((t, r) if k in (1, 4) else (r, t), dt) for k, (r, dt) in enumerate(outs)],
        scratch_shapes=[pltpu.VMEM((N_KV_A * HEAD_DIM + IDX_DIM, tm), F32)],
        compiler_params=_params("parallel"),
        name="inproj_a",
    )(x, g.reshape(1, d), wt, pos_row, freqs)


def _inproj_b_kernel(x_ref, g_ref, wt_ref, chd_ref, shd_ref, *refs):
    ng = len(DIL_PATTERNS)
    out_refs, (rope_scr, tok_scr) = refs[:3 * ng + 1], refs[3 * ng + 1:]
    h = _rms(x_ref[...], g_ref[...]).astype(BF16)
    chd, shd = chd_ref[...], shd_ref[...]
    gw = HEADS_PER_DIL * HEAD_DIM
    tm = h.shape[0]

    def emit(out_ref, slot, dil, value_t):
        tok = value_t.T
        if dil == 1:
            out_ref[...] = tok.astype(BF16)
        else:
            for cb in range(gw // LANES):
                tok_scr[slot, cb] = tok[:, cb * LANES:(cb + 1) * LANES]
            for r in range(dil):
                for cb in range(gw // LANES):
                    out_ref[:, r * gw + cb * LANES:r * gw + (cb + 1) * LANES] = (
                        tok_scr[slot, cb, pl.ds(r, tm // dil, stride=dil), :].astype(BF16))

    for g, (_, dil) in enumerate(DIL_PATTERNS):
        q_ref, k_ref, v_ref = out_refs[3 * g:3 * g + 3]
        base = 3 * g * gw
        _rope_heads(_dot_nt(wt_ref[base:base + gw, :], h), HEADS_PER_DIL, rope_scr.at[0], H_HD, chd, shd,
                    Q_SCALE * LOG2E)
        emit(q_ref, 0, dil, rope_scr[0])
        _rope_heads(_dot_nt(wt_ref[base + gw:base + 2 * gw, :], h), HEADS_PER_DIL, rope_scr.at[1], H_HD, chd, shd, 1.0)
        emit(k_ref, 1, dil, rope_scr[1])
        emit(v_ref, 2, dil, _dot_nt(wt_ref[base + 2 * gw:base + 3 * gw, :], h))
    out_refs[3 * ng][...] = (_dot_nt(wt_ref[3 * ng * gw:3 * ng * gw + N_MEM_HEADS * HEAD_DIM, :], h)
                            * Q_SCALE).astype(BF16)


def _inproj_b(x, g, w_in, tabs):
    t, d = x.shape
    wt = w_in.T.astype(BF16)
    tm = min(1024, t)
    col = lambda r: pl.BlockSpec((r, tm), lambda i: (0, i))
    gw = HEADS_PER_DIL * HEAD_DIM
    sub_specs, sub_shapes = [], []
    for _, dil in DIL_PATTERNS:
        assert tm % (16 * dil) == 0
        sub_specs += [pl.BlockSpec((tm // dil, dil * gw), lambda i: (i, 0))] * 3
        sub_shapes += [jax.ShapeDtypeStruct((t // dil, dil * gw), BF16)] * 3
    return pl.pallas_call(
        _inproj_b_kernel,
        grid=(t // tm,),
        in_specs=[pl.BlockSpec((tm, d), lambda i: (i, 0)), pl.BlockSpec((1, d), lambda i: (0, 0)),
                  pl.BlockSpec(wt.shape, lambda i: (0, 0)), col(H_HD), col(H_HD)],
        out_specs=sub_specs + [col(N_MEM_HEADS * HEAD_DIM)],
        out_shape=sub_shapes + [jax.ShapeDtypeStruct((N_MEM_HEADS * HEAD_DIM, t), BF16)],
        scratch_shapes=[pltpu.VMEM((2, gw, tm), F32), pltpu.VMEM((3, gw // LANES, tm, LANES), F32)],
        compiler_params=_params("parallel"),
        name="inproj_b",
    )(x, g.reshape(1, d), wt, tabs[0], tabs[1])


TQ = 256
CR = 32
COARSE_BITS = 16
FINE_BITS = 17
BOUND_SLACK = 1.01
UNDERFLOW_GUARD = 2.0 ** -100


def _key_to_f32(key):
    bits = jnp.where(key < 0, key ^ jnp.int32(0x7FFFFFFF), key)
    return pltpu.bitcast(bits, F32)


def _dsa_kernel(q_ref, qi_ref, wi_ref, ki_ref, k_ref, va_ref, kn_ref, o_ref,
                sc_scr, sc16_scr, tau_scr, need_scr, tie_scr, tri_scr, bound_scr, m_scr, alpha_scr, acc_scr, s0_scr, s1_scr, p0_scr, p1_scr,
                *, seq, n_sel):
    i = pl.program_id(1)
    nch = i + 1
    krow = lax.broadcasted_iota(I32, (TQ, TQ), 0)
    qcol = lax.broadcasted_iota(I32, (TQ, TQ), 1)

    def chunk_off(c):
        return pl.multiple_of(c * TQ, TQ)

    def score_chunk(c, diag):
        off = chunk_off(c)
        kic = ki_ref[pl.ds(off, TQ), :]
        sc = jnp.zeros((TQ, TQ), F32)
        for h in range(IDX_HEADS):
            lg = _dot(kic, qi_ref[h * IDX_DIM:(h + 1) * IDX_DIM, :])
            sc = sc + jnp.maximum(lg, 0.0) * wi_ref[h:h + 1, :]
        if diag:
            sc = jnp.where(krow > qcol, -jnp.inf, sc)
        sc_scr[pl.ds(off, TQ), :] = sc
        sc16_scr[pl.ds(off, TQ), :] = sc.astype(BF16)

    def score_pair(t, carry):
        score_chunk(2 * t, False)
        score_chunk(2 * t + 1, False)
        return carry

    lax.fori_loop(0, i // 2, score_pair, 0)

    @pl.when(i % 2 == 1)
    def _():
        score_chunk(i - 1, False)

    score_chunk(i, True)

    def over_chunks(one, init):
        acc = lax.fori_loop(0, nch // 2, lambda t, a: one(2 * t + 1, one(2 * t, a)), init)
        return lax.cond(nch % 2 == 1, lambda a: one(nch - 1, a), lambda a: a, acc)

    def count(pred):
        def one(c, acc):
            off = chunk_off(c)
            ind = pred(sc_scr[pl.ds(off, TQ), :], off)
            return acc + jnp.sum(ind.reshape(TQ // CR, CR, TQ), axis=0)
        return jnp.sum(over_chunks(one, jnp.zeros((CR, TQ), F32)), axis=0, keepdims=True)

    @pl.when(i * TQ < n_sel)
    def _():
        tau_scr[...] = jnp.full((1, TQ), -jnp.inf, F32)
        need_scr[...] = jnp.zeros((1, TQ), F32)

    @pl.when(i * TQ >= n_sel)
    def _():
        def body16(step, tau16):
            cand = tau16 + jnp.left_shift(jnp.int32(1), COARSE_BITS - 1 - step)
            bits = jnp.where(cand < 0, cand ^ jnp.int32(0x7FFF), cand)
            cand_f = pltpu.bitcast(jnp.left_shift(bits, COARSE_BITS), F32).astype(BF16)

            def one(c, acc):
                ind = jnp.where(sc16_scr[pl.ds(chunk_off(c), TQ), :] >= cand_f,
                                jnp.bfloat16(1.0), jnp.bfloat16(0.0))
                return functools.reduce(lambda a, k: a + ind[k * CR:(k + 1) * CR], range(TQ // CR), acc)
            acc = over_chunks(one, jnp.zeros((CR, TQ), BF16))
            ok = jnp.sum(acc.astype(F32), axis=0, keepdims=True) >= float(n_sel)
            return jnp.where(ok, cand, tau16)

        tau16 = lax.fori_loop(0, COARSE_BITS, body16, jnp.full((1, TQ), -(1 << (COARSE_BITS - 1)), I32))

        key16 = jnp.left_shift(tau16, COARSE_BITS) + jnp.where(tau16 < 0, (1 << COARSE_BITS) - 1, 0)
        lo = key16 - ((1 << (COARSE_BITS - 1)) + 1)

        def body(step, carry):
            delta, cge, crej = carry
            cand = delta + jnp.left_shift(jnp.int32(1), FINE_BITS - 1 - step)
            cand_f = _key_to_f32(lo + cand)
            cnt = count(lambda blk, _: jnp.where(blk >= cand_f, 1.0, 0.0))
            ok = cnt >= float(n_sel)
            return jnp.where(ok, cand, delta), jnp.where(ok, cnt, cge), jnp.where(ok, crej, cnt)

        init = (jnp.zeros((1, TQ), I32), jnp.full((1, TQ), 2.0 * n_sel, F32), jnp.zeros((1, TQ), F32))
        delta, cge, crej = lax.fori_loop(0, FINE_BITS, body, init)
        tau = lo + delta
        tau_scr[...] = _key_to_f32(tau)
        need_scr[...] = jnp.where(cge > float(n_sel), float(n_sel) - crej, 2.0 * seq)

    tri_scr[...] = jnp.where(krow >= qcol, 1.0, 0.0).astype(BF16)
    grp = N_HEADS_A // N_KV_A
    tau_f = tau_scr[...]
    need = need_scr[...]

    def chunk_select(c):
        blk = sc_scr[pl.ds(chunk_off(c), TQ), :]
        eq = blk == tau_f
        rank = _dot(tri_scr[...], jnp.where(eq, 1.0, 0.0).astype(BF16)) + tie_scr[...]
        tie_scr[...] = rank[TQ - 1:TQ, :]
        return jnp.where(eq, jnp.where(rank <= need, 1.0, 0.0), jnp.where(blk > tau_f, 1.0, 0.0))

    kmax2 = jnp.max(kn_ref[...], axis=1, keepdims=True)
    for h in range(N_HEADS_A):
        qf = q_ref[h * HEAD_DIM:(h + 1) * HEAD_DIM, :].astype(F32)
        qn2 = jnp.sum(qf * qf, axis=0, keepdims=True)
        bound_scr[h] = jnp.sqrt(qn2 * kmax2[h // grp:h // grp + 1]) * BOUND_SLACK
    acc_scr[...] = jnp.zeros(acc_scr.shape, F32)
    tie_scr[...] = jnp.zeros((1, TQ), F32)

    def pipeline(stage_a, stage_b, buf0, buf1):
        def step(c, src, dst):
            stage_b(c - 1, src)
            stage_a(c, dst)

        stage_a(0, buf0)

        def pair_body(t, carry):
            step(2 * t + 1, buf0, buf1)
            step(2 * t + 2, buf1, buf0)
            return carry

        lax.fori_loop(0, (nch - 1) // 2, pair_body, 0)

        @pl.when((nch - 1) % 2 == 1)
        def _():
            step(nch - 1, buf0, buf1)
            stage_b(nch - 1, buf1)

        @pl.when((nch - 1) % 2 == 0)
        def _():
            stage_b(nch - 1, buf0)

    def fast_a(c, p_dst):
        off = chunk_off(c)
        sel = chunk_select(c).astype(BF16)
        for g in range(N_KV_A):
            kc = k_ref[pl.ds(off, TQ), g * HEAD_DIM:(g + 1) * HEAD_DIM]
            for j in range(grp):
                h = g * grp + j
                s = _dot(kc, q_ref[h * HEAD_DIM:(h + 1) * HEAD_DIM, :])
                p_dst[h] = jnp.exp2(s - bound_scr[h]).astype(BF16) * sel

    def fast_b(c, p_src):
        off = chunk_off(c)
        for g in range(N_KV_A):
            vt = va_ref[g * VR:(g + 1) * VR, pl.ds(off, TQ)]
            for j in range(grp):
                h = g * grp + j
                acc_scr[h] += _dot(vt, p_src[h])

    pipeline(fast_a, fast_b, p0_scr, p1_scr)
    lmin = functools.reduce(jnp.minimum, [acc_scr[h, HEAD_DIM:HEAD_DIM + 1, :] for h in range(N_HEADS_A)])

    def stage_a(c, s_dst):
        off = chunk_off(c)
        bias = (chunk_select(c) - 1.0) * (-NEG)
        for g in range(N_KV_A):
            kc = k_ref[pl.ds(off, TQ), g * HEAD_DIM:(g + 1) * HEAD_DIM]
            for j in range(grp):
                h = g * grp + j
                s = _dot(kc, q_ref[h * HEAD_DIM:(h + 1) * HEAD_DIM, :]) + bias
                s_dst[h] = s
                m_old = m_scr[h]
                m_new = jnp.maximum(m_old, jnp.max(s, axis=0, keepdims=True))
                alpha_scr[h] = jnp.exp2(m_old - m_new)
                m_scr[h] = m_new

    def stage_b(c, s_src):
        off = chunk_off(c)
        for g in range(N_KV_A):
            vt = va_ref[g * VR:(g + 1) * VR, pl.ds(off, TQ)]
            for j in range(grp):
                h = g * grp + j
                p = jnp.exp2(s_src[h] - m_scr[h]).astype(BF16)
                acc_scr[h] = alpha_scr[h] * acc_scr[h] + _dot(vt, p)

    @pl.when(jnp.logical_not(jnp.min(lmin) > UNDERFLOW_GUARD))
    def _():
        m_scr[...] = jnp.full(m_scr.shape, NEG, F32)
        acc_scr[...] = jnp.zeros(acc_scr.shape, F32)
        tie_scr[...] = jnp.zeros((1, TQ), F32)
        pipeline(stage_a, stage_b, s0_scr, s1_scr)

    for h in range(N_HEADS_A):
        a = acc_scr[h]
        o_ref[h * HEAD_DIM:(h + 1) * HEAD_DIM, :] = (a[:HEAD_DIM] / a[HEAD_DIM:HEAD_DIM + 1]).astype(BF16)


def _dsa_attention(qt, qit, wit, ki, k, vat, kn, b, s):
    n_sel = min(TOPK_MAX, s // 4)
    assert s % TQ == 0 and n_sel % TQ == 0
    nq = s // TQ
    qblk = lambda r: pl.BlockSpec((r, TQ), lambda bi, i: (0, bi * nq + i))
    tok = lambda c: pl.BlockSpec((s, c), lambda bi, i: (bi, 0))
    return pl.pallas_call(
        functools.partial(_dsa_kernel, seq=s, n_sel=n_sel),
        grid=(b, nq),
        in_specs=[qblk(qt.shape[0]), qblk(qit.shape[0]), qblk(wit.shape[0]),
                  tok(ki.shape[1]), tok(k.shape[1]),
                  pl.BlockSpec((vat.shape[0], s), lambda bi, i: (0, bi)),
                  pl.BlockSpec((kn.shape[0], s), lambda bi, i: (0, bi))],
        out_specs=qblk(qt.shape[0]),
        out_shape=jax.ShapeDtypeStruct(qt.shape, BF16),
        scratch_shapes=[
            pltpu.VMEM((s, TQ), F32),
            pltpu.VMEM((s, TQ), BF16),
            pltpu.VMEM((1, TQ), F32),
            pltpu.VMEM((1, TQ), F32),
            pltpu.VMEM((1, TQ), F32),
            pltpu.VMEM((TQ, TQ), BF16),
            pltpu.VMEM((N_HEADS_A, 1, TQ), F32),
            pltpu.VMEM((N_HEADS_A, 1, TQ), F32),
            pltpu.VMEM((N_HEADS_A, 1, TQ), F32),
            pltpu.VMEM((N_HEADS_A, VR, TQ), F32),
            pltpu.VMEM((N_HEADS_A, TQ, TQ), F32),
            pltpu.VMEM((N_HEADS_A, TQ, TQ), F32),
            pltpu.VMEM((N_HEADS_A, TQ, TQ), BF16),
            pltpu.VMEM((N_HEADS_A, TQ, TQ), BF16),
        ],
        compiler_params=_params("parallel", "arbitrary"),
        name="dsa_attention",
    )(qt, qit, wit, ki, k, vat, kn)


def _mem_attn_kernel(q_ref, k_ref, v_ref, o_ref, s_scr, m_scr):
    nsub = q_ref.shape[1] // MEM_SUB

    def stage_a(sb):
        ts = slice(sb * MEM_SUB, (sb + 1) * MEM_SUB)
        for h in range(N_MEM_HEADS):
            sl = slice(h * HEAD_DIM, (h + 1) * HEAD_DIM)
            s = _dot(k_ref[:, sl], q_ref[sl, ts])
            s_scr[sb, h] = s
            m_scr[sb, h] = jnp.max(s, axis=0, keepdims=True)

    def stage_b(sb):
        ts = slice(sb * MEM_SUB, (sb + 1) * MEM_SUB)
        for h in range(N_MEM_HEADS):
            sl = slice(h * HEAD_DIM, (h + 1) * HEAD_DIM)
            p = jnp.exp(s_scr[sb, h] - m_scr[sb, h])
            l = jnp.sum(p, axis=0, keepdims=True)
            o_ref[sl, ts] = (_dot(v_ref[sl, :], p.astype(BF16)) / l).astype(BF16)

    stage_a(0)
    for sb in range(1, nsub):
        stage_b(sb - 1)
        stage_a(sb)
    stage_b(nsub - 1)


MEM_SUB = 512


def _mem_attention(qmt, km, vmt, s):
    c, t = qmt.shape
    m = km.shape[1]
    tm = min(2048, s)
    assert tm % MEM_SUB == 0
    nq = s // tm
    blk = pl.BlockSpec((c, tm), lambda bi, i: (0, bi * nq + i))
    return pl.pallas_call(
        _mem_attn_kernel,
        grid=(t // s, nq),
        in_specs=[blk,
                  pl.BlockSpec((None, m, km.shape[2]), lambda bi, i: (bi, 0, 0)),
                  pl.BlockSpec((None, vmt.shape[1], m), lambda bi, i: (bi, 0, 0))],
        out_specs=blk,
        out_shape=jax.ShapeDtypeStruct((c, t), BF16),
        scratch_shapes=[pltpu.VMEM((tm // MEM_SUB, N_MEM_HEADS, m, MEM_SUB), F32),
                        pltpu.VMEM((tm // MEM_SUB, N_MEM_HEADS, 1, MEM_SUB), F32)],
        compiler_params=_params("parallel", "parallel"),
        name="mem_attention",
    )(qmt, km, vmt)


QB = 1024
BAND_QUERIES = 2048
LN2 = 0.6931471805599453


def _band_kernel(q_ref, kp_ref, kc_ref, vp_ref, vc_ref, o_ref, lse_ref, s_scr, m_scr, ot_scr, lt_scr, *, qb, ns):
    j = pl.program_id(1)
    nsub = qb // BLK
    krow = lax.broadcasted_iota(I32, (BLK, BLK), 0)
    qcol = lax.broadcasted_iota(I32, (BLK, BLK), 1)
    bias_prev = jnp.where(krow >= qcol, 0.0, NEG)
    bias_cur = jnp.where(krow <= qcol, 0.0, NEG)
    no_prev = jnp.where(j > 0, 0.0, NEG)
    lt_scr[...] = jnp.zeros(lt_scr.shape, F32)

    gw = HEADS_PER_DIL * HEAD_DIM

    def stage_a(u):
        sq, sb = divmod(u, nsub)
        qs = slice(sb * BLK, (sb + 1) * BLK)
        for h in range(HEADS_PER_DIL):
            hs = slice(sq * gw + h * HEAD_DIM, sq * gw + (h + 1) * HEAD_DIM)
            qh = q_ref[qs, hs]
            if sb == 0:
                s_p = _dot_nt(kp_ref[:, hs], qh) + (bias_prev + no_prev)
            else:
                s_p = _dot_nt(kc_ref[(sb - 1) * BLK:sb * BLK, hs], qh) + bias_prev
            s_c = _dot_nt(kc_ref[qs, hs], qh) + bias_cur
            s_scr[u, h, 0:BLK] = s_p
            s_scr[u, h, BLK:2 * BLK] = s_c
            m_scr[u, h] = jnp.maximum(jnp.max(s_p, axis=0, keepdims=True), jnp.max(s_c, axis=0, keepdims=True))

    def stage_b(u):
        sq, sb = divmod(u, nsub)
        qs = slice(sb * BLK, (sb + 1) * BLK)
        for h in range(HEADS_PER_DIL):
            hs = slice(sq * gw + h * HEAD_DIM, sq * gw + (h + 1) * HEAD_DIM)
            m = m_scr[u, h]
            p_p = jnp.exp2(s_scr[u, h, 0:BLK] - m)
            p_c = jnp.exp2(s_scr[u, h, BLK:2 * BLK] - m)
            l = jnp.sum(p_p, axis=0, keepdims=True) + jnp.sum(p_c, axis=0, keepdims=True)
            v_p = vp_ref[:, hs] if sb == 0 else vc_ref[(sb - 1) * BLK:sb * BLK, hs]
            acc = _dot_tn(v_p, p_p.astype(BF16)) + _dot_tn(vc_ref[qs, hs], p_c.astype(BF16))
            ot_scr[sq, h * HEAD_DIM:(h + 1) * HEAD_DIM, qs] = acc / l
            lt_scr[sq, h:h + 1, qs] = m * LN2 + jnp.log(l)

    units = ns * nsub
    groups = [range(g, min(g + 2, units)) for g in range(0, units, 2)]
    for u in groups[0]:
        stage_a(u)
    for prev, nxt in zip(groups[:-1], groups[1:]):
        for u in prev:
            stage_b(u)
        for u in nxt:
            stage_a(u)
    for u in groups[-1]:
        stage_b(u)
    for sq in range(ns):
        o_ref[:, sq * gw:(sq + 1) * gw] = ot_scr[sq].T.astype(o_ref.dtype)
        lse_ref[:, sq * LANES:(sq + 1) * LANES] = lt_scr[sq].T


def _band_attention(q, k, v, b, dil):
    rows, width = q.shape
    c = width // dil
    t = rows * dil
    n = t // (b * dil)
    qb = min(QB, n)
    assert n % qb == 0 and qb % BLK == 0
    rr, nj = qb // BLK, n // qb
    ns = min(dil, max(1, BAND_QUERIES // qb))
    assert dil % ns == 0
    gpb = dil // ns
    cur = lambda w: pl.BlockSpec((qb, ns * w), lambda sg, j: ((sg // gpb) * nj + j, sg % gpb))
    prev = pl.BlockSpec((BLK, ns * c),
                        lambda sg, j: ((sg // gpb) * (n // BLK) + jnp.maximum(j * rr - 1, 0), sg % gpb))
    return pl.pallas_call(
        functools.partial(_band_kernel, qb=qb, ns=ns),
        grid=(b * gpb, nj),
        in_specs=[cur(c), prev, cur(c), prev, cur(c)],
        out_specs=[cur(c), cur(LANES)],
        out_shape=[jax.ShapeDtypeStruct((rows, dil * c), BF16), jax.ShapeDtypeStruct((rows, dil * LANES), F32)],
        scratch_shapes=[pltpu.VMEM((ns * rr, HEADS_PER_DIL, 2 * BLK, BLK), F32),
                        pltpu.VMEM((ns * rr, HEADS_PER_DIL, 1, BLK), F32),
                        pltpu.VMEM((ns, c, qb), F32),
                        pltpu.VMEM((ns, LANES, qb), F32)],
        compiler_params=_params("parallel", "parallel"),
        name="band_attention",
    )(q, k, k, v, v)


def _merge_kernel(*refs):
    ng = len(DIL_PATTERNS)
    o_refs, l_refs, out_ref, o_scr, l_scr = refs[:ng], refs[ng:2 * ng], refs[2 * ng], refs[2 * ng + 1], refs[2 * ng + 2]
    tm, gw = out_ref.shape
    os_, lses = [], []
    for g, (_, dil) in enumerate(DIL_PATTERNS):
        if dil == 1:
            os_.append([o_refs[g][:, cb * LANES:(cb + 1) * LANES].astype(F32) for cb in range(gw // LANES)])
            lses.append(l_refs[g][...])
            continue
        for r in range(dil):
            rows = pl.ds(r, tm // dil, stride=dil)
            for cb in range(gw // LANES):
                o_scr[g, cb, rows, :] = o_refs[g][:, r * gw + cb * LANES:r * gw + (cb + 1) * LANES].astype(F32)
            l_scr[g, rows, :] = l_refs[g][:, r * LANES:(r + 1) * LANES]
        os_.append([o_scr[g, cb] for cb in range(gw // LANES)])
        lses.append(l_scr[g])
    m = functools.reduce(jnp.maximum, lses)
    es = [jnp.exp(l - m) for l in lses]
    den = sum(es)
    spread = jnp.where(lax.broadcasted_iota(I32, (LANES, gw), 1) // HEAD_DIM
                       == lax.broadcasted_iota(I32, (LANES, gw), 0), 1.0, 0.0).astype(BF16)
    ws = [_dot((e / den).astype(BF16), spread) for e in es]
    for cb in range(gw // LANES):
        cs = slice(cb * LANES, (cb + 1) * LANES)
        out_ref[:, cs] = sum(w[:, cs] * o[cb] for w, o in zip(ws, os_)).astype(out_ref.dtype)


def _merge_groups(os_, lses, t):
    gw = HEADS_PER_DIL * HEAD_DIM
    tm = min(1024, t)
    ng = len(DIL_PATTERNS)
    spec = lambda w: [pl.BlockSpec((tm // dil, dil * w), lambda i: (i, 0)) for _, dil in DIL_PATTERNS]
    return pl.pallas_call(
        _merge_kernel,
        grid=(t // tm,),
        in_specs=spec(gw) + spec(LANES),
        out_specs=pl.BlockSpec((tm, gw), lambda i: (i, 0)),
        out_shape=jax.ShapeDtypeStruct((t, gw), BF16),
        scratch_shapes=[pltpu.VMEM((ng, gw // LANES, tm, LANES), F32), pltpu.VMEM((ng, tm, LANES), F32)],
        compiler_params=_params("parallel"),
        name="merge_groups",
    )(*os_, *lses)


def _ffn_kernel(x_ref, mix_ref, mo_ref, wo1_ref, wo2_ref, g_ref, wgu_ref, wd_ref, gf_ref,
                o_ref, act_scr, *, final_norm, tf, mix_token_major):
    dff = wd_ref.shape[0]
    mixed = _dot(mix_ref[...], wo1_ref[...]) if mix_token_major else _dot_tn(mix_ref[...], wo1_ref[...])
    x2 = x_ref[...] + mixed + _dot_tn(mo_ref[...], wo2_ref[...])
    h = _rms(x2, g_ref[...]).astype(BF16)
    for f in range(dff // tf):
        gate = _dot(h, wgu_ref[:, f * tf:(f + 1) * tf])
        up = _dot(h, wgu_ref[:, dff + f * tf:dff + (f + 1) * tf])
        act_scr[:, f * tf:(f + 1) * tf] = (gate * jax.nn.sigmoid(gate) * up).astype(BF16)
    y = x2 + _dot(act_scr[...], wd_ref[...])
    if final_norm:
        y = _rms(y, gf_ref[...])
    o_ref[...] = y


def _out_ffn(x, mix, mo, w_out, g_ffn, w_gate_up, w_down, g_final, final_norm):
    t, d = x.shape
    mix_token_major = mix.shape[0] == t
    cm, cmo = mix.shape[1 if mix_token_major else 0], mo.shape[0]
    dff = w_down.shape[0]
    wo1 = w_out[:cm].astype(BF16)
    wo2 = w_out[cm:].astype(BF16)
    wgu = w_gate_up.astype(BF16)
    wd = w_down.astype(BF16)
    tm = min(512, t)
    tf = 256 if dff % 256 == 0 else dff
    row = lambda c: pl.BlockSpec((tm, c), lambda i: (i, 0))
    const = lambda r, c: pl.BlockSpec((r, c), lambda i: (0, 0), pipeline_mode=pl.Buffered(1))
    return pl.pallas_call(
        functools.partial(_ffn_kernel, final_norm=final_norm, tf=tf, mix_token_major=mix_token_major),
        grid=(t // tm,),
        in_specs=[row(d), row(cm) if mix_token_major else pl.BlockSpec((cm, tm), lambda i: (0, i)),
                  pl.BlockSpec((cmo, tm), lambda i: (0, i)),
                  const(cm, d), const(cmo, d), const(1, d),
                  const(d, 2 * dff), const(dff, d), const(1, d)],
        out_specs=row(d),
        out_shape=jax.ShapeDtypeStruct((t, d), F32),
        scratch_shapes=[pltpu.VMEM((tm, dff), BF16)],
        compiler_params=_params("parallel"),
        name="out_ffn",
    )(x, mix, mo, wo1, wo2, g_ffn.reshape(1, d), wgu, wd, g_final.reshape(1, d))


def kernel(x, mem, positions,
           l0_norm_mix, l0_norm_mem, l0_w_in, l0_w_mem_kv, l0_w_out, l0_norm_ffn, l0_w_gate_up, l0_w_down,
           l1_norm_mix, l1_norm_mem, l1_w_in, l1_w_mem_kv, l1_w_out, l1_norm_ffn, l1_w_gate_up, l1_w_down,
           final_norm):
    b, s, d = x.shape
    t = b * s
    xt = x.reshape(t, d)

    qt, k, vat, qit, ki, wit, qmt, kn, *tabs = _inproj_a(xt, l0_norm_mix, l0_w_in, positions.reshape(1, t))
    mix = _dsa_attention(qt, qit, wit, ki, k, vat, kn, b, s)
    mo = _mem_attention(qmt, *_mem_kv(mem, l0_norm_mem, l0_w_mem_kv), s)
    xt = _out_ffn(xt, mix, mo, l0_w_out, l0_norm_ffn, l0_w_gate_up, l0_w_down, final_norm, False)

    outs = _inproj_b(xt, l1_norm_mix, l1_w_in, tabs)
    os_, lses = [], []
    for g, (window, dil) in enumerate(DIL_PATTERNS):
        assert window // dil == BLK
        o, lse = _band_attention(*outs[3 * g:3 * g + 3], b, dil)
        os_.append(o)
        lses.append(lse)
    mix = _merge_groups(os_, lses, t)
    mo = _mem_attention(outs[-1], *_mem_kv(mem, l1_norm_mem, l1_w_mem_kv), s)
    xt = _out_ffn(xt, mix, mo, l1_w_out, l1_norm_ffn, l1_w_gate_up, l1_w_down, final_norm, True)
    return xt.reshape(b, s, d)
```
